```python
import math
import jax
import jax.numpy as jnp
from jax import lax
import numpy as np

D_MODEL = 2048
BATCH = 32
SEQ = 256
DEPTH = 2
DEC_BATCH = 4
DEC_SEQ = 2048
PAST_LEN = 256

F32 = jnp.float32
GRID_W = 64
N_EVEN = (DEPTH + 1) // 2
N_ODD = DEPTH // 2
N_HEADS = 8
N_KV_HEADS = 2
HEAD_DIM = 128
Q_PER_KV = N_HEADS // N_KV_HEADS
D_ATTN = N_HEADS * HEAD_DIM
D_KV = N_KV_HEADS * HEAD_DIM
WINDOW = 128
ATTN_BLOCK = 128
ROPE_BASE = 10000.0
D_RNN = D_MODEL // 2
RNN_BLOCKS = 8
RNN_BLOCK_W = D_RNN // RNN_BLOCKS
RNN_CONV = 4
RG_LRU_C = 8.0
D_IN_EVEN = D_ATTN + 2 * D_KV + 2 * D_RNN
EVEN_SPLITS = [D_ATTN, D_ATTN + D_KV, D_ATTN + 2 * D_KV, D_ATTN + 2 * D_KV + D_RNN]
D_HYENA = D_MODEL
HYENA_ORDER = 2
HYENA_CONV = 3
HYENA_EMB = 33
HYENA_FILTER_W = 64
HYENA_DECAY_TARGET = 1e-2
HYENA_MIN_DECAY = math.log(HYENA_DECAY_TARGET) / 1.5
HYENA_MAX_DECAY = math.log(HYENA_DECAY_TARGET) / 0.3
D_FF = 5632
N_EXPERTS = 8
TOP_K = 2
D_FF_EXPERT = 7168
MOE_BLOCK = 128
EPS = 1e-6
NEG_INF = -1e30

kernel_name = 'hybrid_flow_prefix_step'


def rms_norm(x, g):
    xf = x.astype(F32)
    y = xf * lax.rsqrt(jnp.mean(xf * xf, axis=-1, keepdims=True) + EPS)
    return (y * g.astype(F32)).astype(x.dtype)


def ada_params(cond, w_mod, b_mod):
    m = jax.nn.silu(cond) @ w_mod + b_mod
    return [t[:, None, :] for t in jnp.split(m, 6, axis=-1)]


def modulate(x, g, shift, scale):
    return rms_norm(x, g) * (1.0 + scale) + shift


def dwconv_centred(x, w, b):
    K = w.shape[0]
    L = x.shape[1]
    left = K // 2
    xp = jnp.pad(x, ((0, 0), (left, K - 1 - left), (0, 0)))
    y = b
    for j in range(K):
        y = y + xp[:, j:j + L] * w[j]
    return y


def axial_rope(x):
    L = x.shape[1]
    rows = L // GRID_W
    row = jnp.repeat(jnp.arange(rows), GRID_W)
    col = jnp.tile(jnp.arange(GRID_W), rows)
    half = HEAD_DIM // 2
    quarter = half // 2
    inv_freq = ROPE_BASE ** (-jnp.arange(quarter, dtype=F32) / quarter)

    def rot(xa, pos):
        ang = pos.astype(F32)[:, None] * inv_freq
        cos = jnp.cos(ang)[None, :, None, :]
        sin = jnp.sin(ang)[None, :, None, :]
        xf = xa.astype(F32)
        x1, x2 = xf[..., :quarter], xf[..., quarter:]
        return jnp.concatenate([x1 * cos - x2 * sin, x2 * cos + x1 * sin], axis=-1)

    return jnp.concatenate([rot(x[..., :half], row), rot(x[..., half:], col)], axis=-1).astype(x.dtype)


def attn_context(q, k, v, sink):
    B, S = q.shape[:2]
    qg = q.reshape(B, S, N_KV_HEADS, Q_PER_KV, HEAD_DIM)
    s = jnp.einsum('bqkgd,bskd->bkgqs', qg, k).astype(F32) * (HEAD_DIM ** -0.5)
    sink_l = jnp.broadcast_to(sink.astype(F32).reshape(1, N_KV_HEADS, Q_PER_KV, 1, 1), s.shape[:-1] + (1,))
    p = jax.nn.softmax(jnp.concatenate([s, sink_l], axis=-1), axis=-1)[..., :S].astype(v.dtype)
    o = jnp.einsum('bkgqs,bskd->bqkgd', p, v)
    return o.reshape(B, S, D_ATTN)


def attn_latent(q, k, v, k_ctx, v_ctx, sink):
    B, L = q.shape[:2]
    nb = L // ATTN_BLOCK
    P = k_ctx.shape[1]
    qb = q.reshape(B, nb, ATTN_BLOCK, N_KV_HEADS, Q_PER_KV, HEAD_DIM)

    def band(t):
        tp = jnp.pad(t, ((0, 0), (ATTN_BLOCK, ATTN_BLOCK), (0, 0), (0, 0)))
        parts = [tp[:, i * ATTN_BLOCK:i * ATTN_BLOCK + L].reshape(B, nb, ATTN_BLOCK, N_KV_HEADS, HEAD_DIM) for i in range(3)]
        return jnp.concatenate(parts, axis=2)

    kw, vw = band(k), band(v)
    qpos = jnp.arange(nb)[:, None, None] * ATTN_BLOCK + jnp.arange(ATTN_BLOCK)[None, :, None]
    kpos = jnp.arange(nb)[:, None, None] * ATTN_BLOCK - ATTN_BLOCK + jnp.arange(3 * ATTN_BLOCK)[None, None, :]
    mask = (jnp.abs(kpos - qpos) <= WINDOW) & (kpos >= 0) & (kpos < L)
    scale = HEAD_DIM ** -0.5
    s_win = jnp.einsum('bnqkgd,bnskd->bnkgqs', qb, kw).astype(F32) * scale
    s_win = jnp.where(mask[None, :, None, None], s_win, NEG_INF)
    s_ctx = jnp.einsum('bnqkgd,bpkd->bnkgqp', qb, k_ctx).astype(F32) * scale
    sink_l = jnp.broadcast_to(sink.astype(F32).reshape(1, 1, N_KV_HEADS, Q_PER_KV, 1, 1), s_win.shape[:-1] + (1,))
    p = jax.nn.softmax(jnp.concatenate([s_win, s_ctx, sink_l], axis=-1), axis=-1)
    nw = 3 * ATTN_BLOCK
    p_win = p[..., :nw].astype(v.dtype)
    p_ctx = p[..., nw:nw + P].astype(v.dtype)
    o = jnp.einsum('bnkgqs,bnskd->bnqkgd', p_win, vw) + jnp.einsum('bnkgqp,bpkd->bnqkgd', p_ctx, v_ctx)
    return o.reshape(B, L, D_ATTN)


def linear_scan(a, b, h0, reverse):
    def combine(left, right):
        a_l, b_l = left
        a_r, b_r = right
        return a_l * a_r, a_r * b_l + b_r

    a_cum, b_cum = lax.associative_scan(combine, (a, b), axis=1, reverse=reverse)
    return a_cum * h0[:, None, :] + b_cum


def rg_lru(xr, yr, h0, conv_w, conv_b, w_a, b_a, w_x, b_x, lam):
    B, L, _ = xr.shape
    xc = dwconv_centred(xr, conv_w, conv_b).astype(F32)
    xb = xc.reshape(B, L, RNN_BLOCKS, RNN_BLOCK_W)
    gate_a = jnp.einsum('blnw,enwv->eblnv', xb, w_a.astype(F32)).reshape(2, B, L, D_RNN) + b_a.astype(F32)[:, None, None, :]
    gate_x = jnp.einsum('blnw,enwv->eblnv', xb, w_x.astype(F32)).reshape(2, B, L, D_RNN) + b_x.astype(F32)[:, None, None, :]
    log_a = -RG_LRU_C * jax.nn.sigmoid(gate_a) * jax.nn.softplus(-lam.astype(F32))[:, None, None, :]
    a = jnp.exp(log_a)
    inp = jnp.sqrt(-jnp.expm1(2.0 * log_a)) * jax.nn.sigmoid(gate_x) * xc[None]
    h0f = h0.astype(F32)
    h_fwd = linear_scan(a[0], inp[0], h0f[:, 0], reverse=False)
    h_bwd = linear_scan(a[1], inp[1], h0f[:, 1], reverse=True)
    out = (h_fwd + h_bwd) * jax.nn.gelu(yr.astype(F32))
    final = jnp.stack([h_fwd[:, -1], h_bwd[:, 0]], axis=1)
    return out.astype(xr.dtype), final


def even_mixer_context(h, w_in, w_out, conv_w, conv_b, w_a, b_a, w_x, b_x, lam, sink):
    B, S, _ = h.shape
    q, k, v, xr, yr = jnp.split(h @ w_in, EVEN_SPLITS, axis=-1)
    q = q.reshape(B, S, N_HEADS, HEAD_DIM)
    k = k.reshape(B, S, N_KV_HEADS, HEAD_DIM)
    v = v.reshape(B, S, N_KV_HEADS, HEAD_DIM)
    o_attn = attn_context(q, k, v, sink)
    o_rnn, state = rg_lru(xr, yr, jnp.zeros((B, 2, D_RNN), F32), conv_w, conv_b, w_a, b_a, w_x, b_x, lam)
    out = jnp.concatenate([o_attn, o_rnn], axis=-1) @ w_out
    return out, k, v, state.astype(h.dtype)


def even_mixer_latent(h, k_ctx, v_ctx, h0, w_in, w_out, conv_w, conv_b, w_a, b_a, w_x, b_x, lam, sink):
    B, L, _ = h.shape
    q, k, v, xr, yr = jnp.split(h @ w_in, EVEN_SPLITS, axis=-1)
    q = axial_rope(q.reshape(B, L, N_HEADS, HEAD_DIM))
    k = axial_rope(k.reshape(B, L, N_KV_HEADS, HEAD_DIM))
    v = v.reshape(B, L, N_KV_HEADS, HEAD_DIM)
    o_attn = attn_latent(q, k, v, k_ctx.astype(h.dtype), v_ctx.astype(h.dtype), sink)
    o_rnn, _ = rg_lru(xr, yr, h0, conv_w, conv_b, w_a, b_a, w_x, b_x, lam)
    return jnp.concatenate([o_attn, o_rnn], axis=-1) @ w_out


def hyena_filter_spectra(L, f_w1, f_b1, f_w2, f_b2, f_freq, f_w3):
    t = jnp.linspace(0.0, 1.0, L, dtype=F32)[:, None]
    bands = (HYENA_EMB - 1) // 2
    w_ang = 2.0 * math.pi * jnp.arange(L, dtype=F32)[:, None] / L
    f = jnp.linspace(1e-4, bands - 1, bands, dtype=F32)[None, :]
    z = jnp.concatenate([t, jnp.cos(f * w_ang), -jnp.sin(f * w_ang)], axis=-1)
    hid = jnp.sin(f_freq[0].astype(F32) * (z @ f_w1.astype(F32) + f_b1.astype(F32)))
    hid = jnp.sin(f_freq[1].astype(F32) * (hid @ f_w2.astype(F32) + f_b2.astype(F32)))
    filt = (hid @ f_w3.astype(F32)).reshape(L, HYENA_ORDER, 2, D_HYENA)
    deltas = jnp.abs(jnp.linspace(HYENA_MIN_DECAY, HYENA_MAX_DECAY, D_HYENA, dtype=F32))
    filt = filt * jnp.exp(-t[:, :, None, None] * deltas)
    fwd, bwd = filt[:, :, 0], filt[:, :, 1]
    circ = jnp.concatenate([fwd, jnp.zeros((1, HYENA_ORDER, D_HYENA), F32), bwd[1:][::-1]], axis=0)
    return jnp.fft.rfft(circ, axis=0)


def hyena_mixer(h, w_in, w_out, conv_w, conv_b, f_w1, f_b1, f_w2, f_b2, f_freq, f_w3, hy_bias):
    B, L, _ = h.shape
    u = dwconv_centred(h @ w_in, conv_w, conv_b).astype(F32)
    v, x1, x2 = jnp.split(u, 3, axis=-1)
    spec = hyena_filter_spectra(L, f_w1, f_b1, f_w2, f_b2, f_freq, f_w3)
    n = 2 * L
    z = v
    for o, gate in enumerate((x1, x2)):
        zc = jnp.fft.irfft(jnp.fft.rfft(z, n=n, axis=1) * spec[None, :, o], n=n, axis=1)[:, :L]
        z = gate * (zc + z * hy_bias[o].astype(F32))
    return z.astype(h.dtype) @ w_out


def swiglu(x, w1, w3, w2):
    return (jax.nn.silu(x @ w1) * (x @ w3)) @ w2


def moe_swiglu(x, w_router, b_router, w_gate, w_up, w_down):
    B, L, D = x.shape
    xt = x.reshape(-1, D)
    T = xt.shape[0]
    logits = (xt @ w_router).astype(F32) + b_router.astype(F32)
    top_v, top_i = lax.top_k(logits, TOP_K)
    gates = jax.nn.softmax(top_v, axis=-1)
    flat_e = top_i.reshape(-1)
    flat_g = gates.reshape(-1)
    order = jnp.argsort(flat_e)
    sorted_e = flat_e[order]
    token = order // TOP_K
    counts = jnp.bincount(flat_e, length=N_EXPERTS)
    starts = jnp.cumsum(counts) - counts
    padded = (counts + MOE_BLOCK - 1) // MOE_BLOCK * MOE_BLOCK
    p_ends = jnp.cumsum(padded)
    p_starts = p_ends - padded
    dest = p_starts[sorted_e] + jnp.arange(T * TOP_K) - starts[sorted_e]
    n_blocks = -(-(T * TOP_K) // MOE_BLOCK) + N_EXPERTS
    buf = jnp.zeros((n_blocks * MOE_BLOCK, D), x.dtype).at[dest].set(xt[token])
    block_e = jnp.minimum(jnp.searchsorted(p_ends, jnp.arange(n_blocks) * MOE_BLOCK, side='right'), N_EXPERTS - 1)

    def expert_block(args):
        xb, e = args
        hid = jax.nn.silu(xb @ w_gate[e]) * (xb @ w_up[e])
        return hid @ w_down[e]

    ybuf = lax.map(expert_block, (buf.reshape(n_blocks, MOE_BLOCK, D), block_e)).reshape(-1, D)
    y = jnp.zeros_like(xt).at[token].add(ybuf[dest] * flat_g[order][:, None].astype(x.dtype))
    return y.reshape(B, L, D)


def setup_inputs(seed: int = 0):
    key = jax.random.key(seed)
    keys = list(jax.random.split(key, 48))

    def nrm(shape, scale):
        return jax.random.normal(keys.pop(), shape, F32) * scale

    D = D_MODEL
    u = jax.random.uniform(keys.pop(), (N_EVEN, 2, D_RNN), F32, 0.9, 0.999)
    s = u ** (1.0 / RG_LRU_C)
    rnn_lam = jnp.log(s) - jnp.log1p(-s)
    return {
        'x_prompt': nrm((BATCH, SEQ, D), 1.0),
        'x_sample': nrm((DEC_BATCH, DEC_SEQ, D), 1.0),
        'cache_k': nrm((DEC_BATCH, N_EVEN, PAST_LEN, N_KV_HEADS, HEAD_DIM), 1.0),
        'cache_v': nrm((DEC_BATCH, N_EVEN, PAST_LEN, N_KV_HEADS, HEAD_DIM), 1.0),
        'state_rglru': nrm((DEC_BATCH, N_EVEN, 2, D_RNN), 0.5),
        'c': nrm((DEC_BATCH, D), 1.0),
        'c_ctx': nrm((D,), 1.0),
        'w_mod': nrm((DEPTH, D, 6 * D), 0.5 * D ** -0.5),
        'b_mod': nrm((DEPTH, 6 * D), 0.02),
        'norm_g': 1.0 + nrm((DEPTH, 2, D), 0.02),
        'final_g': 1.0 + nrm((D,), 0.02),
        'a_w_in': nrm((N_EVEN, D, D_IN_EVEN), D ** -0.5),
        'a_w_out': nrm((N_EVEN, D_ATTN + D_RNN, D), (D_ATTN + D_RNN) ** -0.5),
        'rnn_conv_w': nrm((N_EVEN, RNN_CONV, D_RNN), RNN_CONV ** -0.5),
        'rnn_conv_b': nrm((N_EVEN, D_RNN), 0.02),
        'rnn_w_a': nrm((N_EVEN, 2, RNN_BLOCKS, RNN_BLOCK_W, RNN_BLOCK_W), RNN_BLOCK_W ** -0.5),
        'rnn_b_a': nrm((N_EVEN, 2, D_RNN), 0.02),
        'rnn_w_x': nrm((N_EVEN, 2, RNN_BLOCKS, RNN_BLOCK_W, RNN_BLOCK_W), RNN_BLOCK_W ** -0.5),
        'rnn_b_x': nrm((N_EVEN, 2, D_RNN), 0.02),
        'rnn_lam': rnn_lam,
        'attn_sink': nrm((N_EVEN, N_HEADS), 0.5),
        'ffn_w1': nrm((N_EVEN, D, D_FF), D ** -0.5),
        'ffn_w3': nrm((N_EVEN, D, D_FF), D ** -0.5),
        'ffn_w2': nrm((N_EVEN, D_FF, D), D_FF ** -0.5),
        'h_w_in': nrm((N_ODD, D, 3 * D_HYENA), D ** -0.5),
        'h_w_out': nrm((N_ODD, D_HYENA, D), D_HYENA ** -0.5),
        'h_conv_w': nrm((N_ODD, HYENA_CONV, 3 * D_HYENA), HYENA_CONV ** -0.5),
        'h_conv_b': nrm((N_ODD, 3 * D_HYENA), 0.02),
        'hf_w1': nrm((N_ODD, HYENA_EMB, HYENA_FILTER_W), HYENA_EMB ** -0.5),
        'hf_b1': nrm((N_ODD, HYENA_FILTER_W), 0.5),
        'hf_w2': nrm((N_ODD, HYENA_FILTER_W, HYENA_FILTER_W), HYENA_FILTER_W ** -0.5),
        'hf_b2': nrm((N_ODD, HYENA_FILTER_W), 0.5),
        'hf_freq': 1.0 + nrm((N_ODD, 2, HYENA_FILTER_W), 0.1),
        'hf_w3': nrm((N_ODD, HYENA_FILTER_W, 2 * HYENA_ORDER * D_HYENA), 0.005),
        'h_bias': nrm((N_ODD, HYENA_ORDER, D_HYENA), 0.5),
        'moe_router': nrm((N_ODD, D, N_EXPERTS), D ** -0.5),
        'moe_router_b': nrm((N_ODD, N_EXPERTS), 0.01),
        'moe_w_gate': nrm((N_ODD, N_EXPERTS, D, D_FF_EXPERT), D ** -0.5),
        'moe_w_up': nrm((N_ODD, N_EXPERTS, D, D_FF_EXPERT), D ** -0.5),
        'moe_w_down': nrm((N_ODD, N_EXPERTS, D_FF_EXPERT, D), D_FF_EXPERT ** -0.5),
    }


def reference(x_prompt, x_sample, cache_k, cache_v, state_rglru, c, c_ctx, w_mod, b_mod, norm_g, final_g,
              a_w_in, a_w_out, rnn_conv_w, rnn_conv_b, rnn_w_a, rnn_b_a, rnn_w_x, rnn_b_x, rnn_lam, attn_sink,
              ffn_w1, ffn_w3, ffn_w2, h_w_in, h_w_out, h_conv_w, h_conv_b, hf_w1, hf_b1, hf_w2, hf_b2, hf_freq,
              hf_w3, h_bias, moe_router, moe_router_b, moe_w_gate, moe_w_up, moe_w_down):
    xc = x_prompt
    xl = x_sample
    new_k, new_v, new_s = [], [], []
    for layer in range(DEPTH):
        i = layer // 2
        mc = ada_params(c_ctx[None, :], w_mod[layer], b_mod[layer])
        ml = ada_params(c, w_mod[layer], b_mod[layer])
        hc = modulate(xc, norm_g[layer, 0], mc[0], mc[1])
        hl = modulate(xl, norm_g[layer, 0], ml[0], ml[1])
        if layer % 2 == 0:
            even_w = (a_w_in[i], a_w_out[i], rnn_conv_w[i], rnn_conv_b[i], rnn_w_a[i], rnn_b_a[i],
                      rnn_w_x[i], rnn_b_x[i], rnn_lam[i], attn_sink[i])
            oc, k_new, v_new, s_new = even_mixer_context(hc, *even_w)
            ol = even_mixer_latent(hl, cache_k[:, i], cache_v[:, i], state_rglru[:, i], *even_w)
            new_k.append(k_new)
            new_v.append(v_new)
            new_s.append(s_new)
        else:
            hy_w = (h_w_in[i], h_w_out[i], h_conv_w[i], h_conv_b[i], hf_w1[i], hf_b1[i], hf_w2[i], hf_b2[i],
                    hf_freq[i], hf_w3[i], h_bias[i])
            oc = hyena_mixer(hc, *hy_w)
            ol = hyena_mixer(hl, *hy_w)
        xc = xc + mc[2] * oc
        xl = xl + ml[2] * ol
        hc = modulate(xc, norm_g[layer, 1], mc[3], mc[4])
        hl = modulate(xl, norm_g[layer, 1], ml[3], ml[4])
        if layer % 2 == 0:
            fc = swiglu(hc, ffn_w1[i], ffn_w3[i], ffn_w2[i])
            fl = swiglu(hl, ffn_w1[i], ffn_w3[i], ffn_w2[i])
        else:
            moe_w = (moe_router[i], moe_router_b[i], moe_w_gate[i], moe_w_up[i], moe_w_down[i])
            fc = moe_swiglu(hc, *moe_w)
            fl = moe_swiglu(hl, *moe_w)
        xc = xc + mc[5] * fc
        xl = xl + ml[5] * fl
    y_prompt = rms_norm(xc, final_g)
    y_sample = rms_norm(xl, final_g)
    new_cache_k = jnp.stack(new_k, axis=1)
    new_cache_v = jnp.stack(new_v, axis=1)
    new_state_rglru = jnp.stack(new_s, axis=1)
    return (y_prompt, y_sample, new_cache_k, new_cache_v, new_state_rglru)
```

```python
import functools
import math

import jax
import jax.numpy as jnp
from jax import lax
from jax.experimental import pallas as pl
from jax.experimental.pallas import tpu as pltpu

F32 = jnp.float32
BF16 = jnp.bfloat16
I32 = jnp.int32

D = 2048
B_CTX, L_CTX = 32, 256
B_LAT, L_LAT = 4, 2048
T_CTX = B_CTX * L_CTX
T_LAT = B_LAT * L_LAT
T = T_CTX + T_LAT
GRID_W = 64
N_HEADS, N_KV, HD = 8, 2, 128
Q_PER_KV = N_HEADS // N_KV
D_ATTN = N_HEADS * HD
D_KV = N_KV * HD
WINDOW = 128
ROPE_BASE = 10000.0
D_RNN = D // 2
RNN_BLOCKS = 8
RNN_W = D_RNN // RNN_BLOCKS
RG_LRU_C = 8.0
HY_EMB = 33
HY_W = 64
HY_MIN_DECAY = math.log(1e-2) / 1.5
HY_MAX_DECAY = math.log(1e-2) / 0.3
D_FF = 5632
N_EXP = 8
D_FFE = 7168
EPS = 1e-6
NEG_INF = -1e30

LANES = 128
SUBLANES = 8
VMEM_LIMIT = 52 * 1024 * 1024
TM = 512
N_COND = 8
MOE_TM = 512
R_MOE = 2 * T + N_EXP * MOE_TM
HY_TD = 512
HY_FC = 1024


def _cparams(sem):
    return pltpu.CompilerParams(dimension_semantics=sem, vmem_limit_bytes=VMEM_LIMIT)


def _cond_of_block(i, tm):
    nb_ctx = T_CTX // tm
    return jnp.where(i < nb_ctx, 0, 1 + (i - nb_ctx) // (L_LAT // tm))


def _mm_body(*refs, n_a, dual, residual, grouped):
    it = iter(refs)
    if grouped:
        chg_ref = next(it)
        next(it)
        nact_ref = next(it)
    a_refs = [next(it) for _ in range(n_a)]
    w_refs = [[next(it) for _ in range(n_a)] for _ in range(2 if dual else 1)]
    if residual:
        x_ref = next(it)
        g_ref = next(it)
    o_ref = next(it)
    wb_refs = [[next(it) for _ in range(n_a)] for _ in range(2 if dual else 1)]
    i = pl.program_id(1)
    refresh = (chg_ref[i] != 0) if grouped else (i == 0)

    @pl.when(refresh)
    def _():
        for ws, wbs in zip(w_refs, wb_refs):
            for w, wb in zip(ws, wbs):
                wb[...] = w[...].astype(BF16)

    def prod(wbs):
        acc = None
        for a, wb in zip(a_refs, wbs):
            p = jnp.dot(a[...], wb[...], preferred_element_type=F32)
            acc = p if acc is None else acc + p
        return acc

    def compute():
        y = prod(wb_refs[0])
        if dual:
            y = y * jax.nn.sigmoid(y) * prod(wb_refs[1])
        if residual:
            y = x_ref[...] + g_ref[...] * y
        o_ref[...] = y.astype(o_ref.dtype)

    if grouped:
        pl.when(i < nact_ref[0])(compute)

        @pl.when(i >= nact_ref[0])
        def _():
            o_ref[...] = jnp.zeros_like(o_ref)
    else:
        compute()


def _mm(a_list, w_list, w_row_blocks, *, n_cols, col_off, tn, tm, out_dtype, name,
        dual_w=None, residual=None, group=None):
    M = a_list[0].shape[0]
    n_a = len(a_list)
    dual = dual_w is not None
    grouped = group is not None
    assert n_cols % tn == 0 and col_off % tn == 0 and M % tm == 0
    nj, ni = n_cols // tn, M // tm
    cb = col_off // tn

    def rowblk(i, pref):
        if grouped:
            return jnp.minimum(i, pref[2][0] - 1)
        return i

    a_specs = [pl.BlockSpec((tm, a.shape[1]), lambda j, i, *p: (rowblk(i, p), 0)) for a in a_list]

    def w_spec(w, rb, ka):
        if grouped:
            return pl.BlockSpec((None, ka, tn), lambda j, i, *p: (p[1][rowblk(i, p)], rb, cb + j))
        return pl.BlockSpec((ka, tn), lambda j, i, *p: (rb, cb + j))

    w_specs = [w_spec(w, rb, a.shape[1]) for w, rb, a in zip(w_list, w_row_blocks, a_list)]
    ins = list(a_list) + list(w_list)
    in_specs = a_specs + w_specs
    if dual:
        ins += list(dual_w)
        in_specs += [w_spec(w, rb, a.shape[1]) for w, rb, a in zip(dual_w, w_row_blocks, a_list)]
    if residual is not None:
        x, mods, slot = residual
        ins += [x, mods]
        in_specs += [pl.BlockSpec((tm, tn), lambda j, i, *p: (i, j)),
                     pl.BlockSpec((None, None, 1, tn), lambda j, i, *p: (slot, _cond_of_block(i, tm), 0, j))]
    scratch = [pltpu.VMEM((a.shape[1], tn), BF16) for a in a_list] * (2 if dual else 1)
    body = functools.partial(_mm_body, n_a=n_a, dual=dual, residual=residual is not None, grouped=grouped)
    out_spec = pl.BlockSpec((tm, tn), lambda j, i, *p: (i, j))
    gs = pltpu.PrefetchScalarGridSpec(num_scalar_prefetch=3 if grouped else 0, grid=(nj, ni),
                                      in_specs=in_specs, out_specs=out_spec, scratch_shapes=scratch)
    call = pl.pallas_call(body, grid_spec=gs, out_shape=jax.ShapeDtypeStruct((M, n_cols), out_dtype),
                          compiler_params=_cparams(("arbitrary", "arbitrary")), name=name)
    if grouped:
        return call(*group, *ins)
    return call(*ins)


def _ada_body(c_ref, w_ref, b_ref, o_ref):
    c = c_ref[...]
    s = (c * jax.nn.sigmoid(c)).astype(BF16)
    o_ref[...] = jnp.dot(s, w_ref[...].astype(BF16), preferred_element_type=F32) + b_ref[...]


def _ada_params(cond, w_mod, b_mod):
    depth = w_mod.shape[0]
    tn = 1024
    return pl.pallas_call(
        _ada_body, grid=(depth, 6 * D // tn),
        in_specs=[pl.BlockSpec((N_COND, D), lambda l, j: (0, 0)),
                  pl.BlockSpec((None, D, tn), lambda l, j: (l, 0, j)),
                  pl.BlockSpec((None, 1, tn), lambda l, j: (l, 0, j))],
        out_specs=pl.BlockSpec((None, N_COND, tn), lambda l, j: (l, 0, j)),
        out_shape=jax.ShapeDtypeStruct((depth, N_COND, 6 * D), F32),
        compiler_params=_cparams(("arbitrary", "arbitrary")), name="ada_params",
    )(cond, w_mod, b_mod.reshape(depth, 1, 6 * D))


def _norm_body(*refs, modulate, router):
    it = iter(refs)
    x_ref, g_ref = next(it), next(it)
    if modulate:
        sh_ref, sc_ref = next(it), next(it)
    if router:
        wr_ref, br_ref = next(it), next(it)
    o_ref = next(it)
    x = x_ref[...]
    y = x * lax.rsqrt(jnp.mean(x * x, axis=-1, keepdims=True) + EPS) * g_ref[...]
    if modulate:
        y = y * (1.0 + sc_ref[...]) + sh_ref[...]
    o_ref[...] = y.astype(o_ref.dtype)
    if router:
        idx_ref, g0_ref, g1_ref = next(it), next(it), next(it)
        logits = jnp.dot(y, wr_ref[...], preferred_element_type=F32, precision=lax.Precision.HIGHEST) + br_ref[...]
        lane = lax.broadcasted_iota(I32, logits.shape, 1)
        logits = jnp.where(lane < N_EXP, logits, -jnp.inf)
        lanef = lane.astype(F32)
        m1 = jnp.max(logits, axis=-1, keepdims=True)
        i1 = jnp.min(jnp.where(logits == m1, lanef, float(LANES)), axis=-1, keepdims=True)
        rest = jnp.where(lanef == i1, -jnp.inf, logits)
        m2 = jnp.max(rest, axis=-1, keepdims=True)
        i2 = jnp.min(jnp.where(rest == m2, lanef, float(LANES)), axis=-1, keepdims=True)
        e21 = jnp.exp(m2 - m1)
        gate1 = 1.0 / (1.0 + e21)
        idx_ref[...] = jnp.where(lane == 0, i1, jnp.where(lane == 1, i2, 0.0)).astype(I32)
        g0_ref[...] = jnp.broadcast_to(gate1, logits.shape)
        g1_ref[...] = jnp.broadcast_to(e21 * gate1, logits.shape)


def _norm(x, g, *, mods=None, slots=None, router=None, out_dtype, name, row0=0, rows=None):
    rows = x.shape[0] if rows is None else rows
    rb0 = row0 // TM
    modulate = mods is not None
    ins = [x, g.reshape(1, D)]
    in_specs = [pl.BlockSpec((TM, D), lambda i: (rb0 + i, 0)), pl.BlockSpec((1, D), lambda i: (0, 0))]
    if modulate:
        for slot in slots:
            ins.append(mods)
            in_specs.append(pl.BlockSpec((None, None, 1, D), lambda i, slot=slot: (slot, _cond_of_block(i, TM), 0, 0)))
    out_shape = [jax.ShapeDtypeStruct((rows, D), out_dtype)]
    out_specs = [pl.BlockSpec((TM, D), lambda i: (i, 0))]
    if router is not None:
        w_r, b_r = router
        ins += [jnp.pad(w_r, ((0, 0), (0, LANES - N_EXP))), jnp.pad(b_r, (0, LANES - N_EXP)).reshape(1, LANES)]
        in_specs += [pl.BlockSpec((D, LANES), lambda i: (0, 0)), pl.BlockSpec((1, LANES), lambda i: (0, 0))]
        out_shape += [jax.ShapeDtypeStruct((rows, LANES), I32), jax.ShapeDtypeStruct((rows, LANES), F32),
                      jax.ShapeDtypeStruct((rows, LANES), F32)]
        out_specs += [pl.BlockSpec((TM, LANES), lambda i: (i, 0))] * 3
    body = functools.partial(_norm_body, modulate=modulate, router=router is not None)
    res = pl.pallas_call(body, grid=(rows // TM,), in_specs=in_specs, out_specs=out_specs, out_shape=out_shape,
                         compiler_params=_cparams(("arbitrary",)), name=name)(*ins)
    return res if router is not None else res[0]


def _swap32(x):
    up = jnp.concatenate([x[:, 32:], x[:, :32]], axis=1)
    down = jnp.concatenate([x[:, 96:], x[:, :96]], axis=1)
    lane = lax.broadcasted_iota(I32, x.shape, 1)
    return jnp.where((lane % 64) < 32, up, down)


def _qkprep_body(q_ref, kv_ref, cos_ref, sin_ref, qo_ref, ko_ref, vo_ref):
    cos, sin = cos_ref[...], sin_ref[...]
    for h in range(N_HEADS):
        x = q_ref[:, h * HD:(h + 1) * HD]
        qo_ref[:, h * HD:(h + 1) * HD] = (x * cos + _swap32(x) * sin).astype(BF16)
    for h in range(N_KV):
        x = kv_ref[:, h * HD:(h + 1) * HD]
        ko_ref[:, h * HD:(h + 1) * HD] = (x * cos + _swap32(x) * sin).astype(BF16)
    vo_ref[...] = kv_ref[:, D_KV:].astype(BF16)


def _qk_prep(q, kv, cos_t, sin_t):
    return pl.pallas_call(
        _qkprep_body, grid=(T // TM,),
        in_specs=[pl.BlockSpec((TM, D_ATTN), lambda i: (i, 0)), pl.BlockSpec((TM, 2 * D_KV), lambda i: (i, 0)),
                  pl.BlockSpec((TM, HD), lambda i: (i, 0)), pl.BlockSpec((TM, HD), lambda i: (i, 0))],
        out_specs=[pl.BlockSpec((TM, D_ATTN), lambda i: (i, 0)), pl.BlockSpec((TM, D_KV), lambda i: (i, 0)),
                   pl.BlockSpec((TM, D_KV), lambda i: (i, 0))],
        out_shape=[jax.ShapeDtypeStruct((T, D_ATTN), BF16), jax.ShapeDtypeStruct((T, D_KV), BF16),
                   jax.ShapeDtypeStruct((T, D_KV), BF16)],
        compiler_params=_cparams(("arbitrary",)), name="qk_prep")(q, kv, cos_t, sin_t)


def _attn_core(q_ref, o_ref, sink_ref, kvh, kall, vall, mask):
    scale = HD ** -0.5
    for g in range(Q_PER_KV):
        qh = q_ref[:, g * HD:(g + 1) * HD]
        s = lax.dot_general(qh, kall, (((1,), (1,)), ((), ())), preferred_element_type=F32) * scale
        if mask is not None:
            s = jnp.where(mask, s, NEG_INF)
        sk = sink_ref[kvh * Q_PER_KV + g]
        m = jnp.maximum(jnp.max(s, axis=-1, keepdims=True), sk)
        p = jnp.exp(s - m)
        denom = jnp.sum(p, axis=-1, keepdims=True) + jnp.exp(sk - m)
        o = jnp.dot(p.astype(BF16), vall, preferred_element_type=F32) / denom
        o_ref[:, g * HD:(g + 1) * HD] = o.astype(o_ref.dtype)


def _attn_ctx_body(sink_ref, q_ref, k_ref, v_ref, o_ref):
    _attn_core(q_ref, o_ref, sink_ref, pl.program_id(1), k_ref[...], v_ref[...], None)


def _attn_lat_body(sink_ref, q_ref, kp_ref, kc_ref, kn_ref, vp_ref, vc_ref, vn_ref, ck_ref, cv_ref, o_ref):
    qb = pl.program_id(1)
    kall = jnp.concatenate([kp_ref[...], kc_ref[...], kn_ref[...], ck_ref[...].astype(BF16)], axis=0)
    vall = jnp.concatenate([vp_ref[...], vc_ref[...], vn_ref[...], cv_ref[...].astype(BF16)], axis=0)
    nk = kall.shape[0]
    qpos = qb * WINDOW + lax.broadcasted_iota(I32, (WINDOW, nk), 0)
    col = lax.broadcasted_iota(I32, (WINDOW, nk), 1)
    kpos = (qb - 1) * WINDOW + col
    in_win = (jnp.abs(kpos - qpos) <= WINDOW) & (kpos >= 0) & (kpos < L_LAT)
    mask = in_win | (col >= 3 * WINDOW)
    _attn_core(q_ref, o_ref, sink_ref, pl.program_id(2), kall, vall, mask)


def _attention(qb, kb, vb, cache_k, cache_v, sink):
    qw = Q_PER_KV * HD
    smem = pl.BlockSpec(memory_space=pltpu.SMEM)
    o_ctx = pl.pallas_call(
        _attn_ctx_body, grid=(B_CTX, N_KV),
        in_specs=[smem, pl.BlockSpec((L_CTX, qw), lambda b, h: (b, h)),
                  pl.BlockSpec((L_CTX, HD), lambda b, h: (b, h)), pl.BlockSpec((L_CTX, HD), lambda b, h: (b, h))],
        out_specs=pl.BlockSpec((L_CTX, qw), lambda b, h: (b, h)),
        out_shape=jax.ShapeDtypeStruct((T_CTX, D_ATTN), BF16),
        compiler_params=_cparams(("arbitrary", "arbitrary")), name="attn_ctx")(sink, qb, kb, vb)
    nb = L_LAT // WINDOW
    base = T_CTX // WINDOW

    def cur(b, i, h):
        return (base + b * nb + i, h)

    def prv(b, i, h):
        return (base + b * nb + jnp.maximum(i - 1, 0), h)

    def nxt(b, i, h):
        return (base + b * nb + jnp.minimum(i + 1, nb - 1), h)

    blk = lambda f: pl.BlockSpec((WINDOW, HD), f)
    cspec = pl.BlockSpec((None, cache_k.shape[1], HD), lambda b, i, h: (b, 0, h))
    o_lat = pl.pallas_call(
        _attn_lat_body, grid=(B_LAT, nb, N_KV),
        in_specs=[smem, pl.BlockSpec((WINDOW, qw), cur), blk(prv), blk(cur), blk(nxt), blk(prv), blk(cur), blk(nxt),
                  cspec, cspec],
        out_specs=pl.BlockSpec((WINDOW, qw), lambda b, i, h: (b * nb + i, h)),
        out_shape=jax.ShapeDtypeStruct((T_LAT, D_ATTN), BF16),
        compiler_params=_cparams(("arbitrary", "arbitrary", "arbitrary")), name="attn_lat",
    )(sink, qb, kb, kb, kb, vb, vb, vb, cache_k, cache_v)
    return o_ctx, o_lat


def _rglru_body(x_ref, y_ref, h0_ref, cw_ref, cb_ref, wa_ref, ba_ref, wx_ref, bx_ref, lam_ref,
                o_ref, fin_ref, a_s, b_s, h_s):
    L = x_ref.shape[0]
    x = x_ref[...]
    row = lax.broadcasted_iota(I32, x.shape, 0)
    xc = cb_ref[...] + x * cw_ref[2:3, :]
    xc = xc + jnp.where(row >= 2, pltpu.roll(x, 2, 0), 0.0) * cw_ref[0:1, :]
    xc = xc + jnp.where(row >= 1, pltpu.roll(x, 1, 0), 0.0) * cw_ref[1:2, :]
    xc = xc + jnp.where(row < L - 1, pltpu.roll(x, L - 1, 0), 0.0) * cw_ref[3:4, :]
    xcb = xc.astype(BF16)
    for d in range(2):
        ga = jnp.dot(xcb, wa_ref[d].astype(BF16), preferred_element_type=F32) + ba_ref[d:d + 1, :]
        gx = jnp.dot(xcb, wx_ref[d].astype(BF16), preferred_element_type=F32) + bx_ref[d:d + 1, :]
        log_a = -RG_LRU_C * jax.nn.sigmoid(ga) * jax.nn.softplus(-lam_ref[d:d + 1, :])
        a = jnp.exp(log_a)
        a_s[d] = a
        b_s[d] = jnp.sqrt(-jnp.tanh(log_a) * (1.0 + a * a)) * jax.nn.sigmoid(gx) * xc

    nt = L // SUBLANES
    r8 = lax.broadcasted_iota(I32, (SUBLANES, RNN_W), 0)

    def tile_scan(d, blk, carry, reverse):
        r0 = pl.multiple_of(blk * SUBLANES, SUBLANES)
        a = a_s[d, pl.ds(r0, SUBLANES), :]
        b = b_s[d, pl.ds(r0, SUBLANES), :]
        for k in (1, 2, 4):
            sh = SUBLANES - k if reverse else k
            m = (r8 < SUBLANES - k) if reverse else (r8 >= k)
            b = jnp.where(m, a * pltpu.roll(b, sh, 0) + b, b)
            a = jnp.where(m, a * pltpu.roll(a, sh, 0), a)
        h = a * carry + b
        h_s[d, pl.ds(r0, SUBLANES), :] = h
        last = h[0:1, :] if reverse else h[SUBLANES - 1:SUBLANES, :]
        return jnp.broadcast_to(last, (SUBLANES, RNN_W))

    def step(i, carry):
        return tile_scan(0, i, carry[0], False), tile_scan(1, nt - 1 - i, carry[1], True)

    init = (jnp.broadcast_to(h0_ref[0:1, :], (SUBLANES, RNN_W)), jnp.broadcast_to(h0_ref[1:2, :], (SUBLANES, RNN_W)))
    lax.fori_loop(0, nt, step, init)
    hsum = h_s[0] + h_s[1]
    o_ref[...] = (hsum * jax.nn.gelu(y_ref[...])).astype(o_ref.dtype)
    fin_ref[0:1, :] = h_s[0, L - 1:L, :]
    fin_ref[1:2, :] = h_s[1, 0:1, :]


def _rglru(xy, h0, n_seq, L, row0, conv_w, conv_b, w_a, b_a, w_x, b_x, lam, name):
    rb0 = row0 // L
    nb = RNN_BLOCKS
    vec = lambda r: pl.BlockSpec((r, RNN_W), lambda b, n: (0, n))
    wsp = pl.BlockSpec((2, None, RNN_W, RNN_W), lambda b, n: (0, n, 0, 0))
    return pl.pallas_call(
        _rglru_body, grid=(n_seq, nb),
        in_specs=[pl.BlockSpec((L, RNN_W), lambda b, n: (rb0 + b, n)),
                  pl.BlockSpec((L, RNN_W), lambda b, n: (rb0 + b, nb + n)),
                  pl.BlockSpec((None, 2, RNN_W), lambda b, n: (b, 0, n)),
                  vec(4), vec(1), wsp, vec(2), wsp, vec(2), vec(2)],
        out_specs=[pl.BlockSpec((L, RNN_W), lambda b, n: (b, n)),
                   pl.BlockSpec((None, 2, RNN_W), lambda b, n: (b, 0, n))],
        out_shape=[jax.ShapeDtypeStruct((n_seq * L, D_RNN), BF16), jax.ShapeDtypeStruct((n_seq, 2, D_RNN), F32)],
        scratch_shapes=[pltpu.VMEM((2, L, RNN_W), F32)] * 3,
        compiler_params=_cparams(("arbitrary", "arbitrary")), name=name,
    )(xy, xy, h0, conv_w, conv_b.reshape(1, D_RNN), w_a, b_a, w_x, b_x, lam)


def _hy_filter_body(fv_ref, w1_ref, b1_ref, w2_ref, b2_ref, fr_ref, w3_ref, dl_ref, o_ref, *, L):
    hp = lax.Precision.HIGHEST
    rowi = lax.broadcasted_iota(I32, (L, LANES), 0)
    lane = lax.broadcasted_iota(I32, (L, LANES), 1)

    def features(pos):
        posf = pos.astype(F32)
        tt = posf / (L - 1)
        ang = fv_ref[...] * (2.0 * math.pi * posf / L)
        z = jnp.where(lane == 0, tt, jnp.where(lane <= 16, jnp.cos(ang), jnp.where(lane <= 32, -jnp.sin(ang), 0.0)))
        return z, tt[:, 0:1]

    def mlp(z):
        h = jnp.sin(fr_ref[0:1, :] * (jnp.dot(z, w1_ref[...], preferred_element_type=F32, precision=hp) + b1_ref[...]))
        h = jnp.sin(fr_ref[1:2, :] * (jnp.dot(h, w2_ref[...], preferred_element_type=F32, precision=hp) + b2_ref[...]))
        return jnp.dot(h, w3_ref[...], preferred_element_type=F32, precision=hp)

    is_bwd = (pl.program_id(0) // (D // o_ref.shape[1])) % 2 == 1
    pos = jnp.where(is_bwd, L - rowi, rowi)
    z, tt = features(pos)
    filt = mlp(z) * jnp.exp(-tt * dl_ref[...])
    dead = is_bwd & (lax.broadcasted_iota(I32, filt.shape, 0) == 0)
    o_ref[...] = jnp.where(dead, 0.0, filt).astype(o_ref.dtype)


def _hy_filter(L, f_w1, f_b1, f_w2, f_b2, f_freq, f_w3):
    bands = (HY_EMB - 1) // 2
    f = jnp.linspace(1e-4, bands - 1, bands, dtype=F32)
    fv = jnp.zeros((LANES,), F32).at[1:1 + bands].set(f).at[1 + bands:1 + 2 * bands].set(f).reshape(1, LANES)
    padw = lambda w, r, c: jnp.pad(w.astype(F32), ((0, r - w.shape[0]), (0, c - w.shape[1])))
    padv = lambda v: jnp.pad(v.astype(F32), (0, LANES - v.shape[0])).reshape(1, LANES)
    w1, w2 = padw(f_w1, LANES, LANES), padw(f_w2, LANES, LANES)
    w3 = padw(f_w3, LANES, f_w3.shape[1])
    fr = jnp.pad(f_freq.astype(F32), ((0, 0), (0, LANES - HY_W)))
    deltas = jnp.abs(jnp.linspace(HY_MIN_DECAY, HY_MAX_DECAY, D, dtype=F32)).reshape(1, D)
    tn = 1024
    per = D // tn
    full = lambda r: pl.BlockSpec((r, LANES), lambda j: (0, 0))
    return pl.pallas_call(
        functools.partial(_hy_filter_body, L=L), grid=(4 * per,),
        in_specs=[full(1), full(LANES), full(1), full(LANES), full(1), full(2),
                  pl.BlockSpec((LANES, tn), lambda j: (0, j)), pl.BlockSpec((1, tn), lambda j: (0, j % per))],
        out_specs=pl.BlockSpec((L, tn), lambda j: (0, j)),
        out_shape=jax.ShapeDtypeStruct((L, 4 * D), BF16),
        compiler_params=_cparams(("arbitrary",)), name=f"hy_filter_{L}",
    )(fv, w1, padv(f_b1), w2, padv(f_b2), fr, w3, deltas)


def _dft_mats(L, fc):
    n = 2 * L
    h = fc // 2
    nf = L // h
    k = jnp.arange(L, dtype=I32)[:, None]
    t = jnp.arange(L, dtype=I32)[None, :]
    ang = ((k * t) % n).astype(F32) * (2.0 * math.pi / n)
    cos, sin = jnp.cos(ang), jnp.sin(ang)
    nyq = jnp.where(t % 2 == 0, 1.0, -1.0).astype(F32)
    is0 = k == 0
    re_f = cos
    im_f = jnp.where(is0, nyq, -sin)
    re_i = jnp.where(is0, 1.0, 2.0) * cos / n
    im_i = jnp.where(is0, nyq, -2.0 * sin) / n
    cf = jnp.concatenate([re_f.reshape(nf, h, L), im_f.reshape(nf, h, L)], axis=1)
    ci = jnp.concatenate([re_i.reshape(nf, h, L), im_i.reshape(nf, h, L)], axis=1)
    return cf.astype(BF16), jnp.swapaxes(ci, 1, 2).astype(BF16)


def _spec_body(cf_ref, c1_ref, c2_ref, o_ref):
    f = pl.program_id(1)
    fc = cf_ref.shape[0]
    h = fc // 2
    z1 = jnp.dot(cf_ref[...], c1_ref[...], preferred_element_type=F32)
    z2 = jnp.dot(cf_ref[...], c2_ref[...], preferred_element_type=F32)
    r = lax.broadcasted_iota(I32, z1.shape, 0)
    kk = f * h + jnp.where(r < h, r, r - h)
    nyq = (r == h) & (f == 0)
    odd = (kk % 2 == 1) & jnp.logical_not(nyq)
    o_ref[...] = z1 + jnp.where(odd, -z2, z2)


def _hy_spectra(filt, cf):
    nf, fc, L = cf.shape
    td = HY_TD
    per = D // td
    return pl.pallas_call(
        _spec_body, grid=(2 * per, nf),
        in_specs=[pl.BlockSpec((None, fc, L), lambda c, f: (f, 0, 0)),
                  pl.BlockSpec((L, td), lambda c, f: (0, (c // per) * 2 * per + c % per)),
                  pl.BlockSpec((L, td), lambda c, f: (0, (c // per) * 2 * per + per + c % per))],
        out_specs=pl.BlockSpec((None, fc, td), lambda c, f: (f, 0, c)),
        out_shape=jax.ShapeDtypeStruct((nf, fc, 2 * D), F32),
        compiler_params=_cparams(("arbitrary", "arbitrary")), name=f"hy_spectra_{L}")(cf, filt, filt)


def _hy_prep_body(*refs, L):
    u_refs, cw_refs, cb_refs = refs[0:3], refs[3:6], refs[6:9]
    zf_ref, zb_ref, x1_ref, x2_ref = refs[9:]
    rows = u_refs[0].shape[0]
    pos = lax.broadcasted_iota(I32, u_refs[0].shape, 0) % L

    def conv(u_ref, cw_ref, cb_ref):
        u = u_ref[...]
        uc = cb_ref[...] + u * cw_ref[1:2, :]
        uc = uc + jnp.where(pos >= 1, pltpu.roll(u, 1, 0), 0.0) * cw_ref[0:1, :]
        return uc + jnp.where(pos < L - 1, pltpu.roll(u, rows - 1, 0), 0.0) * cw_ref[2:3, :]

    v = conv(u_refs[0], cw_refs[0], cb_refs[0])
    zf_ref[...] = v
    zb_ref[...] = v.astype(BF16)
    x1_ref[...] = conv(u_refs[1], cw_refs[1], cb_refs[1])
    x2_ref[...] = conv(u_refs[2], cw_refs[2], cb_refs[2])


def _hy_prep(u, conv_w, conv_b):
    tc = 256
    per = D // tc
    cb = conv_b.reshape(1, 3 * D)
    outs = []
    for row0, rows, L in ((0, T_CTX, L_CTX), (T_CTX, T_LAT, L_LAT)):
        tr = max(L, TM)
        rb0 = row0 // tr
        third = lambda r, k: pl.BlockSpec((r, tc), lambda i, c, k=k: (rb0 + i if r == tr else 0, k * per + c))
        ospec = pl.BlockSpec((tr, tc), lambda i, c: (i, c))
        outs.append(pl.pallas_call(
            functools.partial(_hy_prep_body, L=L), grid=(rows // tr, per),
            in_specs=[third(tr, k) for k in range(3)] + [third(3, k) for k in range(3)] + [third(1, k) for k in range(3)],
            out_specs=[ospec] * 4,
            out_shape=[jax.ShapeDtypeStruct((rows, D), F32), jax.ShapeDtypeStruct((rows, D), BF16),
                       jax.ShapeDtypeStruct((rows, D), F32), jax.ShapeDtypeStruct((rows, D), F32)],
            compiler_params=_cparams(("arbitrary", "arbitrary")), name=f"hy_prep_{L}",
        )(u, u, u, conv_w, conv_w, conv_w, cb, cb, cb))
    return outs


def _longconv_body(z_ref, cf_ref, ci_ref, s_ref, o_ref, acc_ref):
    f = pl.program_id(2)
    fc = cf_ref.shape[0]
    h = fc // 2
    zf = jnp.dot(cf_ref[...], z_ref[...], preferred_element_type=F32)
    zre, zim = zf[:h], zf[h:]
    sre, sim = s_ref[:h, :], s_ref[h:, :]
    first = (lax.broadcasted_iota(I32, zre.shape, 0) == 0) & (f == 0)
    yre = zre * sre - jnp.where(first, 0.0, zim * sim)
    yim = jnp.where(first, zim * sim, zre * sim + zim * sre)
    y = jnp.concatenate([yre, yim], axis=0).astype(BF16)
    contrib = jnp.dot(ci_ref[...], y, preferred_element_type=F32)

    @pl.when(f == 0)
    def _():
        acc_ref[...] = contrib

    @pl.when(f > 0)
    def _():
        acc_ref[...] += contrib

    @pl.when(f == pl.num_programs(2) - 1)
    def _():
        o_ref[...] = acc_ref[...]


def _longconv(z, n_seq, L, cf, ci, spec, order):
    nf, fc, _ = cf.shape
    td = HY_TD if L > TM else D
    per = D // td
    return pl.pallas_call(
        _longconv_body, grid=(n_seq, per, nf),
        in_specs=[pl.BlockSpec((L, td), lambda b, c, f: (b, c)),
                  pl.BlockSpec((None, fc, L), lambda b, c, f: (f, 0, 0)),
                  pl.BlockSpec((None, L, fc), lambda b, c, f: (f, 0, 0)),
                  pl.BlockSpec((None, fc, td), lambda b, c, f: (f, 0, order * per + c))],
        out_specs=pl.BlockSpec((L, td), lambda b, c, f: (b, c)),
        out_shape=jax.ShapeDtypeStruct((n_seq * L, D), F32),
        scratch_shapes=[pltpu.VMEM((L, td), F32)],
        compiler_params=_cparams(("arbitrary", "arbitrary", "arbitrary")), name=f"longconv_{L}_{order}",
    )(z, cf, ci, spec)


def _hy_gate_body(zc_ref, z_ref, x_ref, b_ref, of_ref, ob_ref):
    z = x_ref[...] * (zc_ref[...] + z_ref[...] * b_ref[...])
    of_ref[...] = z
    ob_ref[...] = z.astype(BF16)


def _hy_gate(zc, z, x, bias):
    rows = zc.shape[0]
    blk = pl.BlockSpec((TM, D), lambda i: (i, 0))
    return pl.pallas_call(
        _hy_gate_body, grid=(rows // TM,),
        in_specs=[blk, blk, blk, pl.BlockSpec((1, D), lambda i: (0, 0))],
        out_specs=[blk, blk],
        out_shape=[jax.ShapeDtypeStruct((rows, D), F32), jax.ShapeDtypeStruct((rows, D), BF16)],
        compiler_params=_cparams(("arbitrary",)), name="hy_gate")(zc, z, x, bias.reshape(1, D))


GATHER_ROWS = 256
COMBINE_ROWS = 128


def _gather_body(src_ref, h_ref, o_ref, sem):
    base = pl.program_id(0) * GATHER_ROWS

    def copy(r):
        return pltpu.make_async_copy(h_ref.at[src_ref[base + r]], o_ref.at[r], sem)

    def start(r, c):
        copy(r).start()
        return c

    def wait(r, c):
        copy(r).wait()
        return c

    lax.fori_loop(0, GATHER_ROWS, start, 0)
    lax.fori_loop(0, GATHER_ROWS, wait, 0)


def _moe_gather(h, src_tok):
    sub = D // LANES
    gs = pltpu.PrefetchScalarGridSpec(
        num_scalar_prefetch=1, grid=(R_MOE // GATHER_ROWS,),
        in_specs=[pl.BlockSpec(memory_space=pl.ANY)],
        out_specs=pl.BlockSpec((GATHER_ROWS, sub, LANES), lambda i, s: (i, 0, 0)),
        scratch_shapes=[pltpu.SemaphoreType.DMA(())])
    out = pl.pallas_call(_gather_body, grid_spec=gs, out_shape=jax.ShapeDtypeStruct((R_MOE, sub, LANES), BF16),
                         compiler_params=_cparams(("arbitrary",)), name="moe_gather")(src_tok, h.reshape(T, sub, LANES))
    return out.reshape(R_MOE, D)


def _combine_body(d0_ref, d1_ref, y_ref, x_ref, g0_ref, g1_ref, gate_ref, o_ref, buf, sem):
    base = pl.program_id(0) * COMBINE_ROWS

    def copy(k, r):
        d_ref = d0_ref if k == 0 else d1_ref
        return pltpu.make_async_copy(y_ref.at[d_ref[base + r]], buf.at[k, r], sem)

    def start(r, c):
        copy(0, r).start()
        copy(1, r).start()
        return c

    def wait(r, c):
        copy(0, r).wait()
        copy(1, r).wait()
        return c

    lax.fori_loop(0, COMBINE_ROWS, start, 0)
    lax.fori_loop(0, COMBINE_ROWS, wait, 0)
    y = g0_ref[...] * buf[0] + g1_ref[...] * buf[1]
    o_ref[...] = x_ref[...] + gate_ref[...] * y


def _moe_combine(ybuf, dest0, dest1, g0, g1, x, mods, slot):
    sub = D // LANES
    tok = pl.BlockSpec((COMBINE_ROWS, sub, LANES), lambda i, *p: (i, 0, 0))
    gsp = pl.BlockSpec((COMBINE_ROWS, 1, LANES), lambda i, *p: (i, 0, 0))
    gs = pltpu.PrefetchScalarGridSpec(
        num_scalar_prefetch=2, grid=(T // COMBINE_ROWS,),
        in_specs=[pl.BlockSpec(memory_space=pl.ANY), tok, gsp, gsp,
                  pl.BlockSpec((None, None, sub, LANES), lambda i, *p: (slot, _cond_of_block(i, COMBINE_ROWS), 0, 0))],
        out_specs=tok,
        scratch_shapes=[pltpu.VMEM((2, COMBINE_ROWS, sub, LANES), F32), pltpu.SemaphoreType.DMA(())])
    out = pl.pallas_call(_combine_body, grid_spec=gs, out_shape=jax.ShapeDtypeStruct((T, sub, LANES), F32),
                         compiler_params=_cparams(("arbitrary",)), name="moe_combine",
                         )(dest0, dest1, ybuf.reshape(R_MOE, sub, LANES), x.reshape(T, sub, LANES),
                           g0.reshape(T, 1, LANES), g1.reshape(T, 1, LANES), mods.reshape(6, N_COND, sub, LANES))
    return out.reshape(T, D)


def _moe_plan(idx):
    e = idx[:, :2].reshape(-1)
    onehot = (e[:, None] == jnp.arange(N_EXP, dtype=I32)[None, :]).astype(I32)
    csum = jnp.cumsum(onehot, axis=0)
    rank = jnp.sum((csum - onehot) * onehot, axis=1)
    counts = csum[-1]
    padded = (counts + MOE_TM - 1) // MOE_TM * MOE_TM
    p_ends = jnp.cumsum(padded)
    p_starts = p_ends - padded
    dest = p_starts[e] + rank
    src_tok = jnp.zeros((R_MOE,), I32).at[dest].set(jnp.arange(2 * T, dtype=I32) // 2)
    nblk = R_MOE // MOE_TM
    blk_start = jnp.arange(nblk, dtype=I32) * MOE_TM
    gid = jnp.minimum(jnp.sum((blk_start[:, None] >= p_ends[None, :]).astype(I32), axis=1), N_EXP - 1)
    nact = (p_ends[-1] // MOE_TM).astype(I32).reshape(1)
    gid = jnp.where(jnp.arange(nblk) < nact[0], gid, gid[jnp.maximum(nact[0] - 1, 0)])
    chg = jnp.concatenate([jnp.ones((1,), I32), (gid[1:] != gid[:-1]).astype(I32)])
    d2 = dest.reshape(T, 2)
    return d2[:, 0], d2[:, 1], src_tok, (chg, gid, nact)


def _rope_tables():
    quarter = HD // 4
    inv_freq = ROPE_BASE ** (-jnp.arange(quarter, dtype=F32) / quarter)
    t = jnp.arange(L_LAT)
    row = (t // GRID_W).astype(F32)[:, None] * inv_freq
    col = (t % GRID_W).astype(F32)[:, None] * inv_freq
    ang = jnp.concatenate([row, row, col, col], axis=1)
    sign = jnp.tile(jnp.concatenate([-jnp.ones((quarter,), F32), jnp.ones((quarter,), F32)]), 2)
    cos = jnp.concatenate([jnp.ones((T_CTX, HD), F32), jnp.tile(jnp.cos(ang), (B_LAT, 1))], axis=0)
    sin = jnp.concatenate([jnp.zeros((T_CTX, HD), F32), jnp.tile(jnp.sin(ang) * sign, (B_LAT, 1))], axis=0)
    return cos, sin


def kernel(x_prompt, x_sample, cache_k, cache_v, state_rglru, c, c_ctx, w_mod, b_mod, norm_g, final_g, a_w_in, a_w_out, rnn_conv_w, rnn_conv_b, rnn_w_a, rnn_b_a, rnn_w_x, rnn_b_x, rnn_lam, attn_sink, ffn_w1, ffn_w3, ffn_w2, h_w_in, h_w_out, h_conv_w, h_conv_b, hf_w1, hf_b1, hf_w2, hf_b2, hf_freq, hf_w3, h_bias, moe_router, moe_router_b, moe_w_gate, moe_w_up, moe_w_down):
    x = jnp.concatenate([x_prompt.reshape(T_CTX, D), x_sample.reshape(T_LAT, D)], axis=0)
    cond = jnp.concatenate([c_ctx[None, :], c, jnp.zeros((N_COND - 1 - B_LAT, D), F32)], axis=0)
    mods_all = _ada_params(cond, w_mod, b_mod)
    mods_all = mods_all.reshape(-1, N_COND, 6, D).transpose(0, 2, 1, 3).reshape(-1, 6, N_COND, 1, D)
    cos_t, sin_t = _rope_tables()

    mods = mods_all[0]
    h = _norm(x, norm_g[0, 0], mods=mods, slots=(0, 1), out_dtype=BF16, name="norm_mix0")
    w_in = a_w_in[0]
    mm1 = functools.partial(_mm, [h], [w_in], [0], tm=TM)
    q = mm1(n_cols=D_ATTN, col_off=0, tn=512, out_dtype=F32, name="proj_q")
    kv = mm1(n_cols=2 * D_KV, col_off=D_ATTN, tn=256, out_dtype=F32, name="proj_kv")
    xy = mm1(n_cols=2 * D_RNN, col_off=D_ATTN + 2 * D_KV, tn=512, out_dtype=F32, name="proj_rnn")
    qb, kb, vb = _qk_prep(q, kv, cos_t, sin_t)
    ck = cache_k[:, 0].reshape(B_LAT, -1, D_KV)
    cv = cache_v[:, 0].reshape(B_LAT, -1, D_KV)
    o_ctx, o_lat = _attention(qb, kb, vb, ck, cv, attn_sink[0])
    rnn_w = (rnn_conv_w[0], rnn_conv_b[0], rnn_w_a[0], rnn_b_a[0], rnn_w_x[0], rnn_b_x[0], rnn_lam[0])
    r_ctx, s_ctx = _rglru(xy, jnp.zeros((B_CTX, 2, D_RNN), F32), B_CTX, L_CTX, 0, *rnn_w, name="rglru_ctx")
    r_lat, _ = _rglru(xy, state_rglru[:, 0], B_LAT, L_LAT, T_CTX, *rnn_w, name="rglru_lat")
    o_attn = jnp.concatenate([o_ctx, o_lat], axis=0)
    o_rnn = jnp.concatenate([r_ctx, r_lat], axis=0)
    w_out = a_w_out[0]
    x = _mm([o_attn, o_rnn], [w_out, w_out], [0, 1], n_cols=D, col_off=0, tn=512, tm=TM, out_dtype=F32,
            residual=(x, mods, 2), name="proj_out0")
    h = _norm(x, norm_g[0, 1], mods=mods, slots=(3, 4), out_dtype=BF16, name="norm_ffn0")
    hid = _mm([h], [ffn_w1[0]], [0], dual_w=[ffn_w3[0]], n_cols=D_FF, col_off=0, tn=512, tm=TM, out_dtype=BF16,
              name="ffn_up")
    x = _mm([hid], [ffn_w2[0]], [0], n_cols=D, col_off=0, tn=512, tm=TM, out_dtype=F32, residual=(x, mods, 5),
            name="ffn_down")

    mods = mods_all[1]
    h = _norm(x, norm_g[1, 0], mods=mods, slots=(0, 1), out_dtype=BF16, name="norm_mix1")
    u = _mm([h], [h_w_in[0]], [0], n_cols=3 * D, col_off=0, tn=512, tm=TM, out_dtype=F32, name="hy_in")
    preps = _hy_prep(u, h_conv_w[0], h_conv_b[0])
    zs = []
    for (zf, zb, x1, x2), (n_seq, L) in zip(preps, ((B_CTX, L_CTX), (B_LAT, L_LAT))):
        fc = min(HY_FC, 2 * L)
        cf, ci = _dft_mats(L, fc)
        filt = _hy_filter(L, hf_w1[0], hf_b1[0], hf_w2[0], hf_b2[0], hf_freq[0], hf_w3[0])
        spec = _hy_spectra(filt, cf)
        for o, gate in enumerate((x1, x2)):
            zc = _longconv(zb, n_seq, L, cf, ci, spec, o)
            zf, zb = _hy_gate(zc, zf, gate, h_bias[0, o])
        zs.append(zb)
    z = jnp.concatenate(zs, axis=0)
    x = _mm([z], [h_w_out[0]], [0], n_cols=D, col_off=0, tn=512, tm=TM, out_dtype=F32, residual=(x, mods, 2),
            name="hy_out")
    h, idx, g0, g1 = _norm(x, norm_g[1, 1], mods=mods, slots=(3, 4), router=(moe_router[0], moe_router_b[0]),
                           out_dtype=BF16, name="norm_moe")
    dest0, dest1, src_tok, group = _moe_plan(idx)
    xs = _moe_gather(h, src_tok)
    hid = _mm([xs], [moe_w_gate[0]], [0], dual_w=[moe_w_up[0]], n_cols=D_FFE, col_off=0, tn=512, tm=MOE_TM,
              out_dtype=BF16, group=group, name="moe_up")
    ybuf = _mm([hid], [moe_w_down[0]], [0], n_cols=D, col_off=0, tn=256, tm=MOE_TM, out_dtype=F32, group=group,
               name="moe_down")
    x = _moe_combine(ybuf, dest0, dest1, g0, g1, x, mods, 5)

    y_prompt = _norm(x, final_g, out_dtype=F32, name="final_ctx", row0=0, rows=T_CTX).reshape(B_CTX, L_CTX, D)
    y_sample = _norm(x, final_g, out_dtype=F32, name="final_lat", row0=T_CTX, rows=T_LAT).reshape(B_LAT, L_LAT, D)
    new_k = kv[:T_CTX, :D_KV].reshape(B_CTX, 1, L_CTX, N_KV, HD)
    new_v = kv[:T_CTX, D_KV:].reshape(B_CTX, 1, L_CTX, N_KV, HD)
    new_s = s_ctx.reshape(B_CTX, 1, 2, D_RNN)
    return (y_prompt, y_sample, new_k, new_v, new_s)
```

```python
import functools
import math

import jax
import jax.numpy as jnp
from jax import lax
from jax.experimental import pallas as pl
from jax.experimental.pallas import tpu as pltpu

F32 = jnp.float32
BF16 = jnp.bfloat16
I32 = jnp.int32

D = 2048
B_CTX, L_CTX = 32, 256
B_LAT, L_LAT = 4, 2048
T_CTX = B_CTX * L_CTX
T_LAT = B_LAT * L_LAT
T = T_CTX + T_LAT
GRID_W = 64
N_HEADS, N_KV, HD = 8, 2, 128
Q_PER_KV = N_HEADS // N_KV
D_ATTN = N_HEADS * HD
D_KV = N_KV * HD
WINDOW = 128
ROPE_BASE = 10000.0
D_RNN = D // 2
RNN_BLOCKS = 8
RNN_W = D_RNN // RNN_BLOCKS
RG_LRU_C = 8.0
HY_EMB = 33
HY_W = 64
HY_MIN_DECAY = math.log(1e-2) / 1.5
HY_MAX_DECAY = math.log(1e-2) / 0.3
D_FF = 5632
N_EXP = 8
D_FFE = 7168
EPS = 1e-6
NEG_INF = -1e30

LANES = 128
SUBLANES = 8
VMEM_LIMIT = 52 * 1024 * 1024
TM = 512
N_COND = 8
MOE_TM = 512
R_MOE = 2 * T + N_EXP * MOE_TM
HY_TD = 512
HY_FC = 1024


def _cparams(sem):
    return pltpu.CompilerParams(dimension_semantics=sem, vmem_limit_bytes=VMEM_LIMIT)


def _cond_of_block(i, tm):
    nb_ctx = T_CTX // tm
    return jnp.where(i < nb_ctx, 0, 1 + (i - nb_ctx) // (L_LAT // tm))


def _rows_value(refs, i, nbc):
    if len(refs) == 1:
        return refs[0][...]
    return jnp.where(i < nbc, refs[0][...], refs[1][...])


def _rows_specs(src, tm, bw, col, clamp=None):
    nbc = T_CTX // tm
    if isinstance(src, tuple):
        return ([pl.BlockSpec((tm, bw), lambda j, i, *p: (jnp.minimum(i, nbc - 1), col(j))),
                 pl.BlockSpec((tm, bw), lambda j, i, *p: (jnp.maximum(i - nbc, 0), col(j)))], list(src))
    row = (lambda i, p: i) if clamp is None else clamp
    return [pl.BlockSpec((tm, bw), lambda j, i, *p: (row(i, p), col(j)))], [src]


def _mm_body(*refs, a_counts, dual, residual, grouped, nbc):
    it = iter(refs)
    if grouped:
        chg_ref = next(it)
        next(it)
        nact_ref = next(it)
    n_a = len(a_counts)
    a_refs = [[next(it) for _ in range(c)] for c in a_counts]
    w_refs = [[next(it) for _ in range(n_a)] for _ in range(2 if dual else 1)]
    if residual:
        x_refs = [next(it) for _ in range(residual)]
        g_ref = next(it)
    o_ref = next(it)
    wb_refs = [[next(it) for _ in range(n_a)] for _ in range(2 if dual else 1)]
    i = pl.program_id(1)
    refresh = (chg_ref[i] != 0) if grouped else (i == 0)

    @pl.when(refresh)
    def _():
        for ws, wbs in zip(w_refs, wb_refs):
            for w, wb in zip(ws, wbs):
                wb[...] = w[...].astype(BF16)

    def prod(wbs):
        acc = None
        for a, wb in zip(a_refs, wbs):
            p = jnp.dot(_rows_value(a, i, nbc), wb[...], preferred_element_type=F32)
            acc = p if acc is None else acc + p
        return acc

    def compute():
        y = prod(wb_refs[0])
        if dual:
            y = y * jax.nn.sigmoid(y) * prod(wb_refs[1])
        if residual:
            y = _rows_value(x_refs, i, nbc) + g_ref[...] * y
        o_ref[...] = y.astype(o_ref.dtype)

    if grouped:
        pl.when(i < nact_ref[0])(compute)

        @pl.when(i >= nact_ref[0])
        def _():
            o_ref[...] = jnp.zeros_like(o_ref)
    else:
        compute()


def _mm(a_list, w_list, w_row_blocks, *, n_cols, col_off, tn, tm, out_dtype, name,
        dual_w=None, residual=None, group=None):
    width = lambda a: (a[0] if isinstance(a, tuple) else a).shape[1]
    grouped = group is not None
    M = a_list[0].shape[0] if grouped else T
    ks = [width(a) for a in a_list]
    dual = dual_w is not None
    assert n_cols % tn == 0 and col_off % tn == 0 and M % tm == 0
    nj, ni = n_cols // tn, M // tm
    cb = col_off // tn

    def rowblk(i, pref):
        if grouped:
            return jnp.minimum(i, pref[2][0] - 1)
        return i

    in_specs, ins, a_counts = [], [], []
    for a, ka in zip(a_list, ks):
        sp, ops = _rows_specs(a, tm, ka, lambda j: 0, clamp=rowblk)
        in_specs += sp
        ins += ops
        a_counts.append(len(ops))

    def w_spec(rb, ka):
        if grouped:
            return pl.BlockSpec((None, ka, tn), lambda j, i, *p: (p[1][rowblk(i, p)], rb, cb + j))
        return pl.BlockSpec((ka, tn), lambda j, i, *p: (rb, cb + j))

    for ws in ([w_list, dual_w] if dual else [w_list]):
        ins += list(ws)
        in_specs += [w_spec(rb, ka) for rb, ka in zip(w_row_blocks, ks)]
    n_x = 0
    if residual is not None:
        x, mods, slot = residual
        sp, ops = _rows_specs(x, tm, tn, lambda j: j)
        n_x = len(ops)
        ins += ops + [mods]
        in_specs += sp + [pl.BlockSpec((None, None, 1, tn), lambda j, i, *p: (slot, _cond_of_block(i, tm), 0, j))]
    scratch = [pltpu.VMEM((ka, tn), BF16) for ka in ks] * (2 if dual else 1)
    body = functools.partial(_mm_body, a_counts=tuple(a_counts), dual=dual, residual=n_x, grouped=grouped,
                             nbc=T_CTX // tm)
    out_spec = pl.BlockSpec((tm, tn), lambda j, i, *p: (i, j))
    gs = pltpu.PrefetchScalarGridSpec(num_scalar_prefetch=3 if grouped else 0, grid=(nj, ni),
                                      in_specs=in_specs, out_specs=out_spec, scratch_shapes=scratch)
    call = pl.pallas_call(body, grid_spec=gs, out_shape=jax.ShapeDtypeStruct((M, n_cols), out_dtype),
                          compiler_params=_cparams(("arbitrary", "arbitrary")), name=name)
    if grouped:
        return call(*group, *ins)
    return call(*ins)


def _ada_body(c_ref, w_ref, b_ref, o_ref):
    c = c_ref[...]
    s = (c * jax.nn.sigmoid(c)).astype(BF16)
    o_ref[...] = jnp.dot(s, w_ref[...].astype(BF16), preferred_element_type=F32) + b_ref[...]


def _ada_params(cond, w_mod, b_mod):
    depth = w_mod.shape[0]
    tn = 1024
    return pl.pallas_call(
        _ada_body, grid=(depth, 6 * D // tn),
        in_specs=[pl.BlockSpec((N_COND, D), lambda l, j: (0, 0)),
                  pl.BlockSpec((None, D, tn), lambda l, j: (l, 0, j)),
                  pl.BlockSpec((None, 1, tn), lambda l, j: (l, 0, j))],
        out_specs=pl.BlockSpec((None, N_COND, tn), lambda l, j: (l, 0, j)),
        out_shape=jax.ShapeDtypeStruct((depth, N_COND, 6 * D), F32),
        compiler_params=_cparams(("arbitrary", "arbitrary")), name="ada_params",
    )(cond, w_mod, b_mod.reshape(depth, 1, 6 * D))


def _norm_body(*refs, n_x, modulate, router):
    it = iter(refs)
    x_refs = [next(it) for _ in range(n_x)]
    g_ref = next(it)
    if modulate:
        sh_ref, sc_ref = next(it), next(it)
    if router:
        wr_ref, br_ref = next(it), next(it)
    o_ref = next(it)
    pid = pl.program_id(1)
    x = _rows_value(x_refs, pid, T_CTX // TM)
    y = x * lax.rsqrt(jnp.mean(x * x, axis=-1, keepdims=True) + EPS) * g_ref[...]
    if modulate:
        y = y * (1.0 + sc_ref[...]) + sh_ref[...]
    o_ref[...] = y.astype(o_ref.dtype)
    if router:
        idx_ref, g0_ref, g1_ref, cnt_ref, carry_ref = next(it), next(it), next(it), next(it), next(it)
        logits = jnp.dot(y, wr_ref[...], preferred_element_type=F32, precision=lax.Precision.HIGHEST) + br_ref[...]
        lane = lax.broadcasted_iota(I32, logits.shape, 1)
        logits = jnp.where(lane < N_EXP, logits, -jnp.inf)
        lanef = lane.astype(F32)
        m1 = jnp.max(logits, axis=-1, keepdims=True)
        i1 = jnp.min(jnp.where(logits == m1, lanef, float(LANES)), axis=-1, keepdims=True)
        rest = jnp.where(lanef == i1, -jnp.inf, logits)
        m2 = jnp.max(rest, axis=-1, keepdims=True)
        i2 = jnp.min(jnp.where(rest == m2, lanef, float(LANES)), axis=-1, keepdims=True)
        e21 = jnp.exp(m2 - m1)
        gate1 = 1.0 / (1.0 + e21)
        g0_ref[...] = jnp.broadcast_to(gate1, logits.shape)
        g1_ref[...] = jnp.broadcast_to(e21 * gate1, logits.shape)

        @pl.when(pid == 0)
        def _():
            carry_ref[...] = jnp.zeros_like(carry_ref)

        hot1 = jnp.where(lanef == i1, 1.0, 0.0)
        hot2 = jnp.where(lanef == i2, 1.0, 0.0)
        n = logits.shape[0]
        tri = jnp.where(lax.broadcasted_iota(I32, (n, n), 1) < lax.broadcasted_iota(I32, (n, n), 0), 1.0, 0.0)
        tri = tri.astype(BF16)
        before1 = jnp.dot(tri, hot1.astype(BF16), preferred_element_type=F32)
        before2 = jnp.dot(tri, hot2.astype(BF16), preferred_element_type=F32)
        tot1 = jnp.sum(hot1, axis=0, keepdims=True)
        tot2 = jnp.sum(hot2, axis=0, keepdims=True)
        carry = carry_ref[...]
        rank1 = jnp.sum(hot1 * (before1 + carry), axis=-1, keepdims=True)
        rank2 = jnp.sum(hot2 * (before2 + carry + tot1), axis=-1, keepdims=True)
        carry = carry + tot1 + tot2
        carry_ref[...] = carry
        cnt_ref[...] = jnp.broadcast_to(carry, cnt_ref.shape).astype(I32)
        packed = jnp.where(lane == 0, i1, jnp.where(lane == 1, i2, jnp.where(lane == 2, rank1,
                                                                             jnp.where(lane == 3, rank2, 0.0))))
        idx_ref[...] = packed.astype(I32)


def _norm(x, g, *, mods=None, slots=None, router=None, out_dtype, name, row0=0, rows=None):
    pair = isinstance(x, tuple)
    rows = (T if pair else x.shape[0]) if rows is None else rows
    rb0 = row0 // TM
    modulate = mods is not None
    in_specs, ins = _rows_specs(x, TM, D, lambda j: 0, clamp=lambda i, p: rb0 + i)
    n_x = len(ins)
    ins.append(g.reshape(1, D))
    in_specs.append(pl.BlockSpec((1, D), lambda j, i: (0, 0)))
    if modulate:
        for slot in slots:
            ins.append(mods)
            in_specs.append(pl.BlockSpec((None, None, 1, D),
                                         lambda j, i, slot=slot: (slot, _cond_of_block(i, TM), 0, 0)))
    out_shape = [jax.ShapeDtypeStruct((rows, D), out_dtype)]
    out_specs = [pl.BlockSpec((TM, D), lambda j, i: (i, 0))]
    scratch = []
    if router is not None:
        w_r, b_r = router
        ins += [jnp.pad(w_r, ((0, 0), (0, LANES - N_EXP))), jnp.pad(b_r, (0, LANES - N_EXP)).reshape(1, LANES)]
        in_specs += [pl.BlockSpec((D, LANES), lambda j, i: (0, 0)), pl.BlockSpec((1, LANES), lambda j, i: (0, 0))]
        out_shape += [jax.ShapeDtypeStruct((rows, LANES), I32), jax.ShapeDtypeStruct((rows, LANES), F32),
                      jax.ShapeDtypeStruct((rows, LANES), F32), jax.ShapeDtypeStruct((SUBLANES, LANES), I32)]
        out_specs += [pl.BlockSpec((TM, LANES), lambda j, i: (i, 0))] * 3
        out_specs += [pl.BlockSpec((SUBLANES, LANES), lambda j, i: (0, 0))]
        scratch = [pltpu.VMEM((1, LANES), F32)]
    body = functools.partial(_norm_body, n_x=n_x, modulate=modulate, router=router is not None)
    res = pl.pallas_call(body, grid=(1, rows // TM), in_specs=in_specs, out_specs=out_specs, out_shape=out_shape,
                         scratch_shapes=scratch, compiler_params=_cparams(("arbitrary", "arbitrary")), name=name)(*ins)
    return res if router is not None else res[0]


def _swap32(x):
    up = jnp.concatenate([x[:, 32:], x[:, :32]], axis=1)
    down = jnp.concatenate([x[:, 96:], x[:, :96]], axis=1)
    lane = lax.broadcasted_iota(I32, x.shape, 1)
    return jnp.where((lane % 64) < 32, up, down)


def _qkprep_body(q_ref, kv_ref, cos_ref, sin_ref, qo_ref, ko_ref, vo_ref):
    cos, sin = cos_ref[...], sin_ref[...]
    for h in range(N_HEADS):
        x = q_ref[:, h * HD:(h + 1) * HD]
        qo_ref[:, h * HD:(h + 1) * HD] = (x * cos + _swap32(x) * sin).astype(BF16)
    for h in range(N_KV):
        x = kv_ref[:, h * HD:(h + 1) * HD]
        ko_ref[:, h * HD:(h + 1) * HD] = (x * cos + _swap32(x) * sin).astype(BF16)
    vo_ref[...] = kv_ref[:, D_KV:].astype(BF16)


def _qk_prep(q, kv, cos_t, sin_t):
    return pl.pallas_call(
        _qkprep_body, grid=(T // TM,),
        in_specs=[pl.BlockSpec((TM, D_ATTN), lambda i: (i, 0)), pl.BlockSpec((TM, 2 * D_KV), lambda i: (i, 0)),
                  pl.BlockSpec((TM, HD), lambda i: (i, 0)), pl.BlockSpec((TM, HD), lambda i: (i, 0))],
        out_specs=[pl.BlockSpec((TM, D_ATTN), lambda i: (i, 0)), pl.BlockSpec((TM, D_KV), lambda i: (i, 0)),
                   pl.BlockSpec((TM, D_KV), lambda i: (i, 0))],
        out_shape=[jax.ShapeDtypeStruct((T, D_ATTN), BF16), jax.ShapeDtypeStruct((T, D_KV), BF16),
                   jax.ShapeDtypeStruct((T, D_KV), BF16)],
        compiler_params=_cparams(("arbitrary",)), name="qk_prep")(q, kv, cos_t, sin_t)


def _attn_core(q_ref, o_ref, sink_ref, kvh, kall, vall, mask):
    scale = HD ** -0.5
    for g in range(Q_PER_KV):
        qh = q_ref[:, g * HD:(g + 1) * HD]
        s = lax.dot_general(qh, kall, (((1,), (1,)), ((), ())), preferred_element_type=F32) * scale
        if mask is not None:
            s = jnp.where(mask, s, NEG_INF)
        sk = sink_ref[kvh * Q_PER_KV + g]
        m = jnp.maximum(jnp.max(s, axis=-1, keepdims=True), sk)
        p = jnp.exp(s - m)
        denom = jnp.sum(p, axis=-1, keepdims=True) + jnp.exp(sk - m)
        o = jnp.dot(p.astype(BF16), vall, preferred_element_type=F32) / denom
        o_ref[:, g * HD:(g + 1) * HD] = o.astype(o_ref.dtype)


def _attn_ctx_body(sink_ref, q_ref, k_ref, v_ref, o_ref):
    _attn_core(q_ref, o_ref, sink_ref, pl.program_id(1), k_ref[...], v_ref[...], None)


def _attn_lat_body(sink_ref, q_ref, kp_ref, kc_ref, kn_ref, vp_ref, vc_ref, vn_ref, ck_ref, cv_ref, o_ref):
    qb = pl.program_id(1)
    kall = jnp.concatenate([kp_ref[...], kc_ref[...], kn_ref[...], ck_ref[...].astype(BF16)], axis=0)
    vall = jnp.concatenate([vp_ref[...], vc_ref[...], vn_ref[...], cv_ref[...].astype(BF16)], axis=0)
    nk = kall.shape[0]
    qpos = qb * WINDOW + lax.broadcasted_iota(I32, (WINDOW, nk), 0)
    col = lax.broadcasted_iota(I32, (WINDOW, nk), 1)
    kpos = (qb - 1) * WINDOW + col
    in_win = (jnp.abs(kpos - qpos) <= WINDOW) & (kpos >= 0) & (kpos < L_LAT)
    mask = in_win | (col >= 3 * WINDOW)
    _attn_core(q_ref, o_ref, sink_ref, pl.program_id(2), kall, vall, mask)


def _attention(qb, kb, vb, cache_k, cache_v, sink):
    qw = Q_PER_KV * HD
    smem = pl.BlockSpec(memory_space=pltpu.SMEM)
    o_ctx = pl.pallas_call(
        _attn_ctx_body, grid=(B_CTX, N_KV),
        in_specs=[smem, pl.BlockSpec((L_CTX, qw), lambda b, h: (b, h)),
                  pl.BlockSpec((L_CTX, HD), lambda b, h: (b, h)), pl.BlockSpec((L_CTX, HD), lambda b, h: (b, h))],
        out_specs=pl.BlockSpec((L_CTX, qw), lambda b, h: (b, h)),
        out_shape=jax.ShapeDtypeStruct((T_CTX, D_ATTN), BF16),
        compiler_params=_cparams(("arbitrary", "arbitrary")), name="attn_ctx")(sink, qb, kb, vb)
    nb = L_LAT // WINDOW
    base = T_CTX // WINDOW

    def cur(b, i, h):
        return (base + b * nb + i, h)

    def prv(b, i, h):
        return (base + b * nb + jnp.maximum(i - 1, 0), h)

    def nxt(b, i, h):
        return (base + b * nb + jnp.minimum(i + 1, nb - 1), h)

    blk = lambda f: pl.BlockSpec((WINDOW, HD), f)
    cspec = pl.BlockSpec((None, cache_k.shape[1], HD), lambda b, i, h: (b, 0, h))
    o_lat = pl.pallas_call(
        _attn_lat_body, grid=(B_LAT, nb, N_KV),
        in_specs=[smem, pl.BlockSpec((WINDOW, qw), cur), blk(prv), blk(cur), blk(nxt), blk(prv), blk(cur), blk(nxt),
                  cspec, cspec],
        out_specs=pl.BlockSpec((WINDOW, qw), lambda b, i, h: (b * nb + i, h)),
        out_shape=jax.ShapeDtypeStruct((T_LAT, D_ATTN), BF16),
        compiler_params=_cparams(("arbitrary", "arbitrary", "arbitrary")), name="attn_lat",
    )(sink, qb, kb, kb, kb, vb, vb, vb, cache_k, cache_v)
    return o_ctx, o_lat


def _rglru_body(x_ref, y_ref, h0_ref, cw_ref, cb_ref, wa_ref, ba_ref, wx_ref, bx_ref, lam_ref,
                o_ref, fin_ref, a_s, b_s, h_s):
    L = x_ref.shape[0]
    x = x_ref[...]
    row = lax.broadcasted_iota(I32, x.shape, 0)
    xc = cb_ref[...] + x * cw_ref[2:3, :]
    xc = xc + jnp.where(row >= 2, pltpu.roll(x, 2, 0), 0.0) * cw_ref[0:1, :]
    xc = xc + jnp.where(row >= 1, pltpu.roll(x, 1, 0), 0.0) * cw_ref[1:2, :]
    xc = xc + jnp.where(row < L - 1, pltpu.roll(x, L - 1, 0), 0.0) * cw_ref[3:4, :]
    xcb = xc.astype(BF16)
    for d in range(2):
        ga = jnp.dot(xcb, wa_ref[d].astype(BF16), preferred_element_type=F32) + ba_ref[d:d + 1, :]
        gx = jnp.dot(xcb, wx_ref[d].astype(BF16), preferred_element_type=F32) + bx_ref[d:d + 1, :]
        log_a = -RG_LRU_C * jax.nn.sigmoid(ga) * jax.nn.softplus(-lam_ref[d:d + 1, :])
        a = jnp.exp(log_a)
        a_s[d] = a
        b_s[d] = jnp.sqrt(-jnp.tanh(log_a) * (1.0 + a * a)) * jax.nn.sigmoid(gx) * xc

    nt = L // SUBLANES
    r8 = lax.broadcasted_iota(I32, (SUBLANES, RNN_W), 0)

    def tile_scan(d, blk, carry, reverse):
        r0 = pl.multiple_of(blk * SUBLANES, SUBLANES)
        a = a_s[d, pl.ds(r0, SUBLANES), :]
        b = b_s[d, pl.ds(r0, SUBLANES), :]
        for k in (1, 2, 4):
            sh = SUBLANES - k if reverse else k
            m = (r8 < SUBLANES - k) if reverse else (r8 >= k)
            b = jnp.where(m, a * pltpu.roll(b, sh, 0) + b, b)
            a = jnp.where(m, a * pltpu.roll(a, sh, 0), a)
        h = a * carry + b
        h_s[d, pl.ds(r0, SUBLANES), :] = h
        last = h[0:1, :] if reverse else h[SUBLANES - 1:SUBLANES, :]
        return jnp.broadcast_to(last, (SUBLANES, RNN_W))

    def step(i, carry):
        return tile_scan(0, i, carry[0], False), tile_scan(1, nt - 1 - i, carry[1], True)

    init = (jnp.broadcast_to(h0_ref[0:1, :], (SUBLANES, RNN_W)), jnp.broadcast_to(h0_ref[1:2, :], (SUBLANES, RNN_W)))
    lax.fori_loop(0, nt, step, init)
    hsum = h_s[0] + h_s[1]
    o_ref[...] = (hsum * jax.nn.gelu(y_ref[...])).astype(o_ref.dtype)
    fin_ref[0:1, :] = h_s[0, L - 1:L, :]
    fin_ref[1:2, :] = h_s[1, 0:1, :]


def _rglru(xy, h0, n_seq, L, row0, conv_w, conv_b, w_a, b_a, w_x, b_x, lam, name):
    rb0 = row0 // L
    nb = RNN_BLOCKS
    vec = lambda r: pl.BlockSpec((r, RNN_W), lambda b, n: (0, n))
    wsp = pl.BlockSpec((2, None, RNN_W, RNN_W), lambda b, n: (0, n, 0, 0))
    return pl.pallas_call(
        _rglru_body, grid=(n_seq, nb),
        in_specs=[pl.BlockSpec((L, RNN_W), lambda b, n: (rb0 + b, n)),
                  pl.BlockSpec((L, RNN_W), lambda b, n: (rb0 + b, nb + n)),
                  pl.BlockSpec((None, 2, RNN_W), lambda b, n: (b, 0, n)),
                  vec(4), vec(1), wsp, vec(2), wsp, vec(2), vec(2)],
        out_specs=[pl.BlockSpec((L, RNN_W), lambda b, n: (b, n)),
                   pl.BlockSpec((None, 2, RNN_W), lambda b, n: (b, 0, n))],
        out_shape=[jax.ShapeDtypeStruct((n_seq * L, D_RNN), BF16), jax.ShapeDtypeStruct((n_seq, 2, D_RNN), F32)],
        scratch_shapes=[pltpu.VMEM((2, L, RNN_W), F32)] * 3,
        compiler_params=_cparams(("arbitrary", "arbitrary")), name=name,
    )(xy, xy, h0, conv_w, conv_b.reshape(1, D_RNN), w_a, b_a, w_x, b_x, lam)


def _hy_filter_body(fv_ref, w1_ref, b1_ref, w2_ref, b2_ref, fr_ref, w3_ref, dl_ref, o_ref, *, L):
    hp = lax.Precision.HIGHEST
    rowi = lax.broadcasted_iota(I32, (L, LANES), 0)
    lane = lax.broadcasted_iota(I32, (L, LANES), 1)

    def features(pos):
        posf = pos.astype(F32)
        tt = posf / (L - 1)
        ang = fv_ref[...] * (2.0 * math.pi * posf / L)
        z = jnp.where(lane == 0, tt, jnp.where(lane <= 16, jnp.cos(ang), jnp.where(lane <= 32, -jnp.sin(ang), 0.0)))
        return z, tt[:, 0:1]

    def mlp(z):
        h = jnp.sin(fr_ref[0:1, :] * (jnp.dot(z, w1_ref[...], preferred_element_type=F32, precision=hp) + b1_ref[...]))
        h = jnp.sin(fr_ref[1:2, :] * (jnp.dot(h, w2_ref[...], preferred_element_type=F32, precision=hp) + b2_ref[...]))
        return jnp.dot(h, w3_ref[...], preferred_element_type=F32, precision=hp)

    is_bwd = (pl.program_id(0) // (D // o_ref.shape[1])) % 2 == 1
    pos = jnp.where(is_bwd, L - rowi, rowi)
    z, tt = features(pos)
    filt = mlp(z) * jnp.exp(-tt * dl_ref[...])
    dead = is_bwd & (lax.broadcasted_iota(I32, filt.shape, 0) == 0)
    o_ref[...] = jnp.where(dead, 0.0, filt).astype(o_ref.dtype)


def _hy_filter(L, f_w1, f_b1, f_w2, f_b2, f_freq, f_w3):
    bands = (HY_EMB - 1) // 2
    f = jnp.linspace(1e-4, bands - 1, bands, dtype=F32)
    fv = jnp.zeros((LANES,), F32).at[1:1 + bands].set(f).at[1 + bands:1 + 2 * bands].set(f).reshape(1, LANES)
    padw = lambda w, r, c: jnp.pad(w.astype(F32), ((0, r - w.shape[0]), (0, c - w.shape[1])))
    padv = lambda v: jnp.pad(v.astype(F32), (0, LANES - v.shape[0])).reshape(1, LANES)
    w1, w2 = padw(f_w1, LANES, LANES), padw(f_w2, LANES, LANES)
    w3 = padw(f_w3, LANES, f_w3.shape[1])
    fr = jnp.pad(f_freq.astype(F32), ((0, 0), (0, LANES - HY_W)))
    deltas = jnp.abs(jnp.linspace(HY_MIN_DECAY, HY_MAX_DECAY, D, dtype=F32)).reshape(1, D)
    tn = 1024
    per = D // tn
    full = lambda r: pl.BlockSpec((r, LANES), lambda j: (0, 0))
    return pl.pallas_call(
        functools.partial(_hy_filter_body, L=L), grid=(4 * per,),
        in_specs=[full(1), full(LANES), full(1), full(LANES), full(1), full(2),
                  pl.BlockSpec((LANES, tn), lambda j: (0, j)), pl.BlockSpec((1, tn), lambda j: (0, j % per))],
        out_specs=pl.BlockSpec((L, tn), lambda j: (0, j)),
        out_shape=jax.ShapeDtypeStruct((L, 4 * D), BF16),
        compiler_params=_cparams(("arbitrary",)), name=f"hy_filter_{L}",
    )(fv, w1, padv(f_b1), w2, padv(f_b2), fr, w3, deltas)


def _dft_mats(L, fc):
    n = 2 * L
    h = fc // 2
    nf = L // h
    k = jnp.arange(L, dtype=I32)[:, None]
    t = jnp.arange(L, dtype=I32)[None, :]
    ang = ((k * t) % n).astype(F32) * (2.0 * math.pi / n)
    cos, sin = jnp.cos(ang), jnp.sin(ang)
    nyq = jnp.where(t % 2 == 0, 1.0, -1.0).astype(F32)
    is0 = k == 0
    re_f = cos
    im_f = jnp.where(is0, nyq, -sin)
    re_i = jnp.where(is0, 1.0, 2.0) * cos / n
    im_i = jnp.where(is0, nyq, -2.0 * sin) / n
    cf = jnp.concatenate([re_f.reshape(nf, h, L), im_f.reshape(nf, h, L)], axis=1)
    ci = jnp.concatenate([re_i.reshape(nf, h, L), im_i.reshape(nf, h, L)], axis=1)
    return cf.astype(BF16), jnp.swapaxes(ci, 1, 2).astype(BF16)


def _spec_body(cf_ref, c1_ref, c2_ref, o_ref):
    f = pl.program_id(1)
    fc = cf_ref.shape[0]
    h = fc // 2
    z1 = jnp.dot(cf_ref[...], c1_ref[...], preferred_element_type=F32)
    z2 = jnp.dot(cf_ref[...], c2_ref[...], preferred_element_type=F32)
    r = lax.broadcasted_iota(I32, z1.shape, 0)
    kk = f * h + jnp.where(r < h, r, r - h)
    nyq = (r == h) & (f == 0)
    odd = (kk % 2 == 1) & jnp.logical_not(nyq)
    o_ref[...] = z1 + jnp.where(odd, -z2, z2)


def _hy_spectra(filt, cf):
    nf, fc, L = cf.shape
    td = HY_TD
    per = D // td
    return pl.pallas_call(
        _spec_body, grid=(2 * per, nf),
        in_specs=[pl.BlockSpec((None, fc, L), lambda c, f: (f, 0, 0)),
                  pl.BlockSpec((L, td), lambda c, f: (0, (c // per) * 2 * per + c % per)),
                  pl.BlockSpec((L, td), lambda c, f: (0, (c // per) * 2 * per + per + c % per))],
        out_specs=pl.BlockSpec((None, fc, td), lambda c, f: (f, 0, c)),
        out_shape=jax.ShapeDtypeStruct((nf, fc, 2 * D), F32),
        compiler_params=_cparams(("arbitrary", "arbitrary")), name=f"hy_spectra_{L}")(cf, filt, filt)


def _hy_prep_body(*refs, L):
    u_refs, cw_refs, cb_refs = refs[0:3], refs[3:6], refs[6:9]
    zf_ref, zb_ref, x1_ref, x2_ref = refs[9:]
    rows = u_refs[0].shape[0]
    pos = lax.broadcasted_iota(I32, u_refs[0].shape, 0) % L

    def conv(u_ref, cw_ref, cb_ref):
        u = u_ref[...]
        uc = cb_ref[...] + u * cw_ref[1:2, :]
        uc = uc + jnp.where(pos >= 1, pltpu.roll(u, 1, 0), 0.0) * cw_ref[0:1, :]
        return uc + jnp.where(pos < L - 1, pltpu.roll(u, rows - 1, 0), 0.0) * cw_ref[2:3, :]

    v = conv(u_refs[0], cw_refs[0], cb_refs[0])
    zf_ref[...] = v
    zb_ref[...] = v.astype(BF16)
    x1_ref[...] = conv(u_refs[1], cw_refs[1], cb_refs[1])
    x2_ref[...] = conv(u_refs[2], cw_refs[2], cb_refs[2])


def _hy_prep(u, conv_w, conv_b):
    tc = 256
    per = D // tc
    cb = conv_b.reshape(1, 3 * D)
    outs = []
    for row0, rows, L in ((0, T_CTX, L_CTX), (T_CTX, T_LAT, L_LAT)):
        tr = max(L, TM)
        rb0 = row0 // tr
        third = lambda r, k: pl.BlockSpec((r, tc), lambda i, c, k=k: (rb0 + i if r == tr else 0, k * per + c))
        ospec = pl.BlockSpec((tr, tc), lambda i, c: (i, c))
        outs.append(pl.pallas_call(
            functools.partial(_hy_prep_body, L=L), grid=(rows // tr, per),
            in_specs=[third(tr, k) for k in range(3)] + [third(3, k) for k in range(3)] + [third(1, k) for k in range(3)],
            out_specs=[ospec] * 4,
            out_shape=[jax.ShapeDtypeStruct((rows, D), F32), jax.ShapeDtypeStruct((rows, D), BF16),
                       jax.ShapeDtypeStruct((rows, D), F32), jax.ShapeDtypeStruct((rows, D), F32)],
            compiler_params=_cparams(("arbitrary", "arbitrary")), name=f"hy_prep_{L}",
        )(u, u, u, conv_w, conv_w, conv_w, cb, cb, cb))
    return outs


def _longconv_body(z_ref, cf_ref, ci_ref, s_ref, o_ref, acc_ref):
    f = pl.program_id(2)
    fc = cf_ref.shape[0]
    h = fc // 2
    zf = jnp.dot(cf_ref[...], z_ref[...], preferred_element_type=F32)
    zre, zim = zf[:h], zf[h:]
    sre, sim = s_ref[:h, :], s_ref[h:, :]
    first = (lax.broadcasted_iota(I32, zre.shape, 0) == 0) & (f == 0)
    yre = zre * sre - jnp.where(first, 0.0, zim * sim)
    yim = jnp.where(first, zim * sim, zre * sim + zim * sre)
    y = jnp.concatenate([yre, yim], axis=0).astype(BF16)
    contrib = jnp.dot(ci_ref[...], y, preferred_element_type=F32)

    @pl.when(f == 0)
    def _():
        acc_ref[...] = contrib

    @pl.when(f > 0)
    def _():
        acc_ref[...] += contrib

    @pl.when(f == pl.num_programs(2) - 1)
    def _():
        o_ref[...] = acc_ref[...]


def _longconv(z, n_seq, L, cf, ci, spec, order):
    nf, fc, _ = cf.shape
    td = HY_TD if L > TM else D
    per = D // td
    return pl.pallas_call(
        _longconv_body, grid=(n_seq, per, nf),
        in_specs=[pl.BlockSpec((L, td), lambda b, c, f: (b, c)),
                  pl.BlockSpec((None, fc, L), lambda b, c, f: (f, 0, 0)),
                  pl.BlockSpec((None, L, fc), lambda b, c, f: (f, 0, 0)),
                  pl.BlockSpec((None, fc, td), lambda b, c, f: (f, 0, order * per + c))],
        out_specs=pl.BlockSpec((L, td), lambda b, c, f: (b, c)),
        out_shape=jax.ShapeDtypeStruct((n_seq * L, D), F32),
        scratch_shapes=[pltpu.VMEM((L, td), F32)],
        compiler_params=_cparams(("arbitrary", "arbitrary", "arbitrary")), name=f"longconv_{L}_{order}",
    )(z, cf, ci, spec)


def _hy_gate_body(zc_ref, z_ref, x_ref, b_ref, of_ref, ob_ref):
    z = x_ref[...] * (zc_ref[...] + z_ref[...] * b_ref[...])
    of_ref[...] = z
    ob_ref[...] = z.astype(BF16)


def _hy_gate(zc, z, x, bias):
    rows = zc.shape[0]
    blk = pl.BlockSpec((TM, D), lambda i: (i, 0))
    return pl.pallas_call(
        _hy_gate_body, grid=(rows // TM,),
        in_specs=[blk, blk, blk, pl.BlockSpec((1, D), lambda i: (0, 0))],
        out_specs=[blk, blk],
        out_shape=[jax.ShapeDtypeStruct((rows, D), F32), jax.ShapeDtypeStruct((rows, D), BF16)],
        compiler_params=_cparams(("arbitrary",)), name="hy_gate")(zc, z, x, bias.reshape(1, D))


DISPATCH_ROWS = 256
COMBINE_ROWS = 128
DMA_UNROLL = 8


def _dispatch_body(d0_ref, d1_ref, h_ref, zero_ref, o_ref, sem):
    del zero_ref
    base = pl.program_id(0) * DISPATCH_ROWS

    def issue(q, c):
        for u in range(DMA_UNROLL):
            r = q * DMA_UNROLL + u
            pltpu.make_async_copy(h_ref.at[r], o_ref.at[d0_ref[base + r]], sem).start(priority=0)
            pltpu.make_async_copy(h_ref.at[r], o_ref.at[d1_ref[base + r]], sem).start(priority=1)
        return c

    def drain(q, c):
        for _ in range(2 * DMA_UNROLL):
            pltpu.make_async_copy(h_ref.at[0], o_ref.at[0], sem).wait()
        return c

    lax.fori_loop(0, DISPATCH_ROWS // DMA_UNROLL, issue, 0)
    lax.fori_loop(0, DISPATCH_ROWS // DMA_UNROLL, drain, 0)


def _moe_dispatch(h, dest0, dest1):
    sub = D // LANES
    gs = pltpu.PrefetchScalarGridSpec(
        num_scalar_prefetch=2, grid=(T // DISPATCH_ROWS,),
        in_specs=[pl.BlockSpec((DISPATCH_ROWS, sub, LANES), lambda i, *p: (i, 0, 0)),
                  pl.BlockSpec(memory_space=pl.ANY)],
        out_specs=pl.BlockSpec(memory_space=pl.ANY),
        scratch_shapes=[pltpu.SemaphoreType.DMA(())])
    out = pl.pallas_call(_dispatch_body, grid_spec=gs, out_shape=jax.ShapeDtypeStruct((R_MOE, sub, LANES), BF16),
                         input_output_aliases={3: 0}, compiler_params=_cparams(("arbitrary",)), name="moe_dispatch",
                         )(dest0, dest1, h.reshape(T, sub, LANES), jnp.zeros((R_MOE, sub, LANES), BF16))
    return out.reshape(R_MOE, D)


def _combine_body(d0_ref, d1_ref, y_ref, x_ref, g0_ref, g1_ref, gate_ref, o_ref, buf, sems):
    i = pl.program_id(0)

    def issue(step, slot):
        base = step * COMBINE_ROWS

        def body(q, c):
            for u in range(DMA_UNROLL):
                r = q * DMA_UNROLL + u
                pltpu.make_async_copy(y_ref.at[d0_ref[base + r]], buf.at[slot, 0, r], sems.at[slot]).start(priority=0)
                pltpu.make_async_copy(y_ref.at[d1_ref[base + r]], buf.at[slot, 1, r], sems.at[slot]).start(priority=1)
            return c

        lax.fori_loop(0, COMBINE_ROWS // DMA_UNROLL, body, 0)

    slot = i % 2

    @pl.when(i == 0)
    def _():
        issue(0, 0)

    @pl.when(i + 1 < pl.num_programs(0))
    def _():
        issue(i + 1, 1 - slot)

    def drain(q, c):
        for _ in range(2 * DMA_UNROLL):
            pltpu.make_async_copy(y_ref.at[0], buf.at[slot, 0, 0], sems.at[slot]).wait()
        return c

    lax.fori_loop(0, COMBINE_ROWS // DMA_UNROLL, drain, 0)
    y = g0_ref[...] * buf[slot, 0] + g1_ref[...] * buf[slot, 1]
    o_ref[...] = x_ref[...] + gate_ref[...] * y


def _moe_combine(ybuf, dest0, dest1, g0, g1, x, mods, slot):
    sub = D // LANES
    tok = pl.BlockSpec((COMBINE_ROWS, sub, LANES), lambda i, *p: (i, 0, 0))
    gsp = pl.BlockSpec((COMBINE_ROWS, 1, LANES), lambda i, *p: (i, 0, 0))
    gs = pltpu.PrefetchScalarGridSpec(
        num_scalar_prefetch=2, grid=(T // COMBINE_ROWS,),
        in_specs=[pl.BlockSpec(memory_space=pl.ANY), tok, gsp, gsp,
                  pl.BlockSpec((None, None, sub, LANES), lambda i, *p: (slot, _cond_of_block(i, COMBINE_ROWS), 0, 0))],
        out_specs=tok,
        scratch_shapes=[pltpu.VMEM((2, 2, COMBINE_ROWS, sub, LANES), F32), pltpu.SemaphoreType.DMA((2,))])
    out = pl.pallas_call(_combine_body, grid_spec=gs, out_shape=jax.ShapeDtypeStruct((T, sub, LANES), F32),
                         compiler_params=_cparams(("arbitrary",)), name="moe_combine",
                         )(dest0, dest1, ybuf.reshape(R_MOE, sub, LANES), x.reshape(T, sub, LANES),
                           g0.reshape(T, 1, LANES), g1.reshape(T, 1, LANES), mods.reshape(6, N_COND, sub, LANES))
    return out.reshape(T, D)


def _moe_plan(idx, cnt):
    counts = cnt[0, :N_EXP]
    padded = (counts + MOE_TM - 1) // MOE_TM * MOE_TM
    p_ends = jnp.cumsum(padded)
    p_starts = p_ends - padded
    experts = jnp.arange(N_EXP, dtype=I32)[None, :]

    def dest(e, rank):
        return jnp.sum(jnp.where(e[:, None] == experts, p_starts[None, :], 0), axis=1) + rank

    def groups(tm):
        nblk = R_MOE // tm
        blk_start = jnp.arange(nblk, dtype=I32) * tm
        gid = jnp.minimum(jnp.sum((blk_start[:, None] >= p_ends[None, :]).astype(I32), axis=1), N_EXP - 1)
        nact = (p_ends[-1] // tm).astype(I32).reshape(1)
        gid = jnp.where(jnp.arange(nblk) < nact[0], gid, gid[nact[0] - 1])
        chg = jnp.concatenate([jnp.ones((1,), I32), (gid[1:] != gid[:-1]).astype(I32)])
        return chg, gid, nact

    return dest(idx[:, 0], idx[:, 2]), dest(idx[:, 1], idx[:, 3]), groups


def _rope_tables():
    quarter = HD // 4
    inv_freq = ROPE_BASE ** (-jnp.arange(quarter, dtype=F32) / quarter)
    t = jnp.arange(L_LAT)
    row = (t // GRID_W).astype(F32)[:, None] * inv_freq
    col = (t % GRID_W).astype(F32)[:, None] * inv_freq
    ang = jnp.concatenate([row, row, col, col], axis=1)
    sign = jnp.tile(jnp.concatenate([-jnp.ones((quarter,), F32), jnp.ones((quarter,), F32)]), 2)
    cos = jnp.concatenate([jnp.ones((T_CTX, HD), F32), jnp.tile(jnp.cos(ang), (B_LAT, 1))], axis=0)
    sin = jnp.concatenate([jnp.zeros((T_CTX, HD), F32), jnp.tile(jnp.sin(ang) * sign, (B_LAT, 1))], axis=0)
    return cos, sin


def kernel(x_prompt, x_sample, cache_k, cache_v, state_rglru, c, c_ctx, w_mod, b_mod, norm_g, final_g, a_w_in, a_w_out, rnn_conv_w, rnn_conv_b, rnn_w_a, rnn_b_a, rnn_w_x, rnn_b_x, rnn_lam, attn_sink, ffn_w1, ffn_w3, ffn_w2, h_w_in, h_w_out, h_conv_w, h_conv_b, hf_w1, hf_b1, hf_w2, hf_b2, hf_freq, hf_w3, h_bias, moe_router, moe_router_b, moe_w_gate, moe_w_up, moe_w_down):
    x = (x_prompt.reshape(T_CTX, D), x_sample.reshape(T_LAT, D))
    cond = jnp.concatenate([c_ctx[None, :], c, jnp.zeros((N_COND - 1 - B_LAT, D), F32)], axis=0)
    mods_all = _ada_params(cond, w_mod, b_mod)
    mods_all = mods_all.reshape(-1, N_COND, 6, D).transpose(0, 2, 1, 3).reshape(-1, 6, N_COND, 1, D)
    cos_t, sin_t = _rope_tables()

    mods = mods_all[0]
    h = _norm(x, norm_g[0, 0], mods=mods, slots=(0, 1), out_dtype=BF16, name="norm_mix0")
    w_in = a_w_in[0]
    mm1 = functools.partial(_mm, [h], [w_in], [0], tm=2 * TM)
    q = mm1(n_cols=D_ATTN, col_off=0, tn=1024, out_dtype=F32, name="proj_q")
    kv = mm1(n_cols=2 * D_KV, col_off=D_ATTN, tn=512, out_dtype=F32, name="proj_kv")
    xy = mm1(n_cols=2 * D_RNN, col_off=D_ATTN + 2 * D_KV, tn=512, out_dtype=F32, name="proj_rnn")
    qb, kb, vb = _qk_prep(q, kv, cos_t, sin_t)
    ck = cache_k[:, 0].reshape(B_LAT, -1, D_KV)
    cv = cache_v[:, 0].reshape(B_LAT, -1, D_KV)
    o_ctx, o_lat = _attention(qb, kb, vb, ck, cv, attn_sink[0])
    rnn_w = (rnn_conv_w[0], rnn_conv_b[0], rnn_w_a[0], rnn_b_a[0], rnn_w_x[0], rnn_b_x[0], rnn_lam[0])
    r_ctx, s_ctx = _rglru(xy, jnp.zeros((B_CTX, 2, D_RNN), F32), B_CTX, L_CTX, 0, *rnn_w, name="rglru_ctx")
    r_lat, _ = _rglru(xy, state_rglru[:, 0], B_LAT, L_LAT, T_CTX, *rnn_w, name="rglru_lat")
    w_out = a_w_out[0]
    x = _mm([(o_ctx, o_lat), (r_ctx, r_lat)], [w_out, w_out], [0, 1], n_cols=D, col_off=0, tn=1024, tm=TM,
            out_dtype=F32, residual=(x, mods, 2), name="proj_out0")
    h = _norm(x, norm_g[0, 1], mods=mods, slots=(3, 4), out_dtype=BF16, name="norm_ffn0")
    hid = _mm([h], [ffn_w1[0]], [0], dual_w=[ffn_w3[0]], n_cols=D_FF, col_off=0, tn=512, tm=2 * TM,
              out_dtype=BF16, name="ffn_up")
    x = _mm([hid], [ffn_w2[0]], [0], n_cols=D, col_off=0, tn=512, tm=TM, out_dtype=F32, residual=(x, mods, 5),
            name="ffn_down")

    mods = mods_all[1]
    h = _norm(x, norm_g[1, 0], mods=mods, slots=(0, 1), out_dtype=BF16, name="norm_mix1")
    u = _mm([h], [h_w_in[0]], [0], n_cols=3 * D, col_off=0, tn=1024, tm=2 * TM, out_dtype=F32, name="hy_in")
    preps = _hy_prep(u, h_conv_w[0], h_conv_b[0])
    zs = []
    for (zf, zb, x1, x2), (n_seq, L) in zip(preps, ((B_CTX, L_CTX), (B_LAT, L_LAT))):
        fc = min(HY_FC, 2 * L)
        cf, ci = _dft_mats(L, fc)
        filt = _hy_filter(L, hf_w1[0], hf_b1[0], hf_w2[0], hf_b2[0], hf_freq[0], hf_w3[0])
        spec = _hy_spectra(filt, cf)
        for o, gate in enumerate((x1, x2)):
            zc = _longconv(zb, n_seq, L, cf, ci, spec, o)
            zf, zb = _hy_gate(zc, zf, gate, h_bias[0, o])
        zs.append(zb)
    x = _mm([tuple(zs)], [h_w_out[0]], [0], n_cols=D, col_off=0, tn=1024, tm=TM, out_dtype=F32,
            residual=(x, mods, 2), name="hy_out")
    h, idx, g0, g1, cnt = _norm(x, norm_g[1, 1], mods=mods, slots=(3, 4), router=(moe_router[0], moe_router_b[0]),
                                out_dtype=BF16, name="norm_moe")
    dest0, dest1, groups = _moe_plan(idx, cnt)
    xs = _moe_dispatch(h, dest0, dest1)
    hid = _mm([xs], [moe_w_gate[0]], [0], dual_w=[moe_w_up[0]], n_cols=D_FFE, col_off=0, tn=512, tm=MOE_TM,
              out_dtype=BF16, group=groups(MOE_TM), name="moe_up")
    ybuf = _mm([hid], [moe_w_down[0]], [0], n_cols=D, col_off=0, tn=512, tm=MOE_TM // 2, out_dtype=F32,
               group=groups(MOE_TM // 2), name="moe_down")
    x = _moe_combine(ybuf, dest0, dest1, g0, g1, x, mods, 5)

    y_prompt = _norm(x, final_g, out_dtype=F32, name="final_ctx", row0=0, rows=T_CTX).reshape(B_CTX, L_CTX, D)
    y_sample = _norm(x, final_g, out_dtype=F32, name="final_lat", row0=T_CTX, rows=T_LAT).reshape(B_LAT, L_LAT, D)
    new_k = kv[:T_CTX, :D_KV].reshape(B_CTX, 1, L_CTX, N_KV, HD)
    new_v = kv[:T_CTX, D_KV:].reshape(B_CTX, 1, L_CTX, N_KV, HD)
    new_s = s_ctx.reshape(B_CTX, 1, 2, D_RNN)
    return (y_prompt, y_sample, new_k, new_v, new_s)
```

```python
import functools
import math

import jax
import jax.numpy as jnp
from jax import lax
from jax.experimental import pallas as pl
from jax.experimental.pallas import tpu as pltpu

F32 = jnp.float32
BF16 = jnp.bfloat16
I32 = jnp.int32

D = 2048
B_CTX, L_CTX = 32, 256
B_LAT, L_LAT = 4, 2048
T_CTX = B_CTX * L_CTX
T_LAT = B_LAT * L_LAT
T = T_CTX + T_LAT
GRID_W = 64
N_HEADS, N_KV, HD = 8, 2, 128
Q_PER_KV = N_HEADS // N_KV
D_ATTN = N_HEADS * HD
D_KV = N_KV * HD
WINDOW = 128
ROPE_BASE = 10000.0
D_RNN = D // 2
RNN_BLOCKS = 8
RNN_W = D_RNN // RNN_BLOCKS
RG_LRU_C = 8.0
HY_EMB = 33
HY_W = 64
HY_MIN_DECAY = math.log(1e-2) / 1.5
HY_MAX_DECAY = math.log(1e-2) / 0.3
D_FF = 5632
N_EXP = 8
D_FFE = 7168
EPS = 1e-6
NEG_INF = -1e30

LANES = 128
SUBLANES = 8
VMEM_LIMIT = 52 * 1024 * 1024
TM = 512
N_COND = 8
MOE_TM = 512
R_MOE = 2 * T + N_EXP * MOE_TM
HY_TD = 512
HY_FC = 1024


def _cparams(sem):
    return pltpu.CompilerParams(dimension_semantics=sem, vmem_limit_bytes=VMEM_LIMIT)


def _cond_of_block(i, tm):
    nb_ctx = T_CTX // tm
    return jnp.where(i < nb_ctx, 0, 1 + (i - nb_ctx) // (L_LAT // tm))


def _rows_value(refs, i, nbc):
    if len(refs) == 1:
        return refs[0][...]
    return jnp.where(i < nbc, refs[0][...], refs[1][...])


def _rows_specs(src, tm, bw, col, clamp=None):
    nbc = T_CTX // tm
    if isinstance(src, tuple):
        return ([pl.BlockSpec((tm, bw), lambda j, i, *p: (jnp.minimum(i, nbc - 1), col(j))),
                 pl.BlockSpec((tm, bw), lambda j, i, *p: (jnp.maximum(i - nbc, 0), col(j)))], list(src))
    row = (lambda i, p: i) if clamp is None else clamp
    return [pl.BlockSpec((tm, bw), lambda j, i, *p: (row(i, p), col(j)))], [src]


def _mm_body(*refs, a_counts, dual, residual, grouped, nbc):
    it = iter(refs)
    if grouped:
        chg_ref = next(it)
        next(it)
        nact_ref = next(it)
    n_a = len(a_counts)
    a_refs = [[next(it) for _ in range(c)] for c in a_counts]
    w_refs = [[next(it) for _ in range(n_a)] for _ in range(2 if dual else 1)]
    if residual:
        x_refs = [next(it) for _ in range(residual)]
        g_ref = next(it)
    o_ref = next(it)
    wb_refs = [[next(it) for _ in range(n_a)] for _ in range(2 if dual else 1)]
    i = pl.program_id(1)
    refresh = (chg_ref[i] != 0) if grouped else (i == 0)

    @pl.when(refresh)
    def _():
        for ws, wbs in zip(w_refs, wb_refs):
            for w, wb in zip(ws, wbs):
                wb[...] = w[...].astype(BF16)

    def prod(wbs):
        acc = None
        for a, wb in zip(a_refs, wbs):
            p = jnp.dot(_rows_value(a, i, nbc), wb[...], preferred_element_type=F32)
            acc = p if acc is None else acc + p
        return acc

    def compute():
        y = prod(wb_refs[0])
        if dual:
            y = y * jax.nn.sigmoid(y) * prod(wb_refs[1])
        if residual:
            y = _rows_value(x_refs, i, nbc) + g_ref[...] * y
        o_ref[...] = y.astype(o_ref.dtype)

    if grouped:
        pl.when(i < nact_ref[0])(compute)

        @pl.when(i >= nact_ref[0])
        def _():
            o_ref[...] = jnp.zeros_like(o_ref)
    else:
        compute()


def _mm(a_list, w_list, w_row_blocks, *, n_cols, col_off, tn, tm, out_dtype, name,
        dual_w=None, residual=None, group=None):
    width = lambda a: (a[0] if isinstance(a, tuple) else a).shape[1]
    grouped = group is not None
    M = a_list[0].shape[0] if grouped else T
    ks = [width(a) for a in a_list]
    dual = dual_w is not None
    assert n_cols % tn == 0 and col_off % tn == 0 and M % tm == 0
    nj, ni = n_cols // tn, M // tm
    cb = col_off // tn

    def rowblk(i, pref):
        if grouped:
            return jnp.maximum(jnp.minimum(i, pref[2][0] - 1), 0)
        return i

    in_specs, ins, a_counts = [], [], []
    for a, ka in zip(a_list, ks):
        sp, ops = _rows_specs(a, tm, ka, lambda j: 0, clamp=rowblk)
        in_specs += sp
        ins += ops
        a_counts.append(len(ops))

    def w_spec(rb, ka):
        if grouped:
            return pl.BlockSpec((None, ka, tn), lambda j, i, *p: (p[1][rowblk(i, p)], rb, cb + j))
        return pl.BlockSpec((ka, tn), lambda j, i, *p: (rb, cb + j))

    for ws in ([w_list, dual_w] if dual else [w_list]):
        ins += list(ws)
        in_specs += [w_spec(rb, ka) for rb, ka in zip(w_row_blocks, ks)]
    n_x = 0
    if residual is not None:
        x, mods, slot = residual
        sp, ops = _rows_specs(x, tm, tn, lambda j: j)
        n_x = len(ops)
        ins += ops + [mods]
        in_specs += sp + [pl.BlockSpec((None, None, 1, tn), lambda j, i, *p: (slot, _cond_of_block(i, tm), 0, j))]
    scratch = [pltpu.VMEM((ka, tn), BF16) for ka in ks] * (2 if dual else 1)
    body = functools.partial(_mm_body, a_counts=tuple(a_counts), dual=dual, residual=n_x, grouped=grouped,
                             nbc=T_CTX // tm)
    out_spec = pl.BlockSpec((tm, tn), lambda j, i, *p: (i, j))
    gs = pltpu.PrefetchScalarGridSpec(num_scalar_prefetch=3 if grouped else 0, grid=(nj, ni),
                                      in_specs=in_specs, out_specs=out_spec, scratch_shapes=scratch)
    call = pl.pallas_call(body, grid_spec=gs, out_shape=jax.ShapeDtypeStruct((M, n_cols), out_dtype),
                          compiler_params=_cparams(("arbitrary", "arbitrary")), name=name)
    if grouped:
        return call(*group, *ins)
    return call(*ins)


def _ada_body(c_ref, w_ref, b_ref, o_ref):
    c = c_ref[...]
    s = (c * jax.nn.sigmoid(c)).astype(BF16)
    o_ref[...] = jnp.dot(s, w_ref[...].astype(BF16), preferred_element_type=F32) + b_ref[...]


def _ada_params(cond, w_mod, b_mod):
    depth = w_mod.shape[0]
    tn = 1024
    return pl.pallas_call(
        _ada_body, grid=(depth, 6 * D // tn),
        in_specs=[pl.BlockSpec((N_COND, D), lambda l, j: (0, 0)),
                  pl.BlockSpec((None, D, tn), lambda l, j: (l, 0, j)),
                  pl.BlockSpec((None, 1, tn), lambda l, j: (l, 0, j))],
        out_specs=pl.BlockSpec((None, N_COND, tn), lambda l, j: (l, 0, j)),
        out_shape=jax.ShapeDtypeStruct((depth, N_COND, 6 * D), F32),
        compiler_params=_cparams(("arbitrary", "arbitrary")), name="ada_params",
    )(cond, w_mod, b_mod.reshape(depth, 1, 6 * D))


def _norm_body(*refs, n_x, modulate, router):
    it = iter(refs)
    x_refs = [next(it) for _ in range(n_x)]
    g_ref = next(it)
    if modulate:
        sh_ref, sc_ref = next(it), next(it)
    if router:
        wr_ref, br_ref = next(it), next(it)
    o_ref = next(it)
    pid = pl.program_id(1)
    x = _rows_value(x_refs, pid, T_CTX // TM)
    y = x * lax.rsqrt(jnp.mean(x * x, axis=-1, keepdims=True) + EPS) * g_ref[...]
    if modulate:
        y = y * (1.0 + sc_ref[...]) + sh_ref[...]
    o_ref[...] = y.astype(o_ref.dtype)
    if router:
        idx_ref, g0_ref, g1_ref, cnt_ref, carry_ref = next(it), next(it), next(it), next(it), next(it)
        logits = jnp.dot(y, wr_ref[...], preferred_element_type=F32, precision=lax.Precision.HIGHEST) + br_ref[...]
        lane = lax.broadcasted_iota(I32, logits.shape, 1)
        logits = jnp.where(lane < N_EXP, logits, -jnp.inf)
        lanef = lane.astype(F32)
        m1 = jnp.max(logits, axis=-1, keepdims=True)
        i1 = jnp.min(jnp.where(logits == m1, lanef, float(LANES)), axis=-1, keepdims=True)
        rest = jnp.where(lanef == i1, -jnp.inf, logits)
        m2 = jnp.max(rest, axis=-1, keepdims=True)
        i2 = jnp.min(jnp.where(rest == m2, lanef, float(LANES)), axis=-1, keepdims=True)
        e21 = jnp.exp(m2 - m1)
        gate1 = 1.0 / (1.0 + e21)
        g0_ref[...] = jnp.broadcast_to(gate1, logits.shape)
        g1_ref[...] = jnp.broadcast_to(e21 * gate1, logits.shape)

        @pl.when(pid == 0)
        def _():
            carry_ref[...] = jnp.zeros_like(carry_ref)

        hot1 = jnp.where(lanef == i1, 1.0, 0.0)
        hot2 = jnp.where(lanef == i2, 1.0, 0.0)
        n = logits.shape[0]
        tri = jnp.where(lax.broadcasted_iota(I32, (n, n), 1) < lax.broadcasted_iota(I32, (n, n), 0), 1.0, 0.0)
        tri = tri.astype(BF16)
        before1 = jnp.dot(tri, hot1.astype(BF16), preferred_element_type=F32)
        before2 = jnp.dot(tri, hot2.astype(BF16), preferred_element_type=F32)
        tot1 = jnp.sum(hot1, axis=0, keepdims=True)
        tot2 = jnp.sum(hot2, axis=0, keepdims=True)
        carry = carry_ref[...]
        rank1 = jnp.sum(hot1 * (before1 + carry), axis=-1, keepdims=True)
        rank2 = jnp.sum(hot2 * (before2 + carry + tot1), axis=-1, keepdims=True)
        carry = carry + tot1 + tot2
        carry_ref[...] = carry
        cnt_ref[...] = jnp.broadcast_to(carry, cnt_ref.shape).astype(I32)
        packed = jnp.where(lane == 0, i1, jnp.where(lane == 1, i2, jnp.where(lane == 2, rank1,
                                                                             jnp.where(lane == 3, rank2, 0.0))))
        idx_ref[...] = packed.astype(I32)


def _norm(x, g, *, mods=None, slots=None, router=None, out_dtype, name, row0=0, rows=None):
    pair = isinstance(x, tuple)
    rows = (T if pair else x.shape[0]) if rows is None else rows
    rb0 = row0 // TM
    modulate = mods is not None
    in_specs, ins = _rows_specs(x, TM, D, lambda j: 0, clamp=lambda i, p: rb0 + i)
    n_x = len(ins)
    ins.append(g.reshape(1, D))
    in_specs.append(pl.BlockSpec((1, D), lambda j, i: (0, 0)))
    if modulate:
        for slot in slots:
            ins.append(mods)
            in_specs.append(pl.BlockSpec((None, None, 1, D),
                                         lambda j, i, slot=slot: (slot, _cond_of_block(i, TM), 0, 0)))
    out_shape = [jax.ShapeDtypeStruct((rows, D), out_dtype)]
    out_specs = [pl.BlockSpec((TM, D), lambda j, i: (i, 0))]
    scratch = []
    if router is not None:
        w_r, b_r = router
        ins += [jnp.pad(w_r, ((0, 0), (0, LANES - N_EXP))), jnp.pad(b_r, (0, LANES - N_EXP)).reshape(1, LANES)]
        in_specs += [pl.BlockSpec((D, LANES), lambda j, i: (0, 0)), pl.BlockSpec((1, LANES), lambda j, i: (0, 0))]
        out_shape += [jax.ShapeDtypeStruct((rows, LANES), I32), jax.ShapeDtypeStruct((rows, LANES), F32),
                      jax.ShapeDtypeStruct((rows, LANES), F32), jax.ShapeDtypeStruct((SUBLANES, LANES), I32)]
        out_specs += [pl.BlockSpec((TM, LANES), lambda j, i: (i, 0))] * 3
        out_specs += [pl.BlockSpec((SUBLANES, LANES), lambda j, i: (0, 0))]
        scratch = [pltpu.VMEM((1, LANES), F32)]
    body = functools.partial(_norm_body, n_x=n_x, modulate=modulate, router=router is not None)
    res = pl.pallas_call(body, grid=(1, rows // TM), in_specs=in_specs, out_specs=out_specs, out_shape=out_shape,
                         scratch_shapes=scratch, compiler_params=_cparams(("arbitrary", "arbitrary")), name=name)(*ins)
    return res if router is not None else res[0]


def _swap32(x):
    up = jnp.concatenate([x[:, 32:], x[:, :32]], axis=1)
    down = jnp.concatenate([x[:, 96:], x[:, :96]], axis=1)
    lane = lax.broadcasted_iota(I32, x.shape, 1)
    return jnp.where((lane % 64) < 32, up, down)


def _qkprep_body(q_ref, kv_ref, cos_ref, sin_ref, qo_ref, ko_ref, vo_ref):
    cos, sin = cos_ref[...], sin_ref[...]
    for h in range(N_HEADS):
        x = q_ref[:, h * HD:(h + 1) * HD]
        qo_ref[:, h * HD:(h + 1) * HD] = (x * cos + _swap32(x) * sin).astype(BF16)
    for h in range(N_KV):
        x = kv_ref[:, h * HD:(h + 1) * HD]
        ko_ref[:, h * HD:(h + 1) * HD] = (x * cos + _swap32(x) * sin).astype(BF16)
    vo_ref[...] = kv_ref[:, D_KV:].astype(BF16)


def _qk_prep(q, kv, cos_t, sin_t):
    return pl.pallas_call(
        _qkprep_body, grid=(T // TM,),
        in_specs=[pl.BlockSpec((TM, D_ATTN), lambda i: (i, 0)), pl.BlockSpec((TM, 2 * D_KV), lambda i: (i, 0)),
                  pl.BlockSpec((TM, HD), lambda i: (i, 0)), pl.BlockSpec((TM, HD), lambda i: (i, 0))],
        out_specs=[pl.BlockSpec((TM, D_ATTN), lambda i: (i, 0)), pl.BlockSpec((TM, D_KV), lambda i: (i, 0)),
                   pl.BlockSpec((TM, D_KV), lambda i: (i, 0))],
        out_shape=[jax.ShapeDtypeStruct((T, D_ATTN), BF16), jax.ShapeDtypeStruct((T, D_KV), BF16),
                   jax.ShapeDtypeStruct((T, D_KV), BF16)],
        compiler_params=_cparams(("arbitrary",)), name="qk_prep")(q, kv, cos_t, sin_t)


def _attn_core(q_ref, o_ref, sink_ref, kvh, kall, vall, mask):
    scale = HD ** -0.5
    for g in range(Q_PER_KV):
        qh = q_ref[:, g * HD:(g + 1) * HD]
        s = lax.dot_general(qh, kall, (((1,), (1,)), ((), ())), preferred_element_type=F32) * scale
        if mask is not None:
            s = jnp.where(mask, s, NEG_INF)
        sk = sink_ref[kvh * Q_PER_KV + g]
        m = jnp.maximum(jnp.max(s, axis=-1, keepdims=True), sk)
        p = jnp.exp(s - m)
        denom = jnp.sum(p, axis=-1, keepdims=True) + jnp.exp(sk - m)
        o = jnp.dot(p.astype(BF16), vall, preferred_element_type=F32) / denom
        o_ref[:, g * HD:(g + 1) * HD] = o.astype(o_ref.dtype)


def _attn_ctx_body(sink_ref, q_ref, k_ref, v_ref, o_ref):
    _attn_core(q_ref, o_ref, sink_ref, pl.program_id(1), k_ref[...], v_ref[...], None)


def _attn_lat_body(sink_ref, q_ref, kp_ref, kc_ref, kn_ref, vp_ref, vc_ref, vn_ref, ck_ref, cv_ref, o_ref):
    qb = pl.program_id(1)
    kall = jnp.concatenate([kp_ref[...], kc_ref[...], kn_ref[...], ck_ref[...].astype(BF16)], axis=0)
    vall = jnp.concatenate([vp_ref[...], vc_ref[...], vn_ref[...], cv_ref[...].astype(BF16)], axis=0)
    nk = kall.shape[0]
    qpos = qb * WINDOW + lax.broadcasted_iota(I32, (WINDOW, nk), 0)
    col = lax.broadcasted_iota(I32, (WINDOW, nk), 1)
    kpos = (qb - 1) * WINDOW + col
    in_win = (jnp.abs(kpos - qpos) <= WINDOW) & (kpos >= 0) & (kpos < L_LAT)
    mask = in_win | (col >= 3 * WINDOW)
    _attn_core(q_ref, o_ref, sink_ref, pl.program_id(2), kall, vall, mask)


def _attention(qb, kb, vb, cache_k, cache_v, sink):
    qw = Q_PER_KV * HD
    smem = pl.BlockSpec(memory_space=pltpu.SMEM)
    o_ctx = pl.pallas_call(
        _attn_ctx_body, grid=(B_CTX, N_KV),
        in_specs=[smem, pl.BlockSpec((L_CTX, qw), lambda b, h: (b, h)),
                  pl.BlockSpec((L_CTX, HD), lambda b, h: (b, h)), pl.BlockSpec((L_CTX, HD), lambda b, h: (b, h))],
        out_specs=pl.BlockSpec((L_CTX, qw), lambda b, h: (b, h)),
        out_shape=jax.ShapeDtypeStruct((T_CTX, D_ATTN), BF16),
        compiler_params=_cparams(("arbitrary", "arbitrary")), name="attn_ctx")(sink, qb, kb, vb)
    nb = L_LAT // WINDOW
    base = T_CTX // WINDOW

    def cur(b, i, h):
        return (base + b * nb + i, h)

    def prv(b, i, h):
        return (base + b * nb + jnp.maximum(i - 1, 0), h)

    def nxt(b, i, h):
        return (base + b * nb + jnp.minimum(i + 1, nb - 1), h)

    blk = lambda f: pl.BlockSpec((WINDOW, HD), f)
    cspec = pl.BlockSpec((None, cache_k.shape[1], HD), lambda b, i, h: (b, 0, h))
    o_lat = pl.pallas_call(
        _attn_lat_body, grid=(B_LAT, nb, N_KV),
        in_specs=[smem, pl.BlockSpec((WINDOW, qw), cur), blk(prv), blk(cur), blk(nxt), blk(prv), blk(cur), blk(nxt),
                  cspec, cspec],
        out_specs=pl.BlockSpec((WINDOW, qw), lambda b, i, h: (b * nb + i, h)),
        out_shape=jax.ShapeDtypeStruct((T_LAT, D_ATTN), BF16),
        compiler_params=_cparams(("arbitrary", "arbitrary", "arbitrary")), name="attn_lat",
    )(sink, qb, kb, kb, kb, vb, vb, vb, cache_k, cache_v)
    return o_ctx, o_lat


def _scan_pitch(L):
    return L // SUBLANES + 4


def _rglru_body(x_ref, y_ref, h0_ref, cw_ref, cb_ref, wa_ref, ba_ref, wx_ref, bx_ref, lam_ref,
                o_ref, fin_ref, a_s, b_s, p_s, h_s):
    L = x_ref.shape[0]
    lc = L // SUBLANES
    pitch = _scan_pitch(L)
    x = x_ref[...]
    row = lax.broadcasted_iota(I32, x.shape, 0)
    xc = cb_ref[...] + x * cw_ref[2:3, :]
    xc = xc + jnp.where(row >= 2, pltpu.roll(x, 2, 0), 0.0) * cw_ref[0:1, :]
    xc = xc + jnp.where(row >= 1, pltpu.roll(x, 1, 0), 0.0) * cw_ref[1:2, :]
    xc = xc + jnp.where(row < L - 1, pltpu.roll(x, L - 1, 0), 0.0) * cw_ref[3:4, :]
    xcb = xc.astype(BF16)
    for d in range(2):
        ga = jnp.dot(xcb, wa_ref[d].astype(BF16), preferred_element_type=F32) + ba_ref[d:d + 1, :]
        gx = jnp.dot(xcb, wx_ref[d].astype(BF16), preferred_element_type=F32) + bx_ref[d:d + 1, :]
        log_a = -RG_LRU_C * jax.nn.sigmoid(ga) * jax.nn.softplus(-lam_ref[d:d + 1, :])
        a = jnp.exp(log_a)
        b = jnp.sqrt(-jnp.tanh(log_a) * (1.0 + a * a)) * jax.nn.sigmoid(gx) * xc
        for s in range(SUBLANES):
            a_s[d, s * pitch:s * pitch + lc, :] = a[s * lc:(s + 1) * lc]
            b_s[d, s * pitch:s * pitch + lc, :] = b[s * lc:(s + 1) * lc]

    def step(i, carry):
        out = []
        for d, t in ((0, i), (1, lc - 1 - i)):
            rows = pl.ds(t, SUBLANES, stride=pitch)
            a = a_s[d, rows, :]
            h = a * carry[2 * d] + b_s[d, rows, :]
            p = a * carry[2 * d + 1]
            h_s[d, rows, :] = h
            p_s[d, rows, :] = p
            out += [h, p]
        return tuple(out)

    zero = jnp.zeros((SUBLANES, RNN_W), F32)
    one = jnp.ones((SUBLANES, RNN_W), F32)
    hf, pf, hb, pb = lax.fori_loop(0, lc, step, (zero, one, zero, one), unroll=4)

    r8 = lax.broadcasted_iota(I32, (SUBLANES, RNN_W), 0)

    def chunk_carry(p, h, h0, reverse):
        for k in (1, 2, 4):
            sh = SUBLANES - k if reverse else k
            m = (r8 < SUBLANES - k) if reverse else (r8 >= k)
            h = jnp.where(m, p * pltpu.roll(h, sh, 0) + h, h)
            p = jnp.where(m, p * pltpu.roll(p, sh, 0), p)
        h0 = jnp.broadcast_to(h0, (SUBLANES, RNN_W))
        state = p * h0 + h
        if reverse:
            return state, jnp.where(r8 < SUBLANES - 1, pltpu.roll(state, SUBLANES - 1, 0), h0)
        return state, jnp.where(r8 >= 1, pltpu.roll(state, 1, 0), h0)

    sf, cf = chunk_carry(pf, hf, h0_ref[0:1, :], False)
    sb, cb = chunk_carry(pb, hb, h0_ref[1:2, :], True)

    for s in range(SUBLANES):
        rows = slice(s * pitch, s * pitch + lc)
        hsum = (h_s[0, rows, :] + p_s[0, rows, :] * cf[s:s + 1, :]) + (h_s[1, rows, :] + p_s[1, rows, :] * cb[s:s + 1, :])
        o_ref[s * lc:(s + 1) * lc, :] = (hsum * jax.nn.gelu(y_ref[s * lc:(s + 1) * lc, :])).astype(o_ref.dtype)
    fin_ref[0:1, :] = sf[SUBLANES - 1:SUBLANES, :]
    fin_ref[1:2, :] = sb[0:1, :]


def _rglru(xy, h0, n_seq, L, row0, conv_w, conv_b, w_a, b_a, w_x, b_x, lam, name):
    rb0 = row0 // L
    nb = RNN_BLOCKS
    vec = lambda r: pl.BlockSpec((r, RNN_W), lambda b, n: (0, n))
    wsp = pl.BlockSpec((2, None, RNN_W, RNN_W), lambda b, n: (0, n, 0, 0))
    return pl.pallas_call(
        _rglru_body, grid=(n_seq, nb),
        in_specs=[pl.BlockSpec((L, RNN_W), lambda b, n: (rb0 + b, n)),
                  pl.BlockSpec((L, RNN_W), lambda b, n: (rb0 + b, nb + n)),
                  pl.BlockSpec((None, 2, RNN_W), lambda b, n: (b, 0, n)),
                  vec(4), vec(1), wsp, vec(2), wsp, vec(2), vec(2)],
        out_specs=[pl.BlockSpec((L, RNN_W), lambda b, n: (b, n)),
                   pl.BlockSpec((None, 2, RNN_W), lambda b, n: (b, 0, n))],
        out_shape=[jax.ShapeDtypeStruct((n_seq * L, D_RNN), BF16), jax.ShapeDtypeStruct((n_seq, 2, D_RNN), F32)],
        scratch_shapes=[pltpu.VMEM((2, SUBLANES * _scan_pitch(L), RNN_W), F32)] * 4,
        compiler_params=_cparams(("arbitrary", "arbitrary")), name=name,
    )(xy, xy, h0, conv_w, conv_b.reshape(1, D_RNN), w_a, b_a, w_x, b_x, lam)


def _hy_filter_body(fv_ref, w1_ref, b1_ref, w2_ref, b2_ref, fr_ref, w3_ref, dl_ref, o_ref, hid_s, tt_s, *, L):
    hp = lax.Precision.HIGHEST
    rowi = lax.broadcasted_iota(I32, (L, LANES), 0)
    lane = lax.broadcasted_iota(I32, (L, LANES), 1)

    def features(pos):
        posf = pos.astype(F32)
        tt = posf / (L - 1)
        ang = fv_ref[...] * (2.0 * math.pi * posf / L)
        z = jnp.where(lane == 0, tt, jnp.where(lane <= 16, jnp.cos(ang), jnp.where(lane <= 32, -jnp.sin(ang), 0.0)))
        return z, tt[:, 0:1]

    is_bwd = pl.program_id(0) == 1

    @pl.when(pl.program_id(1) == 0)
    def _():
        z, tt = features(jnp.where(is_bwd, L - rowi, rowi))
        h = jnp.sin(fr_ref[0:1, :] * (jnp.dot(z, w1_ref[...], preferred_element_type=F32, precision=hp) + b1_ref[...]))
        h = jnp.sin(fr_ref[1:2, :] * (jnp.dot(h, w2_ref[...], preferred_element_type=F32, precision=hp) + b2_ref[...]))
        hid_s[...] = h
        tt_s[...] = jnp.broadcast_to(tt, tt_s.shape)

    filt = jnp.dot(hid_s[...], w3_ref[...], preferred_element_type=F32, precision=hp)
    filt = filt * jnp.exp(-tt_s[:, 0:1] * dl_ref[...])
    dead = is_bwd & (lax.broadcasted_iota(I32, filt.shape, 0) == 0)
    o_ref[...] = jnp.where(dead, 0.0, filt).astype(o_ref.dtype)


def _hy_filter(L, f_w1, f_b1, f_w2, f_b2, f_freq, f_w3):
    bands = (HY_EMB - 1) // 2
    f = jnp.linspace(1e-4, bands - 1, bands, dtype=F32)
    fv = jnp.zeros((LANES,), F32).at[1:1 + bands].set(f).at[1 + bands:1 + 2 * bands].set(f).reshape(1, LANES)
    padw = lambda w, r, c: jnp.pad(w.astype(F32), ((0, r - w.shape[0]), (0, c - w.shape[1])))
    padv = lambda v: jnp.pad(v.astype(F32), (0, LANES - v.shape[0])).reshape(1, LANES)
    w1, w2 = padw(f_w1, LANES, LANES), padw(f_w2, LANES, LANES)
    w3 = padw(f_w3, LANES, f_w3.shape[1])
    fr = jnp.pad(f_freq.astype(F32), ((0, 0), (0, LANES - HY_W)))
    deltas = jnp.abs(jnp.linspace(HY_MIN_DECAY, HY_MAX_DECAY, D, dtype=F32)).reshape(1, D)
    tn = 1024
    per = D // tn
    full = lambda r: pl.BlockSpec((r, LANES), lambda d, j: (0, 0))
    col = lambda d, j: ((j // per) * 2 + d) * per + j % per
    return pl.pallas_call(
        functools.partial(_hy_filter_body, L=L), grid=(2, 2 * per),
        in_specs=[full(1), full(LANES), full(1), full(LANES), full(1), full(2),
                  pl.BlockSpec((LANES, tn), lambda d, j: (0, col(d, j))),
                  pl.BlockSpec((1, tn), lambda d, j: (0, j % per))],
        out_specs=pl.BlockSpec((L, tn), lambda d, j: (0, col(d, j))),
        out_shape=jax.ShapeDtypeStruct((L, 4 * D), BF16),
        scratch_shapes=[pltpu.VMEM((L, LANES), F32), pltpu.VMEM((L, LANES), F32)],
        compiler_params=_cparams(("arbitrary", "arbitrary")), name=f"hy_filter_{L}",
    )(fv, w1, padv(f_b1), w2, padv(f_b2), fr, w3, deltas)


def _dft_mats(L, fc):
    n = 2 * L
    h = fc // 2
    nf = L // h
    k = jnp.arange(L, dtype=I32)[:, None]
    t = jnp.arange(L, dtype=I32)[None, :]
    ang = ((k * t) % n).astype(F32) * (2.0 * math.pi / n)
    cos, sin = jnp.cos(ang), jnp.sin(ang)
    nyq = jnp.where(t % 2 == 0, 1.0, -1.0).astype(F32)
    is0 = k == 0
    re_f = cos
    im_f = jnp.where(is0, nyq, -sin)
    re_i = jnp.where(is0, 1.0, 2.0) * cos / n
    im_i = jnp.where(is0, nyq, -2.0 * sin) / n
    cf = jnp.concatenate([re_f.reshape(nf, h, L), im_f.reshape(nf, h, L)], axis=1)
    ci = jnp.concatenate([re_i.reshape(nf, h, L), im_i.reshape(nf, h, L)], axis=1)
    return cf.astype(BF16), jnp.swapaxes(ci, 1, 2).astype(BF16)


def _spec_body(cf_ref, c1_ref, c2_ref, o_ref):
    f = pl.program_id(1)
    fc = cf_ref.shape[0]
    h = fc // 2
    z1 = jnp.dot(cf_ref[...], c1_ref[...], preferred_element_type=F32)
    z2 = jnp.dot(cf_ref[...], c2_ref[...], preferred_element_type=F32)
    r = lax.broadcasted_iota(I32, z1.shape, 0)
    kk = f * h + jnp.where(r < h, r, r - h)
    nyq = (r == h) & (f == 0)
    odd = (kk % 2 == 1) & jnp.logical_not(nyq)
    o_ref[...] = z1 + jnp.where(odd, -z2, z2)


def _hy_spectra(filt, cf):
    nf, fc, L = cf.shape
    td = HY_TD
    per = D // td
    return pl.pallas_call(
        _spec_body, grid=(2 * per, nf),
        in_specs=[pl.BlockSpec((None, fc, L), lambda c, f: (f, 0, 0)),
                  pl.BlockSpec((L, td), lambda c, f: (0, (c // per) * 2 * per + c % per)),
                  pl.BlockSpec((L, td), lambda c, f: (0, (c // per) * 2 * per + per + c % per))],
        out_specs=pl.BlockSpec((None, fc, td), lambda c, f: (f, 0, c)),
        out_shape=jax.ShapeDtypeStruct((nf, fc, 2 * D), F32),
        compiler_params=_cparams(("arbitrary", "arbitrary")), name=f"hy_spectra_{L}")(cf, filt, filt)


HY_TC = 256


def _hy_conv(u_ref, cw_ref, cb_ref, L):
    u = u_ref[...]
    rows = u.shape[0]
    pos = lax.broadcasted_iota(I32, u.shape, 0) % L
    uc = cb_ref[...] + u * cw_ref[1:2, :]
    uc = uc + jnp.where(pos >= 1, pltpu.roll(u, 1, 0), 0.0) * cw_ref[0:1, :]
    return uc + jnp.where(pos < L - 1, pltpu.roll(u, rows - 1, 0), 0.0) * cw_ref[2:3, :]


def _hy_third_specs(third, row0, L):
    tr = max(L, TM)
    per = D // HY_TC
    col = lambda i, c: third * per + c
    return [pl.BlockSpec((tr, HY_TC), lambda i, c: (row0 // tr + i, col(i, c))),
            pl.BlockSpec((3, HY_TC), lambda i, c: (0, col(i, c))),
            pl.BlockSpec((1, HY_TC), lambda i, c: (0, col(i, c)))]


def _hy_prep_body(u_ref, cw_ref, cb_ref, zf_ref, zb_ref, *, L):
    v = _hy_conv(u_ref, cw_ref, cb_ref, L)
    zf_ref[...] = v
    zb_ref[...] = v.astype(BF16)


def _hy_prep(u, conv_w, conv_b, row0, rows, L):
    tr = max(L, TM)
    ospec = pl.BlockSpec((tr, HY_TC), lambda i, c: (i, c))
    return pl.pallas_call(
        functools.partial(_hy_prep_body, L=L), grid=(rows // tr, D // HY_TC),
        in_specs=_hy_third_specs(0, row0, L), out_specs=[ospec, ospec],
        out_shape=[jax.ShapeDtypeStruct((rows, D), F32), jax.ShapeDtypeStruct((rows, D), BF16)],
        compiler_params=_cparams(("arbitrary", "arbitrary")), name=f"hy_prep_{L}",
    )(u, conv_w, conv_b.reshape(1, 3 * D))


def _longconv_body(z_ref, cf_ref, ci_ref, s_ref, o_ref, acc_ref):
    f = pl.program_id(2)
    fc = cf_ref.shape[0]
    h = fc // 2
    zf = jnp.dot(cf_ref[...], z_ref[...], preferred_element_type=F32)
    zre, zim = zf[:h], zf[h:]
    sre, sim = s_ref[:h, :], s_ref[h:, :]
    first = (lax.broadcasted_iota(I32, zre.shape, 0) == 0) & (f == 0)
    yre = zre * sre - jnp.where(first, 0.0, zim * sim)
    yim = jnp.where(first, zim * sim, zre * sim + zim * sre)
    y = jnp.concatenate([yre, yim], axis=0).astype(BF16)
    contrib = jnp.dot(ci_ref[...], y, preferred_element_type=F32)

    @pl.when(f == 0)
    def _():
        acc_ref[...] = contrib

    @pl.when(f > 0)
    def _():
        acc_ref[...] += contrib

    @pl.when(f == pl.num_programs(2) - 1)
    def _():
        o_ref[...] = acc_ref[...]


def _longconv(z, n_seq, L, cf, ci, spec, order):
    nf, fc, _ = cf.shape
    td = HY_TD if L > TM else D
    per = D // td
    return pl.pallas_call(
        _longconv_body, grid=(n_seq, per, nf),
        in_specs=[pl.BlockSpec((L, td), lambda b, c, f: (b, c)),
                  pl.BlockSpec((None, fc, L), lambda b, c, f: (f, 0, 0)),
                  pl.BlockSpec((None, L, fc), lambda b, c, f: (f, 0, 0)),
                  pl.BlockSpec((None, fc, td), lambda b, c, f: (f, 0, order * per + c))],
        out_specs=pl.BlockSpec((L, td), lambda b, c, f: (b, c)),
        out_shape=jax.ShapeDtypeStruct((n_seq * L, D), F32),
        scratch_shapes=[pltpu.VMEM((L, td), F32)],
        compiler_params=_cparams(("arbitrary", "arbitrary", "arbitrary")), name=f"longconv_{L}_{order}",
    )(z, cf, ci, spec)


def _hy_gate_body(zc_ref, z_ref, b_ref, u_ref, cw_ref, cb_ref, *out_refs, L):
    z = _hy_conv(u_ref, cw_ref, cb_ref, L) * (zc_ref[...] + z_ref[...] * b_ref[...])
    for o_ref in out_refs:
        o_ref[...] = z.astype(o_ref.dtype)


def _hy_gate(zc, z, bias, u, conv_w, conv_b, order, row0, L, keep_f32):
    rows = zc.shape[0]
    tr = max(L, TM)
    blk = pl.BlockSpec((tr, HY_TC), lambda i, c: (i, c))
    dts = ([F32] if keep_f32 else []) + [BF16]
    return pl.pallas_call(
        functools.partial(_hy_gate_body, L=L), grid=(rows // tr, D // HY_TC),
        in_specs=[blk, blk, pl.BlockSpec((1, HY_TC), lambda i, c: (0, c))] + _hy_third_specs(order + 1, row0, L),
        out_specs=[blk] * len(dts),
        out_shape=[jax.ShapeDtypeStruct((rows, D), dt) for dt in dts],
        compiler_params=_cparams(("arbitrary", "arbitrary")), name=f"hy_gate_{L}_{order}",
    )(zc, z, bias.reshape(1, D), u, conv_w, conv_b.reshape(1, 3 * D))


DISPATCH_ROWS = 256
COMBINE_ROWS = 128
DMA_UNROLL = 8


def _dispatch_body(d0_ref, d1_ref, zs_ref, h_ref, o_ref, zbuf, sem, zsem):
    base = pl.program_id(0) * DISPATCH_ROWS

    @pl.when(pl.program_id(0) == 0)
    def _():
        zbuf[...] = jnp.zeros_like(zbuf)
        for k in range(2 * N_EXP):
            fill = pltpu.make_async_copy(zbuf, o_ref.at[pl.ds(zs_ref[k], MOE_TM)], zsem)
            fill.start()
            fill.wait()

    def issue(q, c):
        for u in range(DMA_UNROLL):
            r = q * DMA_UNROLL + u
            pltpu.make_async_copy(h_ref.at[r], o_ref.at[d0_ref[base + r]], sem).start(priority=0)
            pltpu.make_async_copy(h_ref.at[r], o_ref.at[d1_ref[base + r]], sem).start(priority=1)
        return c

    def drain(q, c):
        for _ in range(2 * DMA_UNROLL):
            pltpu.make_async_copy(h_ref.at[0], o_ref.at[0], sem).wait()
        return c

    lax.fori_loop(0, DISPATCH_ROWS // DMA_UNROLL, issue, 0)
    lax.fori_loop(0, DISPATCH_ROWS // DMA_UNROLL, drain, 0)


def _moe_dispatch(h, dest0, dest1, zero_starts):
    sub = D // LANES
    gs = pltpu.PrefetchScalarGridSpec(
        num_scalar_prefetch=3, grid=(T // DISPATCH_ROWS,),
        in_specs=[pl.BlockSpec((DISPATCH_ROWS, sub, LANES), lambda i, *p: (i, 0, 0))],
        out_specs=pl.BlockSpec(memory_space=pl.ANY),
        scratch_shapes=[pltpu.VMEM((MOE_TM, sub, LANES), BF16), pltpu.SemaphoreType.DMA(()),
                        pltpu.SemaphoreType.DMA(())])
    out = pl.pallas_call(_dispatch_body, grid_spec=gs, out_shape=jax.ShapeDtypeStruct((R_MOE, sub, LANES), BF16),
                         compiler_params=_cparams(("arbitrary",)), name="moe_dispatch",
                         )(dest0, dest1, zero_starts, h.reshape(T, sub, LANES))
    return out.reshape(R_MOE, D)


def _combine_body(d0_ref, d1_ref, y_ref, x_ref, g0_ref, g1_ref, gate_ref, fg_ref, oc_ref, ol_ref, buf, sems):
    i = pl.program_id(0)

    def issue(step, slot):
        base = step * COMBINE_ROWS

        def body(q, c):
            for u in range(DMA_UNROLL):
                r = q * DMA_UNROLL + u
                pltpu.make_async_copy(y_ref.at[pl.ds(d0_ref[base + r], 1), :], buf.at[slot, 0, pl.ds(r, 1), :],
                                      sems.at[slot]).start(priority=0)
                pltpu.make_async_copy(y_ref.at[pl.ds(d1_ref[base + r], 1), :], buf.at[slot, 1, pl.ds(r, 1), :],
                                      sems.at[slot]).start(priority=1)
            return c

        lax.fori_loop(0, COMBINE_ROWS // DMA_UNROLL, body, 0)

    slot = i % 2

    @pl.when(i == 0)
    def _():
        issue(0, 0)

    @pl.when(i + 1 < pl.num_programs(0))
    def _():
        issue(i + 1, 1 - slot)

    def drain(q, c):
        for _ in range(2 * DMA_UNROLL):
            pltpu.make_async_copy(y_ref.at[pl.ds(0, 1), :], buf.at[slot, 0, pl.ds(0, 1), :], sems.at[slot]).wait()
        return c

    lax.fori_loop(0, COMBINE_ROWS // DMA_UNROLL, drain, 0)
    y = g0_ref[:, 0:1] * buf[slot, 0] + g1_ref[:, 0:1] * buf[slot, 1]
    x = x_ref[...] + gate_ref[...] * y
    out = x * lax.rsqrt(jnp.mean(x * x, axis=-1, keepdims=True) + EPS) * fg_ref[...]
    nbc = T_CTX // COMBINE_ROWS

    @pl.when(i < nbc)
    def _():
        oc_ref[...] = out

    @pl.when(i >= nbc)
    def _():
        ol_ref[...] = out


def _moe_combine_norm(ybuf, dest0, dest1, g0, g1, x, mods, slot, final_g):
    nbc = T_CTX // COMBINE_ROWS
    tok = pl.BlockSpec((COMBINE_ROWS, D), lambda i, *p: (i, 0))
    gsp = pl.BlockSpec((COMBINE_ROWS, LANES), lambda i, *p: (i, 0))
    gs = pltpu.PrefetchScalarGridSpec(
        num_scalar_prefetch=2, grid=(T // COMBINE_ROWS,),
        in_specs=[pl.BlockSpec(memory_space=pl.ANY), tok, gsp, gsp,
                  pl.BlockSpec((None, None, 1, D), lambda i, *p: (slot, _cond_of_block(i, COMBINE_ROWS), 0, 0)),
                  pl.BlockSpec((1, D), lambda i, *p: (0, 0))],
        out_specs=[pl.BlockSpec((COMBINE_ROWS, D), lambda i, *p: (jnp.minimum(i, nbc - 1), 0)),
                   pl.BlockSpec((COMBINE_ROWS, D), lambda i, *p: (jnp.maximum(i - nbc, 0), 0))],
        scratch_shapes=[pltpu.VMEM((2, 2, COMBINE_ROWS, D), F32), pltpu.SemaphoreType.DMA((2,))])
    return pl.pallas_call(_combine_body, grid_spec=gs,
                          out_shape=[jax.ShapeDtypeStruct((T_CTX, D), F32), jax.ShapeDtypeStruct((T_LAT, D), F32)],
                          compiler_params=_cparams(("arbitrary",)), name="moe_combine_norm",
                          )(dest0, dest1, ybuf, x, g0, g1, mods, final_g.reshape(1, D))


def _moe_plan(idx, cnt):
    counts = cnt[0, :N_EXP]
    padded = (counts + MOE_TM - 1) // MOE_TM * MOE_TM
    p_ends = jnp.cumsum(padded)
    p_starts = p_ends - padded
    experts = jnp.arange(N_EXP, dtype=I32)[None, :]

    def dest(e, rank):
        return jnp.sum(jnp.where(e[:, None] == experts, p_starts[None, :], 0), axis=1) + rank

    def groups(tm):
        nblk = R_MOE // tm
        blk_start = jnp.arange(nblk, dtype=I32) * tm
        gid = jnp.minimum(jnp.sum((blk_start[:, None] >= p_ends[None, :]).astype(I32), axis=1), N_EXP - 1)
        nact = (p_ends[-1] // tm).astype(I32).reshape(1)
        gid = jnp.where(jnp.arange(nblk) < nact[0], gid, gid[jnp.maximum(nact[0] - 1, 0)])
        chg = jnp.concatenate([jnp.ones((1,), I32), (gid[1:] != gid[:-1]).astype(I32)])
        return chg, gid, nact

    tail = jnp.minimum(p_ends[-1] + jnp.arange(N_EXP, dtype=I32) * MOE_TM, R_MOE - MOE_TM)
    zero_starts = jnp.concatenate([jnp.maximum(p_ends - MOE_TM, 0), tail]).astype(I32)
    return dest(idx[:, 0], idx[:, 2]), dest(idx[:, 1], idx[:, 3]), groups, zero_starts


def _rope_tables():
    quarter = HD // 4
    inv_freq = ROPE_BASE ** (-jnp.arange(quarter, dtype=F32) / quarter)
    t = jnp.arange(L_LAT)
    row = (t // GRID_W).astype(F32)[:, None] * inv_freq
    col = (t % GRID_W).astype(F32)[:, None] * inv_freq
    ang = jnp.concatenate([row, row, col, col], axis=1)
    sign = jnp.tile(jnp.concatenate([-jnp.ones((quarter,), F32), jnp.ones((quarter,), F32)]), 2)
    cos = jnp.concatenate([jnp.ones((T_CTX, HD), F32), jnp.tile(jnp.cos(ang), (B_LAT, 1))], axis=0)
    sin = jnp.concatenate([jnp.zeros((T_CTX, HD), F32), jnp.tile(jnp.sin(ang) * sign, (B_LAT, 1))], axis=0)
    return cos, sin


def kernel(x_prompt, x_sample, cache_k, cache_v, state_rglru, c, c_ctx, w_mod, b_mod, norm_g, final_g, a_w_in, a_w_out, rnn_conv_w, rnn_conv_b, rnn_w_a, rnn_b_a, rnn_w_x, rnn_b_x, rnn_lam, attn_sink, ffn_w1, ffn_w3, ffn_w2, h_w_in, h_w_out, h_conv_w, h_conv_b, hf_w1, hf_b1, hf_w2, hf_b2, hf_freq, hf_w3, h_bias, moe_router, moe_router_b, moe_w_gate, moe_w_up, moe_w_down):
    x = (x_prompt.reshape(T_CTX, D), x_sample.reshape(T_LAT, D))
    cond = jnp.concatenate([c_ctx[None, :], c, jnp.zeros((N_COND - 1 - B_LAT, D), F32)], axis=0)
    mods_all = _ada_params(cond, w_mod, b_mod)
    mods_all = mods_all.reshape(-1, N_COND, 6, D).transpose(0, 2, 1, 3).reshape(-1, 6, N_COND, 1, D)
    cos_t, sin_t = _rope_tables()

    mods = mods_all[0]
    h = _norm(x, norm_g[0, 0], mods=mods, slots=(0, 1), out_dtype=BF16, name="norm_mix0")
    w_in = a_w_in[0]
    mm1 = functools.partial(_mm, [h], [w_in], [0], tm=2 * TM)
    q = mm1(n_cols=D_ATTN, col_off=0, tn=1024, out_dtype=F32, name="proj_q")
    kv = mm1(n_cols=2 * D_KV, col_off=D_ATTN, tn=512, out_dtype=F32, name="proj_kv")
    xy = mm1(n_cols=2 * D_RNN, col_off=D_ATTN + 2 * D_KV, tn=512, out_dtype=F32, name="proj_rnn")
    qb, kb, vb = _qk_prep(q, kv, cos_t, sin_t)
    ck = cache_k[:, 0].reshape(B_LAT, -1, D_KV)
    cv = cache_v[:, 0].reshape(B_LAT, -1, D_KV)
    o_ctx, o_lat = _attention(qb, kb, vb, ck, cv, attn_sink[0])
    rnn_w = (rnn_conv_w[0], rnn_conv_b[0], rnn_w_a[0], rnn_b_a[0], rnn_w_x[0], rnn_b_x[0], rnn_lam[0])
    r_ctx, s_ctx = _rglru(xy, jnp.zeros((B_CTX, 2, D_RNN), F32), B_CTX, L_CTX, 0, *rnn_w, name="rglru_ctx")
    r_lat, _ = _rglru(xy, state_rglru[:, 0], B_LAT, L_LAT, T_CTX, *rnn_w, name="rglru_lat")
    w_out = a_w_out[0]
    x = _mm([(o_ctx, o_lat), (r_ctx, r_lat)], [w_out, w_out], [0, 1], n_cols=D, col_off=0, tn=1024, tm=TM,
            out_dtype=F32, residual=(x, mods, 2), name="proj_out0")
    h = _norm(x, norm_g[0, 1], mods=mods, slots=(3, 4), out_dtype=BF16, name="norm_ffn0")
    hid = _mm([h], [ffn_w1[0]], [0], dual_w=[ffn_w3[0]], n_cols=D_FF, col_off=0, tn=512, tm=2 * TM,
              out_dtype=BF16, name="ffn_up")
    x = _mm([hid], [ffn_w2[0]], [0], n_cols=D, col_off=0, tn=512, tm=TM, out_dtype=F32, residual=(x, mods, 5),
            name="ffn_down")

    mods = mods_all[1]
    h = _norm(x, norm_g[1, 0], mods=mods, slots=(0, 1), out_dtype=BF16, name="norm_mix1")
    u = _mm([h], [h_w_in[0]], [0], n_cols=3 * D, col_off=0, tn=1024, tm=2 * TM, out_dtype=F32, name="hy_in")
    zs = []
    for row0, n_seq, L in ((0, B_CTX, L_CTX), (T_CTX, B_LAT, L_LAT)):
        zf, zb = _hy_prep(u, h_conv_w[0], h_conv_b[0], row0, n_seq * L, L)
        cf, ci = _dft_mats(L, min(HY_FC, 2 * L))
        filt = _hy_filter(L, hf_w1[0], hf_b1[0], hf_w2[0], hf_b2[0], hf_freq[0], hf_w3[0])
        spec = _hy_spectra(filt, cf)
        zc = _longconv(zb, n_seq, L, cf, ci, spec, 0)
        zf, zb = _hy_gate(zc, zf, h_bias[0, 0], u, h_conv_w[0], h_conv_b[0], 0, row0, L, True)
        zc = _longconv(zb, n_seq, L, cf, ci, spec, 1)
        (zb,) = _hy_gate(zc, zf, h_bias[0, 1], u, h_conv_w[0], h_conv_b[0], 1, row0, L, False)
        zs.append(zb)
    x = _mm([tuple(zs)], [h_w_out[0]], [0], n_cols=D, col_off=0, tn=1024, tm=TM, out_dtype=F32,
            residual=(x, mods, 2), name="hy_out")
    h, idx, g0, g1, cnt = _norm(x, norm_g[1, 1], mods=mods, slots=(3, 4), router=(moe_router[0], moe_router_b[0]),
                                out_dtype=BF16, name="norm_moe")
    dest0, dest1, groups, zero_starts = _moe_plan(idx, cnt)
    xs = _moe_dispatch(h, dest0, dest1, zero_starts)
    hid = _mm([xs], [moe_w_gate[0]], [0], dual_w=[moe_w_up[0]], n_cols=D_FFE, col_off=0, tn=512, tm=MOE_TM,
              out_dtype=BF16, group=groups(MOE_TM), name="moe_up")
    ybuf = _mm([hid], [moe_w_down[0]], [0], n_cols=D, col_off=0, tn=512, tm=MOE_TM // 2, out_dtype=F32,
               group=groups(MOE_TM // 2), name="moe_down")
    y_prompt, y_sample = _moe_combine_norm(ybuf, dest0, dest1, g0, g1, x, mods, 5, final_g)
    y_prompt = y_prompt.reshape(B_CTX, L_CTX, D)
    y_sample = y_sample.reshape(B_LAT, L_LAT, D)
    new_k = kv[:T_CTX, :D_KV].reshape(B_CTX, 1, L_CTX, N_KV, HD)
    new_v = kv[:T_CTX, D_KV:].reshape(B_CTX, 1, L_CTX, N_KV, HD)
    new_s = s_ctx.reshape(B_CTX, 1, 2, D_RNN)
    return (y_prompt, y_sample, new_k, new_v, new_s)
```

```python
import functools
import math

import jax
import jax.numpy as jnp
from jax import lax
from jax.experimental import pallas as pl
from jax.experimental.pallas import tpu as pltpu

F32 = jnp.float32
BF16 = jnp.bfloat16
I32 = jnp.int32

D = 2048
B_CTX, L_CTX = 32, 256
B_LAT, L_LAT = 4, 2048
T_CTX = B_CTX * L_CTX
T_LAT = B_LAT * L_LAT
T = T_CTX + T_LAT
GRID_W = 64
N_HEADS, N_KV, HD = 8, 2, 128
Q_PER_KV = N_HEADS // N_KV
D_ATTN = N_HEADS * HD
D_KV = N_KV * HD
WINDOW = 128
ROPE_BASE = 10000.0
D_RNN = D // 2
RNN_BLOCKS = 8
RNN_W = D_RNN // RNN_BLOCKS
RG_LRU_C = 8.0
HY_EMB = 33
HY_W = 64
HY_MIN_DECAY = math.log(1e-2) / 1.5
HY_MAX_DECAY = math.log(1e-2) / 0.3
D_FF = 5632
N_EXP = 8
D_FFE = 7168
EPS = 1e-6
NEG_INF = -1e30

LANES = 128
SUBLANES = 8
VMEM_LIMIT = 52 * 1024 * 1024
TM = 512
N_COND = 8
MOE_TM = 512
R_MOE = 2 * T + N_EXP * MOE_TM
HY_TD = 512
HY_FC = 1024


def _cparams(sem):
    return pltpu.CompilerParams(dimension_semantics=sem, vmem_limit_bytes=VMEM_LIMIT)


def _cond_of_block(i, tm):
    nb_ctx = T_CTX // tm
    return jnp.where(i < nb_ctx, 0, 1 + (i - nb_ctx) // (L_LAT // tm))


def _rows_value(refs, i, nbc):
    if len(refs) == 1:
        return refs[0][...]
    return jnp.where(i < nbc, refs[0][...], refs[1][...])


def _rows_specs(src, tm, bw, col, clamp=None):
    nbc = T_CTX // tm
    if isinstance(src, tuple):
        return ([pl.BlockSpec((tm, bw), lambda j, i, *p: (jnp.minimum(i, nbc - 1), col(j))),
                 pl.BlockSpec((tm, bw), lambda j, i, *p: (jnp.maximum(i - nbc, 0), col(j)))], list(src))
    row = (lambda i, p: i) if clamp is None else clamp
    return [pl.BlockSpec((tm, bw), lambda j, i, *p: (row(i, p), col(j)))], [src]


def _mm_body(*refs, a_counts, dual, residual, nbc):
    it = iter(refs)
    n_a = len(a_counts)
    a_refs = [[next(it) for _ in range(c)] for c in a_counts]
    w_refs = [[next(it) for _ in range(n_a)] for _ in range(2 if dual else 1)]
    if residual:
        x_refs = [next(it) for _ in range(residual)]
        g_ref = next(it)
    o_ref = next(it)
    wb_refs = [[next(it) for _ in range(n_a)] for _ in range(2 if dual else 1)]
    i = pl.program_id(1)

    @pl.when(i == 0)
    def _():
        for ws, wbs in zip(w_refs, wb_refs):
            for w, wb in zip(ws, wbs):
                wb[...] = w[...].astype(BF16)

    def prod(wbs):
        acc = None
        for a, wb in zip(a_refs, wbs):
            p = jnp.dot(_rows_value(a, i, nbc), wb[...], preferred_element_type=F32)
            acc = p if acc is None else acc + p
        return acc

    y = prod(wb_refs[0])
    if dual:
        y = y * jax.nn.sigmoid(y) * prod(wb_refs[1])
    if residual:
        y = _rows_value(x_refs, i, nbc) + g_ref[...] * y
    o_ref[...] = y.astype(o_ref.dtype)


def _mm(a_list, w_list, w_row_blocks, *, n_cols, col_off, tn, tm, out_dtype, name, dual_w=None, residual=None):
    width = lambda a: (a[0] if isinstance(a, tuple) else a).shape[1]
    ks = [width(a) for a in a_list]
    dual = dual_w is not None
    assert n_cols % tn == 0 and col_off % tn == 0 and T % tm == 0
    cb = col_off // tn
    in_specs, ins, a_counts = [], [], []
    for a, ka in zip(a_list, ks):
        sp, ops = _rows_specs(a, tm, ka, lambda j: 0)
        in_specs += sp
        ins += ops
        a_counts.append(len(ops))
    for ws in ([w_list, dual_w] if dual else [w_list]):
        ins += list(ws)
        in_specs += [pl.BlockSpec((ka, tn), lambda j, i, rb=rb: (rb, cb + j)) for rb, ka in zip(w_row_blocks, ks)]
    n_x = 0
    if residual is not None:
        x, mods, slot = residual
        sp, ops = _rows_specs(x, tm, tn, lambda j: j)
        n_x = len(ops)
        ins += ops + [mods]
        in_specs += sp + [pl.BlockSpec((None, None, 1, tn), lambda j, i: (slot, _cond_of_block(i, tm), 0, j))]
    scratch = [pltpu.VMEM((ka, tn), BF16) for ka in ks] * (2 if dual else 1)
    body = functools.partial(_mm_body, a_counts=tuple(a_counts), dual=dual, residual=n_x, nbc=T_CTX // tm)
    return pl.pallas_call(body, grid=(n_cols // tn, T // tm), in_specs=in_specs,
                          out_specs=pl.BlockSpec((tm, tn), lambda j, i: (i, j)), scratch_shapes=scratch,
                          out_shape=jax.ShapeDtypeStruct((T, n_cols), out_dtype),
                          compiler_params=_cparams(("arbitrary", "arbitrary")), name=name)(*ins)


def _gmm_body(chg_ref, gid_ref, nact_ref, nxt_ref, a_ref, *rest, n_w, tn):
    w_refs, (o_ref, stage, wb, sem) = rest[:n_w], rest[n_w:]
    j, i = pl.program_id(0), pl.program_id(1)
    nact = nact_ref[0]

    def fetch(jj, ii):
        cols = pl.ds(pl.multiple_of(jj * tn, tn), tn)
        for m, w in enumerate(w_refs):
            pltpu.make_async_copy(w.at[gid_ref[ii], :, cols], stage.at[m], sem).start()

    refresh = (i < nact) & (chg_ref[i] != 0)

    @pl.when(refresh & (j == 0) & (i == 0))
    def _():
        fetch(0, 0)

    @pl.when(refresh)
    def _():
        for m, w in enumerate(w_refs):
            pltpu.make_async_copy(w.at[0, :, pl.ds(0, tn)], stage.at[m], sem).wait()
        for m in range(n_w):
            wb[m] = stage[m].astype(BF16)
        nxt = nxt_ref[i]

        @pl.when(nxt >= 0)
        def _():
            fetch(j, nxt)

        @pl.when((nxt < 0) & (j + 1 < pl.num_programs(0)))
        def _():
            fetch(j + 1, 0)

    @pl.when(i < nact)
    def _():
        a = a_ref[...]
        y = jnp.dot(a, wb[0], preferred_element_type=F32)
        if n_w == 2:
            y = y * jax.nn.sigmoid(y) * jnp.dot(a, wb[1], preferred_element_type=F32)
        o_ref[...] = y.astype(o_ref.dtype)

    @pl.when(i >= nact)
    def _():
        o_ref[...] = jnp.zeros_like(o_ref)


def _gmm(a, w_list, group, *, tn, tm, out_dtype, name):
    K = a.shape[1]
    n = w_list[0].shape[2]
    n_w = len(w_list)
    rowblk = lambda i, p: jnp.maximum(jnp.minimum(i, p[2][0] - 1), 0)
    gs = pltpu.PrefetchScalarGridSpec(
        num_scalar_prefetch=4, grid=(n // tn, R_MOE // tm),
        in_specs=[pl.BlockSpec((tm, K), lambda j, i, *p: (rowblk(i, p), 0))]
        + [pl.BlockSpec(memory_space=pl.ANY)] * n_w,
        out_specs=pl.BlockSpec((tm, tn), lambda j, i, *p: (i, j)),
        scratch_shapes=[pltpu.VMEM((n_w, K, tn), F32), pltpu.VMEM((n_w, K, tn), BF16), pltpu.SemaphoreType.DMA(())])
    return pl.pallas_call(functools.partial(_gmm_body, n_w=n_w, tn=tn), grid_spec=gs,
                          out_shape=jax.ShapeDtypeStruct((R_MOE, n), out_dtype),
                          compiler_params=_cparams(("arbitrary", "arbitrary")), name=name)(*group, a, *w_list)


def _ada_body(c_ref, w_ref, b_ref, o_ref):
    c = c_ref[...]
    s = (c * jax.nn.sigmoid(c)).astype(BF16)
    o_ref[...] = jnp.dot(s, w_ref[...].astype(BF16), preferred_element_type=F32) + b_ref[...]


def _ada_params(cond, w_mod, b_mod):
    depth = w_mod.shape[0]
    tn = 1024
    return pl.pallas_call(
        _ada_body, grid=(depth, 6 * D // tn),
        in_specs=[pl.BlockSpec((N_COND, D), lambda l, j: (0, 0)),
                  pl.BlockSpec((None, D, tn), lambda l, j: (l, 0, j)),
                  pl.BlockSpec((None, 1, tn), lambda l, j: (l, 0, j))],
        out_specs=pl.BlockSpec((None, N_COND, tn), lambda l, j: (l, 0, j)),
        out_shape=jax.ShapeDtypeStruct((depth, N_COND, 6 * D), F32),
        compiler_params=_cparams(("arbitrary", "arbitrary")), name="ada_params",
    )(cond, w_mod, b_mod.reshape(depth, 1, 6 * D))


def _norm_body(*refs, n_x, modulate, router):
    it = iter(refs)
    x_refs = [next(it) for _ in range(n_x)]
    g_ref = next(it)
    if modulate:
        sh_ref, sc_ref = next(it), next(it)
    if router:
        wr_ref, br_ref = next(it), next(it)
    o_ref = next(it)
    pid = pl.program_id(1)
    x = _rows_value(x_refs, pid, T_CTX // TM)
    y = x * lax.rsqrt(jnp.mean(x * x, axis=-1, keepdims=True) + EPS) * g_ref[...]
    if modulate:
        y = y * (1.0 + sc_ref[...]) + sh_ref[...]
    o_ref[...] = y.astype(o_ref.dtype)
    if router:
        idx_ref, g0_ref, g1_ref, cnt_ref, carry_ref = next(it), next(it), next(it), next(it), next(it)
        logits = jnp.dot(y, wr_ref[...], preferred_element_type=F32, precision=lax.Precision.HIGHEST) + br_ref[...]
        lane = lax.broadcasted_iota(I32, logits.shape, 1)
        logits = jnp.where(lane < N_EXP, logits, -jnp.inf)
        lanef = lane.astype(F32)
        m1 = jnp.max(logits, axis=-1, keepdims=True)
        i1 = jnp.min(jnp.where(logits == m1, lanef, float(LANES)), axis=-1, keepdims=True)
        rest = jnp.where(lanef == i1, -jnp.inf, logits)
        m2 = jnp.max(rest, axis=-1, keepdims=True)
        i2 = jnp.min(jnp.where(rest == m2, lanef, float(LANES)), axis=-1, keepdims=True)
        e21 = jnp.exp(m2 - m1)
        gate1 = 1.0 / (1.0 + e21)
        g0_ref[...] = jnp.broadcast_to(gate1, logits.shape)
        g1_ref[...] = jnp.broadcast_to(e21 * gate1, logits.shape)

        @pl.when(pid == 0)
        def _():
            carry_ref[...] = jnp.zeros_like(carry_ref)

        hot1 = jnp.where(lanef == i1, 1.0, 0.0)
        hot2 = jnp.where(lanef == i2, 1.0, 0.0)
        n = logits.shape[0]
        tri = jnp.where(lax.broadcasted_iota(I32, (n, n), 1) < lax.broadcasted_iota(I32, (n, n), 0), 1.0, 0.0)
        tri = tri.astype(BF16)
        before1 = jnp.dot(tri, hot1.astype(BF16), preferred_element_type=F32)
        before2 = jnp.dot(tri, hot2.astype(BF16), preferred_element_type=F32)
        tot1 = jnp.sum(hot1, axis=0, keepdims=True)
        tot2 = jnp.sum(hot2, axis=0, keepdims=True)
        carry = carry_ref[...]
        rank1 = jnp.sum(hot1 * (before1 + carry), axis=-1, keepdims=True)
        rank2 = jnp.sum(hot2 * (before2 + carry + tot1), axis=-1, keepdims=True)
        carry = carry + tot1 + tot2
        carry_ref[...] = carry
        cnt_ref[...] = jnp.broadcast_to(carry, cnt_ref.shape).astype(I32)
        packed = jnp.where(lane == 0, i1, jnp.where(lane == 1, i2, jnp.where(lane == 2, rank1,
                                                                             jnp.where(lane == 3, rank2, 0.0))))
        idx_ref[...] = packed.astype(I32)


def _norm(x, g, *, mods=None, slots=None, router=None, out_dtype, name, row0=0, rows=None):
    pair = isinstance(x, tuple)
    rows = (T if pair else x.shape[0]) if rows is None else rows
    rb0 = row0 // TM
    modulate = mods is not None
    in_specs, ins = _rows_specs(x, TM, D, lambda j: 0, clamp=lambda i, p: rb0 + i)
    n_x = len(ins)
    ins.append(g.reshape(1, D))
    in_specs.append(pl.BlockSpec((1, D), lambda j, i: (0, 0)))
    if modulate:
        for slot in slots:
            ins.append(mods)
            in_specs.append(pl.BlockSpec((None, None, 1, D),
                                         lambda j, i, slot=slot: (slot, _cond_of_block(i, TM), 0, 0)))
    out_shape = [jax.ShapeDtypeStruct((rows, D), out_dtype)]
    out_specs = [pl.BlockSpec((TM, D), lambda j, i: (i, 0))]
    scratch = []
    if router is not None:
        w_r, b_r = router
        ins += [jnp.pad(w_r, ((0, 0), (0, LANES - N_EXP))), jnp.pad(b_r, (0, LANES - N_EXP)).reshape(1, LANES)]
        in_specs += [pl.BlockSpec((D, LANES), lambda j, i: (0, 0)), pl.BlockSpec((1, LANES), lambda j, i: (0, 0))]
        out_shape += [jax.ShapeDtypeStruct((rows, LANES), I32), jax.ShapeDtypeStruct((rows, LANES), F32),
                      jax.ShapeDtypeStruct((rows, LANES), F32), jax.ShapeDtypeStruct((SUBLANES, LANES), I32)]
        out_specs += [pl.BlockSpec((TM, LANES), lambda j, i: (i, 0))] * 3
        out_specs += [pl.BlockSpec((SUBLANES, LANES), lambda j, i: (0, 0))]
        scratch = [pltpu.VMEM((1, LANES), F32)]
    body = functools.partial(_norm_body, n_x=n_x, modulate=modulate, router=router is not None)
    res = pl.pallas_call(body, grid=(1, rows // TM), in_specs=in_specs, out_specs=out_specs, out_shape=out_shape,
                         scratch_shapes=scratch, compiler_params=_cparams(("arbitrary", "arbitrary")), name=name)(*ins)
    return res if router is not None else res[0]


def _swap32(x):
    up = jnp.concatenate([x[:, 32:], x[:, :32]], axis=1)
    down = jnp.concatenate([x[:, 96:], x[:, :96]], axis=1)
    lane = lax.broadcasted_iota(I32, x.shape, 1)
    return jnp.where((lane % 64) < 32, up, down)


def _qkprep_body(q_ref, kv_ref, cos_ref, sin_ref, qo_ref, ko_ref, vo_ref):
    cos, sin = cos_ref[...], sin_ref[...]
    for h in range(N_HEADS):
        x = q_ref[:, h * HD:(h + 1) * HD]
        qo_ref[:, h * HD:(h + 1) * HD] = (x * cos + _swap32(x) * sin).astype(BF16)
    for h in range(N_KV):
        x = kv_ref[:, h * HD:(h + 1) * HD]
        ko_ref[:, h * HD:(h + 1) * HD] = (x * cos + _swap32(x) * sin).astype(BF16)
    vo_ref[...] = kv_ref[:, D_KV:].astype(BF16)


def _qk_prep(q, kv, cos_t, sin_t):
    return pl.pallas_call(
        _qkprep_body, grid=(T // TM,),
        in_specs=[pl.BlockSpec((TM, D_ATTN), lambda i: (i, 0)), pl.BlockSpec((TM, 2 * D_KV), lambda i: (i, 0)),
                  pl.BlockSpec((TM, HD), lambda i: (i, 0)), pl.BlockSpec((TM, HD), lambda i: (i, 0))],
        out_specs=[pl.BlockSpec((TM, D_ATTN), lambda i: (i, 0)), pl.BlockSpec((TM, D_KV), lambda i: (i, 0)),
                   pl.BlockSpec((TM, D_KV), lambda i: (i, 0))],
        out_shape=[jax.ShapeDtypeStruct((T, D_ATTN), BF16), jax.ShapeDtypeStruct((T, D_KV), BF16),
                   jax.ShapeDtypeStruct((T, D_KV), BF16)],
        compiler_params=_cparams(("arbitrary",)), name="qk_prep")(q, kv, cos_t, sin_t)


def _attn_core(q_ref, o_ref, sink_ref, kvh, kall, vall, mask):
    scale = HD ** -0.5
    for g in range(Q_PER_KV):
        qh = q_ref[:, g * HD:(g + 1) * HD]
        s = lax.dot_general(qh, kall, (((1,), (1,)), ((), ())), preferred_element_type=F32) * scale
        if mask is not None:
            s = jnp.where(mask, s, NEG_INF)
        sk = sink_ref[kvh * Q_PER_KV + g]
        m = jnp.maximum(jnp.max(s, axis=-1, keepdims=True), sk)
        p = jnp.exp(s - m)
        denom = jnp.sum(p, axis=-1, keepdims=True) + jnp.exp(sk - m)
        o = jnp.dot(p.astype(BF16), vall, preferred_element_type=F32) / denom
        o_ref[:, g * HD:(g + 1) * HD] = o.astype(o_ref.dtype)


def _attn_ctx_body(sink_ref, q_ref, k_ref, v_ref, o_ref):
    _attn_core(q_ref, o_ref, sink_ref, pl.program_id(1), k_ref[...], v_ref[...], None)


def _attn_lat_body(sink_ref, q_ref, kp_ref, kc_ref, kn_ref, vp_ref, vc_ref, vn_ref, ck_ref, cv_ref, o_ref):
    qb = pl.program_id(1)
    kall = jnp.concatenate([kp_ref[...], kc_ref[...], kn_ref[...], ck_ref[...].astype(BF16)], axis=0)
    vall = jnp.concatenate([vp_ref[...], vc_ref[...], vn_ref[...], cv_ref[...].astype(BF16)], axis=0)
    nk = kall.shape[0]
    qpos = qb * WINDOW + lax.broadcasted_iota(I32, (WINDOW, nk), 0)
    col = lax.broadcasted_iota(I32, (WINDOW, nk), 1)
    kpos = (qb - 1) * WINDOW + col
    in_win = (jnp.abs(kpos - qpos) <= WINDOW) & (kpos >= 0) & (kpos < L_LAT)
    mask = in_win | (col >= 3 * WINDOW)
    _attn_core(q_ref, o_ref, sink_ref, pl.program_id(2), kall, vall, mask)


def _attention(qb, kb, vb, cache_k, cache_v, sink):
    qw = Q_PER_KV * HD
    smem = pl.BlockSpec(memory_space=pltpu.SMEM)
    o_ctx = pl.pallas_call(
        _attn_ctx_body, grid=(B_CTX, N_KV),
        in_specs=[smem, pl.BlockSpec((L_CTX, qw), lambda b, h: (b, h)),
                  pl.BlockSpec((L_CTX, HD), lambda b, h: (b, h)), pl.BlockSpec((L_CTX, HD), lambda b, h: (b, h))],
        out_specs=pl.BlockSpec((L_CTX, qw), lambda b, h: (b, h)),
        out_shape=jax.ShapeDtypeStruct((T_CTX, D_ATTN), BF16),
        compiler_params=_cparams(("arbitrary", "arbitrary")), name="attn_ctx")(sink, qb, kb, vb)
    nb = L_LAT // WINDOW
    base = T_CTX // WINDOW

    def cur(b, i, h):
        return (base + b * nb + i, h)

    def prv(b, i, h):
        return (base + b * nb + jnp.maximum(i - 1, 0), h)

    def nxt(b, i, h):
        return (base + b * nb + jnp.minimum(i + 1, nb - 1), h)

    blk = lambda f: pl.BlockSpec((WINDOW, HD), f)
    cspec = pl.BlockSpec((None, cache_k.shape[1], HD), lambda b, i, h: (b, 0, h))
    o_lat = pl.pallas_call(
        _attn_lat_body, grid=(B_LAT, nb, N_KV),
        in_specs=[smem, pl.BlockSpec((WINDOW, qw), cur), blk(prv), blk(cur), blk(nxt), blk(prv), blk(cur), blk(nxt),
                  cspec, cspec],
        out_specs=pl.BlockSpec((WINDOW, qw), lambda b, i, h: (b * nb + i, h)),
        out_shape=jax.ShapeDtypeStruct((T_LAT, D_ATTN), BF16),
        compiler_params=_cparams(("arbitrary", "arbitrary", "arbitrary")), name="attn_lat",
    )(sink, qb, kb, kb, kb, vb, vb, vb, cache_k, cache_v)
    return o_ctx, o_lat


def _scan_pitch(L):
    return L // SUBLANES + 4


def _rglru_body(x_ref, y_ref, h0_ref, cw_ref, cb_ref, wa_ref, ba_ref, wx_ref, bx_ref, lam_ref,
                o_ref, fin_ref, a_s, b_s, p_s, h_s):
    L = x_ref.shape[0]
    lc = L // SUBLANES
    pitch = _scan_pitch(L)
    x = x_ref[...]
    row = lax.broadcasted_iota(I32, x.shape, 0)
    xc = cb_ref[...] + x * cw_ref[2:3, :]
    xc = xc + jnp.where(row >= 2, pltpu.roll(x, 2, 0), 0.0) * cw_ref[0:1, :]
    xc = xc + jnp.where(row >= 1, pltpu.roll(x, 1, 0), 0.0) * cw_ref[1:2, :]
    xc = xc + jnp.where(row < L - 1, pltpu.roll(x, L - 1, 0), 0.0) * cw_ref[3:4, :]
    xcb = xc.astype(BF16)
    for d in range(2):
        ga = jnp.dot(xcb, wa_ref[d].astype(BF16), preferred_element_type=F32) + ba_ref[d:d + 1, :]
        gx = jnp.dot(xcb, wx_ref[d].astype(BF16), preferred_element_type=F32) + bx_ref[d:d + 1, :]
        log_a = -RG_LRU_C * jax.nn.sigmoid(ga) * jax.nn.softplus(-lam_ref[d:d + 1, :])
        a = jnp.exp(log_a)
        b = jnp.sqrt(-jnp.tanh(log_a) * (1.0 + a * a)) * jax.nn.sigmoid(gx) * xc
        for s in range(SUBLANES):
            a_s[d, s * pitch:s * pitch + lc, :] = a[s * lc:(s + 1) * lc]
            b_s[d, s * pitch:s * pitch + lc, :] = b[s * lc:(s + 1) * lc]

    def step(i, carry):
        out = []
        for d, t in ((0, i), (1, lc - 1 - i)):
            rows = pl.ds(t, SUBLANES, stride=pitch)
            a = a_s[d, rows, :]
            h = a * carry[2 * d] + b_s[d, rows, :]
            p = a * carry[2 * d + 1]
            h_s[d, rows, :] = h
            p_s[d, rows, :] = p
            out += [h, p]
        return tuple(out)

    zero = jnp.zeros((SUBLANES, RNN_W), F32)
    one = jnp.ones((SUBLANES, RNN_W), F32)
    hf, pf, hb, pb = lax.fori_loop(0, lc, step, (zero, one, zero, one), unroll=4)

    r8 = lax.broadcasted_iota(I32, (SUBLANES, RNN_W), 0)

    def chunk_carry(p, h, h0, reverse):
        for k in (1, 2, 4):
            sh = SUBLANES - k if reverse else k
            m = (r8 < SUBLANES - k) if reverse else (r8 >= k)
            h = jnp.where(m, p * pltpu.roll(h, sh, 0) + h, h)
            p = jnp.where(m, p * pltpu.roll(p, sh, 0), p)
        h0 = jnp.broadcast_to(h0, (SUBLANES, RNN_W))
        state = p * h0 + h
        if reverse:
            return state, jnp.where(r8 < SUBLANES - 1, pltpu.roll(state, SUBLANES - 1, 0), h0)
        return state, jnp.where(r8 >= 1, pltpu.roll(state, 1, 0), h0)

    sf, cf = chunk_carry(pf, hf, h0_ref[0:1, :], False)
    sb, cb = chunk_carry(pb, hb, h0_ref[1:2, :], True)

    for s in range(SUBLANES):
        rows = slice(s * pitch, s * pitch + lc)
        hsum = (h_s[0, rows, :] + p_s[0, rows, :] * cf[s:s + 1, :]) + (h_s[1, rows, :] + p_s[1, rows, :] * cb[s:s + 1, :])
        o_ref[s * lc:(s + 1) * lc, :] = (hsum * jax.nn.gelu(y_ref[s * lc:(s + 1) * lc, :])).astype(o_ref.dtype)
    fin_ref[0:1, :] = sf[SUBLANES - 1:SUBLANES, :]
    fin_ref[1:2, :] = sb[0:1, :]


def _rglru(xy, h0, n_seq, L, row0, conv_w, conv_b, w_a, b_a, w_x, b_x, lam, name):
    rb0 = row0 // L
    nb = RNN_BLOCKS
    vec = lambda r: pl.BlockSpec((r, RNN_W), lambda b, n: (0, n))
    wsp = pl.BlockSpec((2, None, RNN_W, RNN_W), lambda b, n: (0, n, 0, 0))
    return pl.pallas_call(
        _rglru_body, grid=(n_seq, nb),
        in_specs=[pl.BlockSpec((L, RNN_W), lambda b, n: (rb0 + b, n)),
                  pl.BlockSpec((L, RNN_W), lambda b, n: (rb0 + b, nb + n)),
                  pl.BlockSpec((None, 2, RNN_W), lambda b, n: (b, 0, n)),
                  vec(4), vec(1), wsp, vec(2), wsp, vec(2), vec(2)],
        out_specs=[pl.BlockSpec((L, RNN_W), lambda b, n: (b, n)),
                   pl.BlockSpec((None, 2, RNN_W), lambda b, n: (b, 0, n))],
        out_shape=[jax.ShapeDtypeStruct((n_seq * L, D_RNN), BF16), jax.ShapeDtypeStruct((n_seq, 2, D_RNN), F32)],
        scratch_shapes=[pltpu.VMEM((2, SUBLANES * _scan_pitch(L), RNN_W), F32)] * 4,
        compiler_params=_cparams(("arbitrary", "arbitrary")), name=name,
    )(xy, xy, h0, conv_w, conv_b.reshape(1, D_RNN), w_a, b_a, w_x, b_x, lam)


def _hy_filter_body(fv_ref, w1_ref, b1_ref, w2_ref, b2_ref, fr_ref, w3_ref, dl_ref, o_ref, hid_s, tt_s, *, L):
    hp = lax.Precision.HIGHEST
    rowi = lax.broadcasted_iota(I32, (L, LANES), 0)
    lane = lax.broadcasted_iota(I32, (L, LANES), 1)

    def features(pos):
        posf = pos.astype(F32)
        tt = posf / (L - 1)
        ang = fv_ref[...] * (2.0 * math.pi * posf / L)
        z = jnp.where(lane == 0, tt, jnp.where(lane <= 16, jnp.cos(ang), jnp.where(lane <= 32, -jnp.sin(ang), 0.0)))
        return z, tt[:, 0:1]

    is_bwd = pl.program_id(0) == 1

    @pl.when(pl.program_id(1) == 0)
    def _():
        z, tt = features(jnp.where(is_bwd, L - rowi, rowi))
        h = jnp.sin(fr_ref[0:1, :] * (jnp.dot(z, w1_ref[...], preferred_element_type=F32, precision=hp) + b1_ref[...]))
        h = jnp.sin(fr_ref[1:2, :] * (jnp.dot(h, w2_ref[...], preferred_element_type=F32, precision=hp) + b2_ref[...]))
        hid_s[...] = h
        tt_s[...] = jnp.broadcast_to(tt, tt_s.shape)

    filt = jnp.dot(hid_s[...], w3_ref[...], preferred_element_type=F32, precision=hp)
    filt = filt * jnp.exp(-tt_s[:, 0:1] * dl_ref[...])
    dead = is_bwd & (lax.broadcasted_iota(I32, filt.shape, 0) == 0)
    o_ref[...] = jnp.where(dead, 0.0, filt).astype(o_ref.dtype)


def _hy_filter(L, f_w1, f_b1, f_w2, f_b2, f_freq, f_w3):
    bands = (HY_EMB - 1) // 2
    f = jnp.linspace(1e-4, bands - 1, bands, dtype=F32)
    fv = jnp.zeros((LANES,), F32).at[1:1 + bands].set(f).at[1 + bands:1 + 2 * bands].set(f).reshape(1, LANES)
    padw = lambda w, r, c: jnp.pad(w.astype(F32), ((0, r - w.shape[0]), (0, c - w.shape[1])))
    padv = lambda v: jnp.pad(v.astype(F32), (0, LANES - v.shape[0])).reshape(1, LANES)
    w1, w2 = padw(f_w1, LANES, LANES), padw(f_w2, LANES, LANES)
    w3 = padw(f_w3, LANES, f_w3.shape[1])
    fr = jnp.pad(f_freq.astype(F32), ((0, 0), (0, LANES - HY_W)))
    deltas = jnp.abs(jnp.linspace(HY_MIN_DECAY, HY_MAX_DECAY, D, dtype=F32)).reshape(1, D)
    tn = 1024
    per = D // tn
    full = lambda r: pl.BlockSpec((r, LANES), lambda d, j: (0, 0))
    col = lambda d, j: ((j // per) * 2 + d) * per + j % per
    return pl.pallas_call(
        functools.partial(_hy_filter_body, L=L), grid=(2, 2 * per),
        in_specs=[full(1), full(LANES), full(1), full(LANES), full(1), full(2),
                  pl.BlockSpec((LANES, tn), lambda d, j: (0, col(d, j))),
                  pl.BlockSpec((1, tn), lambda d, j: (0, j % per))],
        out_specs=pl.BlockSpec((L, tn), lambda d, j: (0, col(d, j))),
        out_shape=jax.ShapeDtypeStruct((L, 4 * D), BF16),
        scratch_shapes=[pltpu.VMEM((L, LANES), F32), pltpu.VMEM((L, LANES), F32)],
        compiler_params=_cparams(("arbitrary", "arbitrary")), name=f"hy_filter_{L}",
    )(fv, w1, padv(f_b1), w2, padv(f_b2), fr, w3, deltas)


def _dft_mats(L, fc):
    n = 2 * L
    h = fc // 2
    nf = L // h
    k = jnp.arange(L, dtype=I32)[:, None]
    t = jnp.arange(L, dtype=I32)[None, :]
    ang = ((k * t) % n).astype(F32) * (2.0 * math.pi / n)
    cos, sin = jnp.cos(ang), jnp.sin(ang)
    nyq = jnp.where(t % 2 == 0, 1.0, -1.0).astype(F32)
    is0 = k == 0
    re_f = cos
    im_f = jnp.where(is0, nyq, -sin)
    re_i = jnp.where(is0, 1.0, 2.0) * cos / n
    im_i = jnp.where(is0, nyq, -2.0 * sin) / n
    cf = jnp.concatenate([re_f.reshape(nf, h, L), im_f.reshape(nf, h, L)], axis=1)
    ci = jnp.concatenate([re_i.reshape(nf, h, L), im_i.reshape(nf, h, L)], axis=1)
    return cf.astype(BF16), jnp.swapaxes(ci, 1, 2).astype(BF16)


def _spec_body(cf_ref, c1_ref, c2_ref, o_ref):
    f = pl.program_id(1)
    fc = cf_ref.shape[0]
    h = fc // 2
    z1 = jnp.dot(cf_ref[...], c1_ref[...], preferred_element_type=F32)
    z2 = jnp.dot(cf_ref[...], c2_ref[...], preferred_element_type=F32)
    r = lax.broadcasted_iota(I32, z1.shape, 0)
    kk = f * h + jnp.where(r < h, r, r - h)
    nyq = (r == h) & (f == 0)
    odd = (kk % 2 == 1) & jnp.logical_not(nyq)
    o_ref[...] = z1 + jnp.where(odd, -z2, z2)


def _hy_spectra(filt, cf):
    nf, fc, L = cf.shape
    td = HY_TD
    per = D // td
    return pl.pallas_call(
        _spec_body, grid=(2 * per, nf),
        in_specs=[pl.BlockSpec((None, fc, L), lambda c, f: (f, 0, 0)),
                  pl.BlockSpec((L, td), lambda c, f: (0, (c // per) * 2 * per + c % per)),
                  pl.BlockSpec((L, td), lambda c, f: (0, (c // per) * 2 * per + per + c % per))],
        out_specs=pl.BlockSpec((None, fc, td), lambda c, f: (f, 0, c)),
        out_shape=jax.ShapeDtypeStruct((nf, fc, 2 * D), F32),
        compiler_params=_cparams(("arbitrary", "arbitrary")), name=f"hy_spectra_{L}")(cf, filt, filt)


HY_TC = 256
HY_TR = 2048


def _hy_conv(u_ref, cw_ref, cb_ref, L):
    u = u_ref[...]
    rows = u.shape[0]
    pos = lax.broadcasted_iota(I32, u.shape, 0) % L
    uc = cb_ref[...] + u * cw_ref[1:2, :]
    uc = uc + jnp.where(pos >= 1, pltpu.roll(u, 1, 0), 0.0) * cw_ref[0:1, :]
    return uc + jnp.where(pos < L - 1, pltpu.roll(u, rows - 1, 0), 0.0) * cw_ref[2:3, :]


def _hy_third_specs(third, row0, L):
    tr = max(L, HY_TR)
    per = D // HY_TC
    col = lambda i, c: third * per + c
    return [pl.BlockSpec((tr, HY_TC), lambda i, c: (row0 // tr + i, col(i, c))),
            pl.BlockSpec((3, HY_TC), lambda i, c: (0, col(i, c))),
            pl.BlockSpec((1, HY_TC), lambda i, c: (0, col(i, c)))]


def _hy_prep_body(u_ref, cw_ref, cb_ref, zf_ref, zb_ref, *, L):
    v = _hy_conv(u_ref, cw_ref, cb_ref, L)
    zf_ref[...] = v
    zb_ref[...] = v.astype(BF16)


def _hy_prep(u, conv_w, conv_b, row0, rows, L):
    tr = max(L, HY_TR)
    ospec = pl.BlockSpec((tr, HY_TC), lambda i, c: (i, c))
    return pl.pallas_call(
        functools.partial(_hy_prep_body, L=L), grid=(rows // tr, D // HY_TC),
        in_specs=_hy_third_specs(0, row0, L), out_specs=[ospec, ospec],
        out_shape=[jax.ShapeDtypeStruct((rows, D), F32), jax.ShapeDtypeStruct((rows, D), BF16)],
        compiler_params=_cparams(("arbitrary", "arbitrary")), name=f"hy_prep_{L}",
    )(u, conv_w, conv_b.reshape(1, 3 * D))


def _longconv_body(z_ref, cf_ref, ci_ref, s_ref, o_ref, acc_ref):
    f = pl.program_id(2)
    fc = cf_ref.shape[0]
    h = fc // 2
    zf = jnp.dot(cf_ref[...], z_ref[...], preferred_element_type=F32)
    zre, zim = zf[:h], zf[h:]
    sre, sim = s_ref[:h, :], s_ref[h:, :]
    first = (lax.broadcasted_iota(I32, zre.shape, 0) == 0) & (f == 0)
    yre = zre * sre - jnp.where(first, 0.0, zim * sim)
    yim = jnp.where(first, zim * sim, zre * sim + zim * sre)
    y = jnp.concatenate([yre, yim], axis=0).astype(BF16)
    contrib = jnp.dot(ci_ref[...], y, preferred_element_type=F32)

    @pl.when(f == 0)
    def _():
        acc_ref[...] = contrib

    @pl.when(f > 0)
    def _():
        acc_ref[...] += contrib

    @pl.when(f == pl.num_programs(2) - 1)
    def _():
        o_ref[...] = acc_ref[...]


def _longconv(z, n_seq, L, cf, ci, spec, order):
    nf, fc, _ = cf.shape
    td = HY_TD if L > TM else D
    per = D // td
    return pl.pallas_call(
        _longconv_body, grid=(n_seq, per, nf),
        in_specs=[pl.BlockSpec((L, td), lambda b, c, f: (b, c)),
                  pl.BlockSpec((None, fc, L), lambda b, c, f: (f, 0, 0)),
                  pl.BlockSpec((None, L, fc), lambda b, c, f: (f, 0, 0)),
                  pl.BlockSpec((None, fc, td), lambda b, c, f: (f, 0, order * per + c))],
        out_specs=pl.BlockSpec((L, td), lambda b, c, f: (b, c)),
        out_shape=jax.ShapeDtypeStruct((n_seq * L, D), F32),
        scratch_shapes=[pltpu.VMEM((L, td), F32)],
        compiler_params=_cparams(("arbitrary", "arbitrary", "arbitrary")), name=f"longconv_{L}_{order}",
    )(z, cf, ci, spec)


def _hy_gate_body(zc_ref, z_ref, b_ref, u_ref, cw_ref, cb_ref, *out_refs, L):
    z = _hy_conv(u_ref, cw_ref, cb_ref, L) * (zc_ref[...] + z_ref[...] * b_ref[...])
    for o_ref in out_refs:
        o_ref[...] = z.astype(o_ref.dtype)


def _hy_gate(zc, z, bias, u, conv_w, conv_b, order, row0, L, keep_f32):
    rows = zc.shape[0]
    tr = max(L, HY_TR)
    blk = pl.BlockSpec((tr, HY_TC), lambda i, c: (i, c))
    dts = ([F32] if keep_f32 else []) + [BF16]
    return pl.pallas_call(
        functools.partial(_hy_gate_body, L=L), grid=(rows // tr, D // HY_TC),
        in_specs=[blk, blk, pl.BlockSpec((1, HY_TC), lambda i, c: (0, c))] + _hy_third_specs(order + 1, row0, L),
        out_specs=[blk] * len(dts),
        out_shape=[jax.ShapeDtypeStruct((rows, D), dt) for dt in dts],
        compiler_params=_cparams(("arbitrary", "arbitrary")), name=f"hy_gate_{L}_{order}",
    )(zc, z, bias.reshape(1, D), u, conv_w, conv_b.reshape(1, 3 * D))


DISPATCH_ROWS = 256
COMBINE_ROWS = 128
DMA_UNROLL = 8


def _dispatch_body(d0_ref, d1_ref, zs_ref, h_ref, o_ref, zbuf, sem, zsem):
    base = pl.program_id(0) * DISPATCH_ROWS

    @pl.when(pl.program_id(0) == 0)
    def _():
        zbuf[...] = jnp.zeros_like(zbuf)
        for k in range(2 * N_EXP):
            fill = pltpu.make_async_copy(zbuf, o_ref.at[pl.ds(zs_ref[k], MOE_TM)], zsem)
            fill.start()
            fill.wait()

    def issue(q, c):
        for u in range(DMA_UNROLL):
            r = q * DMA_UNROLL + u
            pltpu.make_async_copy(h_ref.at[r], o_ref.at[d0_ref[base + r]], sem).start(priority=0)
            pltpu.make_async_copy(h_ref.at[r], o_ref.at[d1_ref[base + r]], sem).start(priority=1)
        return c

    lax.fori_loop(0, DISPATCH_ROWS // DMA_UNROLL, issue, 0)
    for _ in range(2):
        pltpu.make_async_copy(h_ref, h_ref, sem).wait()


def _moe_dispatch(h, dest0, dest1, zero_starts):
    sub = D // LANES
    gs = pltpu.PrefetchScalarGridSpec(
        num_scalar_prefetch=3, grid=(T // DISPATCH_ROWS,),
        in_specs=[pl.BlockSpec((DISPATCH_ROWS, sub, LANES), lambda i, *p: (i, 0, 0))],
        out_specs=pl.BlockSpec(memory_space=pl.ANY),
        scratch_shapes=[pltpu.VMEM((MOE_TM, sub, LANES), BF16), pltpu.SemaphoreType.DMA(()),
                        pltpu.SemaphoreType.DMA(())])
    out = pl.pallas_call(_dispatch_body, grid_spec=gs, out_shape=jax.ShapeDtypeStruct((R_MOE, sub, LANES), BF16),
                         compiler_params=_cparams(("arbitrary",)), name="moe_dispatch",
                         )(dest0, dest1, zero_starts, h.reshape(T, sub, LANES))
    return out.reshape(R_MOE, D)


def _combine_body(d0_ref, d1_ref, y_ref, x_ref, g0_ref, g1_ref, gate_ref, fg_ref, oc_ref, ol_ref, buf, sems):
    i = pl.program_id(0)

    def issue(step, slot):
        base = step * COMBINE_ROWS

        def body(q, c):
            for u in range(DMA_UNROLL):
                r = q * DMA_UNROLL + u
                pltpu.make_async_copy(y_ref.at[pl.ds(d0_ref[base + r], 1), :], buf.at[slot, 0, pl.ds(r, 1), :],
                                      sems.at[slot]).start(priority=0)
                pltpu.make_async_copy(y_ref.at[pl.ds(d1_ref[base + r], 1), :], buf.at[slot, 1, pl.ds(r, 1), :],
                                      sems.at[slot]).start(priority=1)
            return c

        lax.fori_loop(0, COMBINE_ROWS // DMA_UNROLL, body, 0)

    slot = i % 2

    @pl.when(i == 0)
    def _():
        issue(0, 0)

    @pl.when(i + 1 < pl.num_programs(0))
    def _():
        issue(i + 1, 1 - slot)

    pltpu.make_async_copy(buf.at[slot], buf.at[slot], sems.at[slot]).wait()
    y = g0_ref[:, 0:1] * buf[slot, 0] + g1_ref[:, 0:1] * buf[slot, 1]
    x = x_ref[...] + gate_ref[...] * y
    out = x * lax.rsqrt(jnp.mean(x * x, axis=-1, keepdims=True) + EPS) * fg_ref[...]
    nbc = T_CTX // COMBINE_ROWS

    @pl.when(i < nbc)
    def _():
        oc_ref[...] = out

    @pl.when(i >= nbc)
    def _():
        ol_ref[...] = out


def _moe_combine_norm(ybuf, dest0, dest1, g0, g1, x, mods, slot, final_g):
    nbc = T_CTX // COMBINE_ROWS
    tok = pl.BlockSpec((COMBINE_ROWS, D), lambda i, *p: (i, 0))
    gsp = pl.BlockSpec((COMBINE_ROWS, LANES), lambda i, *p: (i, 0))
    gs = pltpu.PrefetchScalarGridSpec(
        num_scalar_prefetch=2, grid=(T // COMBINE_ROWS,),
        in_specs=[pl.BlockSpec(memory_space=pl.ANY), tok, gsp, gsp,
                  pl.BlockSpec((None, None, 1, D), lambda i, *p: (slot, _cond_of_block(i, COMBINE_ROWS), 0, 0)),
                  pl.BlockSpec((1, D), lambda i, *p: (0, 0))],
        out_specs=[pl.BlockSpec((COMBINE_ROWS, D), lambda i, *p: (jnp.minimum(i, nbc - 1), 0)),
                   pl.BlockSpec((COMBINE_ROWS, D), lambda i, *p: (jnp.maximum(i - nbc, 0), 0))],
        scratch_shapes=[pltpu.VMEM((2, 2, COMBINE_ROWS, D), F32), pltpu.SemaphoreType.DMA((2,))])
    return pl.pallas_call(_combine_body, grid_spec=gs,
                          out_shape=[jax.ShapeDtypeStruct((T_CTX, D), F32), jax.ShapeDtypeStruct((T_LAT, D), F32)],
                          compiler_params=_cparams(("arbitrary",)), name="moe_combine_norm",
                          )(dest0, dest1, ybuf, x, g0, g1, mods, final_g.reshape(1, D))


def _moe_plan(idx, cnt):
    counts = cnt[0, :N_EXP]
    padded = (counts + MOE_TM - 1) // MOE_TM * MOE_TM
    p_ends = jnp.cumsum(padded)
    p_starts = p_ends - padded
    experts = jnp.arange(N_EXP, dtype=I32)[None, :]

    def dest(e, rank):
        return jnp.sum(jnp.where(e[:, None] == experts, p_starts[None, :], 0), axis=1) + rank

    def groups(tm):
        nblk = R_MOE // tm
        blk_start = jnp.arange(nblk, dtype=I32) * tm
        gid = jnp.minimum(jnp.sum((blk_start[:, None] >= p_ends[None, :]).astype(I32), axis=1), N_EXP - 1)
        nact = (p_ends[-1] // tm).astype(I32).reshape(1)
        gid = jnp.where(jnp.arange(nblk) < nact[0], gid, gid[jnp.maximum(nact[0] - 1, 0)])
        chg = jnp.concatenate([jnp.ones((1,), I32), (gid[1:] != gid[:-1]).astype(I32)])
        blk = jnp.arange(nblk, dtype=I32)
        starts = (chg == 1) & (blk < nact[0])
        later = jnp.where(starts[None, :] & (blk[None, :] > blk[:, None]), blk[None, :], nblk)
        nxt = jnp.min(later, axis=1)
        return chg, gid, nact, jnp.where(nxt == nblk, -1, nxt).astype(I32)

    tail = jnp.minimum(p_ends[-1] + jnp.arange(N_EXP, dtype=I32) * MOE_TM, R_MOE - MOE_TM)
    zero_starts = jnp.concatenate([jnp.maximum(p_ends - MOE_TM, 0), tail]).astype(I32)
    return dest(idx[:, 0], idx[:, 2]), dest(idx[:, 1], idx[:, 3]), groups, zero_starts


def _rope_tables():
    quarter = HD // 4
    inv_freq = ROPE_BASE ** (-jnp.arange(quarter, dtype=F32) / quarter)
    t = jnp.arange(L_LAT)
    row = (t // GRID_W).astype(F32)[:, None] * inv_freq
    col = (t % GRID_W).astype(F32)[:, None] * inv_freq
    ang = jnp.concatenate([row, row, col, col], axis=1)
    sign = jnp.tile(jnp.concatenate([-jnp.ones((quarter,), F32), jnp.ones((quarter,), F32)]), 2)
    cos = jnp.concatenate([jnp.ones((T_CTX, HD), F32), jnp.tile(jnp.cos(ang), (B_LAT, 1))], axis=0)
    sin = jnp.concatenate([jnp.zeros((T_CTX, HD), F32), jnp.tile(jnp.sin(ang) * sign, (B_LAT, 1))], axis=0)
    return cos, sin


def kernel(x_prompt, x_sample, cache_k, cache_v, state_rglru, c, c_ctx, w_mod, b_mod, norm_g, final_g, a_w_in, a_w_out, rnn_conv_w, rnn_conv_b, rnn_w_a, rnn_b_a, rnn_w_x, rnn_b_x, rnn_lam, attn_sink, ffn_w1, ffn_w3, ffn_w2, h_w_in, h_w_out, h_conv_w, h_conv_b, hf_w1, hf_b1, hf_w2, hf_b2, hf_freq, hf_w3, h_bias, moe_router, moe_router_b, moe_w_gate, moe_w_up, moe_w_down):
    x = (x_prompt.reshape(T_CTX, D), x_sample.reshape(T_LAT, D))
    cond = jnp.concatenate([c_ctx[None, :], c, jnp.zeros((N_COND - 1 - B_LAT, D), F32)], axis=0)
    mods_all = _ada_params(cond, w_mod, b_mod)
    mods_all = mods_all.reshape(-1, N_COND, 6, D).transpose(0, 2, 1, 3).reshape(-1, 6, N_COND, 1, D)
    cos_t, sin_t = _rope_tables()

    mods = mods_all[0]
    h = _norm(x, norm_g[0, 0], mods=mods, slots=(0, 1), out_dtype=BF16, name="norm_mix0")
    w_in = a_w_in[0]
    mm1 = functools.partial(_mm, [h], [w_in], [0], tm=2 * TM)
    q = mm1(n_cols=D_ATTN, col_off=0, tn=1024, out_dtype=F32, name="proj_q")
    kv = mm1(n_cols=2 * D_KV, col_off=D_ATTN, tn=512, out_dtype=F32, name="proj_kv")
    xy = mm1(n_cols=2 * D_RNN, col_off=D_ATTN + 2 * D_KV, tn=512, out_dtype=F32, name="proj_rnn")
    qb, kb, vb = _qk_prep(q, kv, cos_t, sin_t)
    ck = cache_k[:, 0].reshape(B_LAT, -1, D_KV)
    cv = cache_v[:, 0].reshape(B_LAT, -1, D_KV)
    o_ctx, o_lat = _attention(qb, kb, vb, ck, cv, attn_sink[0])
    rnn_w = (rnn_conv_w[0], rnn_conv_b[0], rnn_w_a[0], rnn_b_a[0], rnn_w_x[0], rnn_b_x[0], rnn_lam[0])
    r_ctx, s_ctx = _rglru(xy, jnp.zeros((B_CTX, 2, D_RNN), F32), B_CTX, L_CTX, 0, *rnn_w, name="rglru_ctx")
    r_lat, _ = _rglru(xy, state_rglru[:, 0], B_LAT, L_LAT, T_CTX, *rnn_w, name="rglru_lat")
    w_out = a_w_out[0]
    x = _mm([(o_ctx, o_lat), (r_ctx, r_lat)], [w_out, w_out], [0, 1], n_cols=D, col_off=0, tn=1024, tm=TM,
            out_dtype=F32, residual=(x, mods, 2), name="proj_out0")
    h = _norm(x, norm_g[0, 1], mods=mods, slots=(3, 4), out_dtype=BF16, name="norm_ffn0")
    hid = _mm([h], [ffn_w1[0]], [0], dual_w=[ffn_w3[0]], n_cols=D_FF, col_off=0, tn=512, tm=2 * TM,
              out_dtype=BF16, name="ffn_up")
    x = _mm([hid], [ffn_w2[0]], [0], n_cols=D, col_off=0, tn=512, tm=TM, out_dtype=F32, residual=(x, mods, 5),
            name="ffn_down")

    mods = mods_all[1]
    h = _norm(x, norm_g[1, 0], mods=mods, slots=(0, 1), out_dtype=BF16, name="norm_mix1")
    u = _mm([h], [h_w_in[0]], [0], n_cols=3 * D, col_off=0, tn=1024, tm=2 * TM, out_dtype=F32, name="hy_in")
    zs = []
    for row0, n_seq, L in ((0, B_CTX, L_CTX), (T_CTX, B_LAT, L_LAT)):
        zf, zb = _hy_prep(u, h_conv_w[0], h_conv_b[0], row0, n_seq * L, L)
        cf, ci = _dft_mats(L, min(HY_FC, 2 * L))
        filt = _hy_filter(L, hf_w1[0], hf_b1[0], hf_w2[0], hf_b2[0], hf_freq[0], hf_w3[0])
        spec = _hy_spectra(filt, cf)
        zc = _longconv(zb, n_seq, L, cf, ci, spec, 0)
        zf, zb = _hy_gate(zc, zf, h_bias[0, 0], u, h_conv_w[0], h_conv_b[0], 0, row0, L, True)
        zc = _longconv(zb, n_seq, L, cf, ci, spec, 1)
        (zb,) = _hy_gate(zc, zf, h_bias[0, 1], u, h_conv_w[0], h_conv_b[0], 1, row0, L, False)
        zs.append(zb)
    x = _mm([tuple(zs)], [h_w_out[0]], [0], n_cols=D, col_off=0, tn=1024, tm=TM, out_dtype=F32,
            residual=(x, mods, 2), name="hy_out")
    h, idx, g0, g1, cnt = _norm(x, norm_g[1, 1], mods=mods, slots=(3, 4), router=(moe_router[0], moe_router_b[0]),
                                out_dtype=BF16, name="norm_moe")
    dest0, dest1, groups, zero_starts = _moe_plan(idx, cnt)
    xs = _moe_dispatch(h, dest0, dest1, zero_starts)
    group = groups(MOE_TM)
    hid = _gmm(xs, [moe_w_gate[0], moe_w_up[0]], group, tn=1024, tm=MOE_TM, out_dtype=BF16, name="moe_up")
    ybuf = _gmm(hid, [moe_w_down[0]], group, tn=512, tm=MOE_TM, out_dtype=F32, name="moe_down")
    y_prompt, y_sample = _moe_combine_norm(ybuf, dest0, dest1, g0, g1, x, mods, 5, final_g)
    y_prompt = y_prompt.reshape(B_CTX, L_CTX, D)
    y_sample = y_sample.reshape(B_LAT, L_LAT, D)
    new_k = kv[:T_CTX, :D_KV].reshape(B_CTX, 1, L_CTX, N_KV, HD)
    new_v = kv[:T_CTX, D_KV:].reshape(B_CTX, 1, L_CTX, N_KV, HD)
    new_s = s_ctx.reshape(B_CTX, 1, 2, D_RNN)
    return (y_prompt, y_sample, new_k, new_v, new_s)
```

```python
import functools
import math

import jax
import jax.numpy as jnp
from jax import lax
from jax.experimental import pallas as pl
from jax.experimental.pallas import tpu as pltpu

F32 = jnp.float32
BF16 = jnp.bfloat16
I32 = jnp.int32

D = 2048
B_CTX, L_CTX = 32, 256
B_LAT, L_LAT = 4, 2048
T_CTX = B_CTX * L_CTX
T_LAT = B_LAT * L_LAT
T = T_CTX + T_LAT
GRID_W = 64
N_HEADS, N_KV, HD = 8, 2, 128
Q_PER_KV = N_HEADS // N_KV
D_ATTN = N_HEADS * HD
D_KV = N_KV * HD
WINDOW = 128
ROPE_BASE = 10000.0
D_RNN = D // 2
RNN_BLOCKS = 8
RNN_W = D_RNN // RNN_BLOCKS
RG_LRU_C = 8.0
HY_EMB = 33
HY_W = 64
HY_MIN_DECAY = math.log(1e-2) / 1.5
HY_MAX_DECAY = math.log(1e-2) / 0.3
D_FF = 5632
N_EXP = 8
D_FFE = 7168
EPS = 1e-6
NEG_INF = -1e30

LANES = 128
SUBLANES = 8
VMEM_LIMIT = 52 * 1024 * 1024
TM = 512
N_COND = 8
MOE_TM = 512
R_MOE = 2 * T + N_EXP * MOE_TM
HY_TD = 512
HY_FC = 1024


def _cparams(sem):
    return pltpu.CompilerParams(dimension_semantics=sem, vmem_limit_bytes=VMEM_LIMIT)


def _cond_of_block(i, tm):
    nb_ctx = T_CTX // tm
    return jnp.where(i < nb_ctx, 0, 1 + (i - nb_ctx) // (L_LAT // tm))


def _rows_value(refs, i, nbc):
    if len(refs) == 1:
        return refs[0][...]
    return jnp.where(i < nbc, refs[0][...], refs[1][...])


def _rows_specs(src, tm, bw, col, clamp=None):
    nbc = T_CTX // tm
    if isinstance(src, tuple):
        return ([pl.BlockSpec((tm, bw), lambda j, i, *p: (jnp.minimum(i, nbc - 1), col(j))),
                 pl.BlockSpec((tm, bw), lambda j, i, *p: (jnp.maximum(i - nbc, 0), col(j)))], list(src))
    row = (lambda i, p: i) if clamp is None else clamp
    return [pl.BlockSpec((tm, bw), lambda j, i, *p: (row(i, p), col(j)))], [src]


def _mm_body(*refs, a_counts, dual, residual, nbc):
    it = iter(refs)
    n_a = len(a_counts)
    a_refs = [[next(it) for _ in range(c)] for c in a_counts]
    w_refs = [[next(it) for _ in range(n_a)] for _ in range(2 if dual else 1)]
    if residual:
        x_refs = [next(it) for _ in range(residual)]
        g_ref = next(it)
    o_ref = next(it)
    wb_refs = [[next(it) for _ in range(n_a)] for _ in range(2 if dual else 1)]
    i = pl.program_id(1)

    @pl.when(i == 0)
    def _():
        for ws, wbs in zip(w_refs, wb_refs):
            for w, wb in zip(ws, wbs):
                wb[...] = w[...].astype(BF16)

    def prod(wbs):
        acc = None
        for a, wb in zip(a_refs, wbs):
            p = jnp.dot(_rows_value(a, i, nbc), wb[...], preferred_element_type=F32)
            acc = p if acc is None else acc + p
        return acc

    y = prod(wb_refs[0])
    if dual:
        y = y * jax.nn.sigmoid(y) * prod(wb_refs[1])
    if residual:
        y = _rows_value(x_refs, i, nbc) + g_ref[...] * y
    o_ref[...] = y.astype(o_ref.dtype)


def _mm(a_list, w_list, w_row_blocks, *, n_cols, col_off, tn, tm, out_dtype, name, dual_w=None, residual=None):
    width = lambda a: (a[0] if isinstance(a, tuple) else a).shape[1]
    ks = [width(a) for a in a_list]
    dual = dual_w is not None
    assert n_cols % tn == 0 and col_off % tn == 0 and T % tm == 0
    cb = col_off // tn
    in_specs, ins, a_counts = [], [], []
    for a, ka in zip(a_list, ks):
        sp, ops = _rows_specs(a, tm, ka, lambda j: 0)
        in_specs += sp
        ins += ops
        a_counts.append(len(ops))
    for ws in ([w_list, dual_w] if dual else [w_list]):
        ins += list(ws)
        in_specs += [pl.BlockSpec((ka, tn), lambda j, i, rb=rb: (rb, cb + j)) for rb, ka in zip(w_row_blocks, ks)]
    n_x = 0
    if residual is not None:
        x, mods, slot = residual
        sp, ops = _rows_specs(x, tm, tn, lambda j: j)
        n_x = len(ops)
        ins += ops + [mods]
        in_specs += sp + [pl.BlockSpec((None, None, 1, tn), lambda j, i: (slot, _cond_of_block(i, tm), 0, j))]
    scratch = [pltpu.VMEM((ka, tn), BF16) for ka in ks] * (2 if dual else 1)
    body = functools.partial(_mm_body, a_counts=tuple(a_counts), dual=dual, residual=n_x, nbc=T_CTX // tm)
    return pl.pallas_call(body, grid=(n_cols // tn, T // tm), in_specs=in_specs,
                          out_specs=pl.BlockSpec((tm, tn), lambda j, i: (i, j)), scratch_shapes=scratch,
                          out_shape=jax.ShapeDtypeStruct((T, n_cols), out_dtype),
                          compiler_params=_cparams(("arbitrary", "arbitrary")), name=name)(*ins)


def _gmm_body(chg_ref, gid_ref, nact_ref, nxt_ref, a_ref, *rest, n_w, tn):
    w_refs, (o_ref, stage, wb, sem) = rest[:n_w], rest[n_w:]
    j, i = pl.program_id(0), pl.program_id(1)
    nact = nact_ref[0]

    def fetch(jj, ii):
        cols = pl.ds(pl.multiple_of(jj * tn, tn), tn)
        for m, w in enumerate(w_refs):
            pltpu.make_async_copy(w.at[gid_ref[ii], :, cols], stage.at[m], sem).start()

    refresh = (i < nact) & (chg_ref[i] != 0)

    @pl.when(refresh & (j == 0) & (i == 0))
    def _():
        fetch(0, 0)

    @pl.when(refresh)
    def _():
        for m, w in enumerate(w_refs):
            pltpu.make_async_copy(w.at[0, :, pl.ds(0, tn)], stage.at[m], sem).wait()
        for m in range(n_w):
            wb[m] = stage[m].astype(BF16)
        nxt = nxt_ref[i]

        @pl.when(nxt >= 0)
        def _():
            fetch(j, nxt)

        @pl.when((nxt < 0) & (j + 1 < pl.num_programs(0)))
        def _():
            fetch(j + 1, 0)

    @pl.when(i < nact)
    def _():
        a = a_ref[...]
        y = jnp.dot(a, wb[0], preferred_element_type=F32)
        if n_w == 2:
            y = y * jax.nn.sigmoid(y) * jnp.dot(a, wb[1], preferred_element_type=F32)
        o_ref[...] = y.astype(o_ref.dtype)

    @pl.when(i >= nact)
    def _():
        o_ref[...] = jnp.zeros_like(o_ref)


def _gmm(a, w_list, group, *, tn, tm, out_dtype, name):
    K = a.shape[1]
    n = w_list[0].shape[2]
    n_w = len(w_list)
    rowblk = lambda i, p: jnp.maximum(jnp.minimum(i, p[2][0] - 1), 0)
    gs = pltpu.PrefetchScalarGridSpec(
        num_scalar_prefetch=4, grid=(n // tn, R_MOE // tm),
        in_specs=[pl.BlockSpec((tm, K), lambda j, i, *p: (rowblk(i, p), 0))]
        + [pl.BlockSpec(memory_space=pl.ANY)] * n_w,
        out_specs=pl.BlockSpec((tm, tn), lambda j, i, *p: (i, j)),
        scratch_shapes=[pltpu.VMEM((n_w, K, tn), F32), pltpu.VMEM((n_w, K, tn), BF16), pltpu.SemaphoreType.DMA(())])
    return pl.pallas_call(functools.partial(_gmm_body, n_w=n_w, tn=tn), grid_spec=gs,
                          out_shape=jax.ShapeDtypeStruct((R_MOE, n), out_dtype),
                          compiler_params=_cparams(("arbitrary", "arbitrary")), name=name)(*group, a, *w_list)


def _ada_body(c_ref, w_ref, b_ref, o_ref):
    c = c_ref[...]
    s = (c * jax.nn.sigmoid(c)).astype(BF16)
    o_ref[...] = jnp.dot(s, w_ref[...].astype(BF16), preferred_element_type=F32) + b_ref[...]


def _ada_params(cond, w_mod, b_mod):
    depth = w_mod.shape[0]
    tn = 1024
    return pl.pallas_call(
        _ada_body, grid=(depth, 6 * D // tn),
        in_specs=[pl.BlockSpec((N_COND, D), lambda l, j: (0, 0)),
                  pl.BlockSpec((None, D, tn), lambda l, j: (l, 0, j)),
                  pl.BlockSpec((None, 1, tn), lambda l, j: (l, 0, j))],
        out_specs=pl.BlockSpec((None, N_COND, tn), lambda l, j: (l, 0, j)),
        out_shape=jax.ShapeDtypeStruct((depth, N_COND, 6 * D), F32),
        compiler_params=_cparams(("arbitrary", "arbitrary")), name="ada_params",
    )(cond, w_mod, b_mod.reshape(depth, 1, 6 * D))


def _norm_body(*refs, n_x, modulate, router):
    it = iter(refs)
    x_refs = [next(it) for _ in range(n_x)]
    g_ref = next(it)
    if modulate:
        sh_ref, sc_ref = next(it), next(it)
    if router:
        wr_ref, br_ref = next(it), next(it)
    o_ref = next(it)
    pid = pl.program_id(1)
    x = _rows_value(x_refs, pid, T_CTX // TM)
    y = x * lax.rsqrt(jnp.mean(x * x, axis=-1, keepdims=True) + EPS) * g_ref[...]
    if modulate:
        y = y * (1.0 + sc_ref[...]) + sh_ref[...]
    o_ref[...] = y.astype(o_ref.dtype)
    if router:
        idx_ref, g0_ref, g1_ref, cnt_ref, carry_ref = next(it), next(it), next(it), next(it), next(it)
        logits = jnp.dot(y, wr_ref[...], preferred_element_type=F32, precision=lax.Precision.HIGHEST) + br_ref[...]
        lane = lax.broadcasted_iota(I32, logits.shape, 1)
        logits = jnp.where(lane < N_EXP, logits, -jnp.inf)
        lanef = lane.astype(F32)
        m1 = jnp.max(logits, axis=-1, keepdims=True)
        i1 = jnp.min(jnp.where(logits == m1, lanef, float(LANES)), axis=-1, keepdims=True)
        rest = jnp.where(lanef == i1, -jnp.inf, logits)
        m2 = jnp.max(rest, axis=-1, keepdims=True)
        i2 = jnp.min(jnp.where(rest == m2, lanef, float(LANES)), axis=-1, keepdims=True)
        e21 = jnp.exp(m2 - m1)
        gate1 = 1.0 / (1.0 + e21)
        g0_ref[...] = jnp.broadcast_to(gate1, logits.shape)
        g1_ref[...] = jnp.broadcast_to(e21 * gate1, logits.shape)

        @pl.when(pid == 0)
        def _():
            carry_ref[...] = jnp.zeros_like(carry_ref)

        hot1 = jnp.where(lanef == i1, 1.0, 0.0)
        hot2 = jnp.where(lanef == i2, 1.0, 0.0)
        n = logits.shape[0]
        tri = jnp.where(lax.broadcasted_iota(I32, (n, n), 1) < lax.broadcasted_iota(I32, (n, n), 0), 1.0, 0.0)
        tri = tri.astype(BF16)
        before1 = jnp.dot(tri, hot1.astype(BF16), preferred_element_type=F32)
        before2 = jnp.dot(tri, hot2.astype(BF16), preferred_element_type=F32)
        tot1 = jnp.sum(hot1, axis=0, keepdims=True)
        tot2 = jnp.sum(hot2, axis=0, keepdims=True)
        carry = carry_ref[...]
        rank1 = jnp.sum(hot1 * (before1 + carry), axis=-1, keepdims=True)
        rank2 = jnp.sum(hot2 * (before2 + carry + tot1), axis=-1, keepdims=True)
        carry = carry + tot1 + tot2
        carry_ref[...] = carry
        cnt_ref[...] = jnp.broadcast_to(carry, cnt_ref.shape).astype(I32)
        packed = jnp.where(lane == 0, i1, jnp.where(lane == 1, i2, jnp.where(lane == 2, rank1,
                                                                             jnp.where(lane == 3, rank2, 0.0))))
        idx_ref[...] = packed.astype(I32)


def _norm(x, g, *, mods=None, slots=None, router=None, out_dtype, name, row0=0, rows=None):
    pair = isinstance(x, tuple)
    rows = (T if pair else x.shape[0]) if rows is None else rows
    rb0 = row0 // TM
    modulate = mods is not None
    in_specs, ins = _rows_specs(x, TM, D, lambda j: 0, clamp=lambda i, p: rb0 + i)
    n_x = len(ins)
    ins.append(g.reshape(1, D))
    in_specs.append(pl.BlockSpec((1, D), lambda j, i: (0, 0)))
    if modulate:
        for slot in slots:
            ins.append(mods)
            in_specs.append(pl.BlockSpec((None, None, 1, D),
                                         lambda j, i, slot=slot: (slot, _cond_of_block(i, TM), 0, 0)))
    out_shape = [jax.ShapeDtypeStruct((rows, D), out_dtype)]
    out_specs = [pl.BlockSpec((TM, D), lambda j, i: (i, 0))]
    scratch = []
    if router is not None:
        w_r, b_r = router
        ins += [jnp.pad(w_r, ((0, 0), (0, LANES - N_EXP))), jnp.pad(b_r, (0, LANES - N_EXP)).reshape(1, LANES)]
        in_specs += [pl.BlockSpec((D, LANES), lambda j, i: (0, 0)), pl.BlockSpec((1, LANES), lambda j, i: (0, 0))]
        out_shape += [jax.ShapeDtypeStruct((rows, LANES), I32), jax.ShapeDtypeStruct((rows, LANES), F32),
                      jax.ShapeDtypeStruct((rows, LANES), F32), jax.ShapeDtypeStruct((SUBLANES, LANES), I32)]
        out_specs += [pl.BlockSpec((TM, LANES), lambda j, i: (i, 0))] * 3
        out_specs += [pl.BlockSpec((SUBLANES, LANES), lambda j, i: (0, 0))]
        scratch = [pltpu.VMEM((1, LANES), F32)]
    body = functools.partial(_norm_body, n_x=n_x, modulate=modulate, router=router is not None)
    res = pl.pallas_call(body, grid=(1, rows // TM), in_specs=in_specs, out_specs=out_specs, out_shape=out_shape,
                         scratch_shapes=scratch, compiler_params=_cparams(("arbitrary", "arbitrary")), name=name)(*ins)
    return res if router is not None else res[0]


def _swap32(x):
    up = jnp.concatenate([x[:, 32:], x[:, :32]], axis=1)
    down = jnp.concatenate([x[:, 96:], x[:, :96]], axis=1)
    lane = lax.broadcasted_iota(I32, x.shape, 1)
    return jnp.where((lane % 64) < 32, up, down)


def _qkprep_body(q_ref, kv_ref, cos_ref, sin_ref, qo_ref, ko_ref, vo_ref):
    cos, sin = cos_ref[...], sin_ref[...]
    for h in range(N_HEADS):
        x = q_ref[:, h * HD:(h + 1) * HD]
        qo_ref[:, h * HD:(h + 1) * HD] = (x * cos + _swap32(x) * sin).astype(BF16)
    for h in range(N_KV):
        x = kv_ref[:, h * HD:(h + 1) * HD]
        ko_ref[:, h * HD:(h + 1) * HD] = (x * cos + _swap32(x) * sin).astype(BF16)
    vo_ref[...] = kv_ref[:, D_KV:].astype(BF16)


def _qk_prep(q, kv, cos_t, sin_t):
    return pl.pallas_call(
        _qkprep_body, grid=(T // TM,),
        in_specs=[pl.BlockSpec((TM, D_ATTN), lambda i: (i, 0)), pl.BlockSpec((TM, 2 * D_KV), lambda i: (i, 0)),
                  pl.BlockSpec((TM, HD), lambda i: (i, 0)), pl.BlockSpec((TM, HD), lambda i: (i, 0))],
        out_specs=[pl.BlockSpec((TM, D_ATTN), lambda i: (i, 0)), pl.BlockSpec((TM, D_KV), lambda i: (i, 0)),
                   pl.BlockSpec((TM, D_KV), lambda i: (i, 0))],
        out_shape=[jax.ShapeDtypeStruct((T, D_ATTN), BF16), jax.ShapeDtypeStruct((T, D_KV), BF16),
                   jax.ShapeDtypeStruct((T, D_KV), BF16)],
        compiler_params=_cparams(("arbitrary",)), name="qk_prep")(q, kv, cos_t, sin_t)


def _attn_core(q_ref, o_ref, sink_ref, kvh, kall, vall, mask):
    scale = HD ** -0.5
    for g in range(Q_PER_KV):
        qh = q_ref[:, g * HD:(g + 1) * HD]
        s = lax.dot_general(qh, kall, (((1,), (1,)), ((), ())), preferred_element_type=F32) * scale
        if mask is not None:
            s = jnp.where(mask, s, NEG_INF)
        sk = sink_ref[kvh * Q_PER_KV + g]
        m = jnp.maximum(jnp.max(s, axis=-1, keepdims=True), sk)
        p = jnp.exp(s - m)
        denom = jnp.sum(p, axis=-1, keepdims=True) + jnp.exp(sk - m)
        o = jnp.dot(p.astype(BF16), vall, preferred_element_type=F32) / denom
        o_ref[:, g * HD:(g + 1) * HD] = o.astype(o_ref.dtype)


def _attn_ctx_body(sink_ref, q_ref, k_ref, v_ref, o_ref):
    _attn_core(q_ref, o_ref, sink_ref, pl.program_id(1), k_ref[...], v_ref[...], None)


def _attn_lat_body(sink_ref, q_ref, kp_ref, kc_ref, kn_ref, vp_ref, vc_ref, vn_ref, ck_ref, cv_ref, o_ref):
    qb = pl.program_id(1)
    kall = jnp.concatenate([kp_ref[...], kc_ref[...], kn_ref[...], ck_ref[...].astype(BF16)], axis=0)
    vall = jnp.concatenate([vp_ref[...], vc_ref[...], vn_ref[...], cv_ref[...].astype(BF16)], axis=0)
    nk = kall.shape[0]
    qpos = qb * WINDOW + lax.broadcasted_iota(I32, (WINDOW, nk), 0)
    col = lax.broadcasted_iota(I32, (WINDOW, nk), 1)
    kpos = (qb - 1) * WINDOW + col
    in_win = (jnp.abs(kpos - qpos) <= WINDOW) & (kpos >= 0) & (kpos < L_LAT)
    mask = in_win | (col >= 3 * WINDOW)
    _attn_core(q_ref, o_ref, sink_ref, pl.program_id(2), kall, vall, mask)


def _attention(qb, kb, vb, cache_k, cache_v, sink):
    qw = Q_PER_KV * HD
    smem = pl.BlockSpec(memory_space=pltpu.SMEM)
    o_ctx = pl.pallas_call(
        _attn_ctx_body, grid=(B_CTX, N_KV),
        in_specs=[smem, pl.BlockSpec((L_CTX, qw), lambda b, h: (b, h)),
                  pl.BlockSpec((L_CTX, HD), lambda b, h: (b, h)), pl.BlockSpec((L_CTX, HD), lambda b, h: (b, h))],
        out_specs=pl.BlockSpec((L_CTX, qw), lambda b, h: (b, h)),
        out_shape=jax.ShapeDtypeStruct((T_CTX, D_ATTN), BF16),
        compiler_params=_cparams(("arbitrary", "arbitrary")), name="attn_ctx")(sink, qb, kb, vb)
    nb = L_LAT // WINDOW
    base = T_CTX // WINDOW

    def cur(b, i, h):
        return (base + b * nb + i, h)

    def prv(b, i, h):
        return (base + b * nb + jnp.maximum(i - 1, 0), h)

    def nxt(b, i, h):
        return (base + b * nb + jnp.minimum(i + 1, nb - 1), h)

    blk = lambda f: pl.BlockSpec((WINDOW, HD), f)
    cspec = pl.BlockSpec((None, cache_k.shape[1], HD), lambda b, i, h: (b, 0, h))
    o_lat = pl.pallas_call(
        _attn_lat_body, grid=(B_LAT, nb, N_KV),
        in_specs=[smem, pl.BlockSpec((WINDOW, qw), cur), blk(prv), blk(cur), blk(nxt), blk(prv), blk(cur), blk(nxt),
                  cspec, cspec],
        out_specs=pl.BlockSpec((WINDOW, qw), lambda b, i, h: (b * nb + i, h)),
        out_shape=jax.ShapeDtypeStruct((T_LAT, D_ATTN), BF16),
        compiler_params=_cparams(("arbitrary", "arbitrary", "arbitrary")), name="attn_lat",
    )(sink, qb, kb, kb, kb, vb, vb, vb, cache_k, cache_v)
    return o_ctx, o_lat


def _scan_pitch(L):
    return L // SUBLANES + 4


def _rglru_body(x_ref, y_ref, h0_ref, cw_ref, cb_ref, wa_ref, ba_ref, wx_ref, bx_ref, lam_ref,
                o_ref, fin_ref, a_s, b_s, p_s, h_s):
    L = x_ref.shape[0]
    lc = L // SUBLANES
    pitch = _scan_pitch(L)
    x = x_ref[...]
    row = lax.broadcasted_iota(I32, x.shape, 0)
    xc = cb_ref[...] + x * cw_ref[2:3, :]
    xc = xc + jnp.where(row >= 2, pltpu.roll(x, 2, 0), 0.0) * cw_ref[0:1, :]
    xc = xc + jnp.where(row >= 1, pltpu.roll(x, 1, 0), 0.0) * cw_ref[1:2, :]
    xc = xc + jnp.where(row < L - 1, pltpu.roll(x, L - 1, 0), 0.0) * cw_ref[3:4, :]
    xcb = xc.astype(BF16)
    for d in range(2):
        ga = jnp.dot(xcb, wa_ref[d].astype(BF16), preferred_element_type=F32) + ba_ref[d:d + 1, :]
        gx = jnp.dot(xcb, wx_ref[d].astype(BF16), preferred_element_type=F32) + bx_ref[d:d + 1, :]
        log_a = -RG_LRU_C * jax.nn.sigmoid(ga) * jax.nn.softplus(-lam_ref[d:d + 1, :])
        a = jnp.exp(log_a)
        b = jnp.sqrt(-jnp.tanh(log_a) * (1.0 + a * a)) * jax.nn.sigmoid(gx) * xc
        for s in range(SUBLANES):
            a_s[d, s * pitch:s * pitch + lc, :] = a[s * lc:(s + 1) * lc]
            b_s[d, s * pitch:s * pitch + lc, :] = b[s * lc:(s + 1) * lc]

    def step(i, carry):
        out = []
        for d, t in ((0, i), (1, lc - 1 - i)):
            rows = pl.ds(t, SUBLANES, stride=pitch)
            a = a_s[d, rows, :]
            h = a * carry[2 * d] + b_s[d, rows, :]
            p = a * carry[2 * d + 1]
            h_s[d, rows, :] = h
            p_s[d, rows, :] = p
            out += [h, p]
        return tuple(out)

    zero = jnp.zeros((SUBLANES, RNN_W), F32)
    one = jnp.ones((SUBLANES, RNN_W), F32)
    hf, pf, hb, pb = lax.fori_loop(0, lc, step, (zero, one, zero, one), unroll=4)

    r8 = lax.broadcasted_iota(I32, (SUBLANES, RNN_W), 0)

    def chunk_carry(p, h, h0, reverse):
        for k in (1, 2, 4):
            sh = SUBLANES - k if reverse else k
            m = (r8 < SUBLANES - k) if reverse else (r8 >= k)
            h = jnp.where(m, p * pltpu.roll(h, sh, 0) + h, h)
            p = jnp.where(m, p * pltpu.roll(p, sh, 0), p)
        h0 = jnp.broadcast_to(h0, (SUBLANES, RNN_W))
        state = p * h0 + h
        if reverse:
            return state, jnp.where(r8 < SUBLANES - 1, pltpu.roll(state, SUBLANES - 1, 0), h0)
        return state, jnp.where(r8 >= 1, pltpu.roll(state, 1, 0), h0)

    sf, cf = chunk_carry(pf, hf, h0_ref[0:1, :], False)
    sb, cb = chunk_carry(pb, hb, h0_ref[1:2, :], True)

    for s in range(SUBLANES):
        rows = slice(s * pitch, s * pitch + lc)
        hsum = (h_s[0, rows, :] + p_s[0, rows, :] * cf[s:s + 1, :]) + (h_s[1, rows, :] + p_s[1, rows, :] * cb[s:s + 1, :])
        o_ref[s * lc:(s + 1) * lc, :] = (hsum * jax.nn.gelu(y_ref[s * lc:(s + 1) * lc, :])).astype(o_ref.dtype)
    fin_ref[0:1, :] = sf[SUBLANES - 1:SUBLANES, :]
    fin_ref[1:2, :] = sb[0:1, :]


def _rglru(xy, h0, n_seq, L, row0, conv_w, conv_b, w_a, b_a, w_x, b_x, lam, name):
    rb0 = row0 // L
    nb = RNN_BLOCKS
    vec = lambda r: pl.BlockSpec((r, RNN_W), lambda b, n: (0, n))
    wsp = pl.BlockSpec((2, None, RNN_W, RNN_W), lambda b, n: (0, n, 0, 0))
    return pl.pallas_call(
        _rglru_body, grid=(n_seq, nb),
        in_specs=[pl.BlockSpec((L, RNN_W), lambda b, n: (rb0 + b, n)),
                  pl.BlockSpec((L, RNN_W), lambda b, n: (rb0 + b, nb + n)),
                  pl.BlockSpec((None, 2, RNN_W), lambda b, n: (b, 0, n)),
                  vec(4), vec(1), wsp, vec(2), wsp, vec(2), vec(2)],
        out_specs=[pl.BlockSpec((L, RNN_W), lambda b, n: (b, n)),
                   pl.BlockSpec((None, 2, RNN_W), lambda b, n: (b, 0, n))],
        out_shape=[jax.ShapeDtypeStruct((n_seq * L, D_RNN), BF16), jax.ShapeDtypeStruct((n_seq, 2, D_RNN), F32)],
        scratch_shapes=[pltpu.VMEM((2, SUBLANES * _scan_pitch(L), RNN_W), F32)] * 4,
        compiler_params=_cparams(("arbitrary", "arbitrary")), name=name,
    )(xy, xy, h0, conv_w, conv_b.reshape(1, D_RNN), w_a, b_a, w_x, b_x, lam)


def _hy_filter_body(fv_ref, w1_ref, b1_ref, w2_ref, b2_ref, fr_ref, w3_ref, dl_ref, o_ref, hid_s, tt_s, *, L):
    hp = lax.Precision.HIGHEST
    rowi = lax.broadcasted_iota(I32, (L, LANES), 0)
    lane = lax.broadcasted_iota(I32, (L, LANES), 1)

    def features(pos):
        posf = pos.astype(F32)
        tt = posf / (L - 1)
        ang = fv_ref[...] * (2.0 * math.pi * posf / L)
        z = jnp.where(lane == 0, tt, jnp.where(lane <= 16, jnp.cos(ang), jnp.where(lane <= 32, -jnp.sin(ang), 0.0)))
        return z, tt[:, 0:1]

    is_bwd = pl.program_id(0) == 1

    @pl.when(pl.program_id(1) == 0)
    def _():
        z, tt = features(jnp.where(is_bwd, L - rowi, rowi))
        h = jnp.sin(fr_ref[0:1, :] * (jnp.dot(z, w1_ref[...], preferred_element_type=F32, precision=hp) + b1_ref[...]))
        h = jnp.sin(fr_ref[1:2, :] * (jnp.dot(h, w2_ref[...], preferred_element_type=F32, precision=hp) + b2_ref[...]))
        hid_s[...] = h
        tt_s[...] = jnp.broadcast_to(tt, tt_s.shape)

    filt = jnp.dot(hid_s[...], w3_ref[...], preferred_element_type=F32, precision=hp)
    filt = filt * jnp.exp(-tt_s[:, 0:1] * dl_ref[...])
    dead = is_bwd & (lax.broadcasted_iota(I32, filt.shape, 0) == 0)
    o_ref[...] = jnp.where(dead, 0.0, filt).astype(o_ref.dtype)


def _hy_filter(L, f_w1, f_b1, f_w2, f_b2, f_freq, f_w3):
    bands = (HY_EMB - 1) // 2
    f = jnp.linspace(1e-4, bands - 1, bands, dtype=F32)
    fv = jnp.zeros((LANES,), F32).at[1:1 + bands].set(f).at[1 + bands:1 + 2 * bands].set(f).reshape(1, LANES)
    padw = lambda w, r, c: jnp.pad(w.astype(F32), ((0, r - w.shape[0]), (0, c - w.shape[1])))
    padv = lambda v: jnp.pad(v.astype(F32), (0, LANES - v.shape[0])).reshape(1, LANES)
    w1, w2 = padw(f_w1, LANES, LANES), padw(f_w2, LANES, LANES)
    w3 = padw(f_w3, LANES, f_w3.shape[1])
    fr = jnp.pad(f_freq.astype(F32), ((0, 0), (0, LANES - HY_W)))
    deltas = jnp.abs(jnp.linspace(HY_MIN_DECAY, HY_MAX_DECAY, D, dtype=F32)).reshape(1, D)
    tn = 1024
    per = D // tn
    full = lambda r: pl.BlockSpec((r, LANES), lambda d, j: (0, 0))
    col = lambda d, j: ((j // per) * 2 + d) * per + j % per
    return pl.pallas_call(
        functools.partial(_hy_filter_body, L=L), grid=(2, 2 * per),
        in_specs=[full(1), full(LANES), full(1), full(LANES), full(1), full(2),
                  pl.BlockSpec((LANES, tn), lambda d, j: (0, col(d, j))),
                  pl.BlockSpec((1, tn), lambda d, j: (0, j % per))],
        out_specs=pl.BlockSpec((L, tn), lambda d, j: (0, col(d, j))),
        out_shape=jax.ShapeDtypeStruct((L, 4 * D), BF16),
        scratch_shapes=[pltpu.VMEM((L, LANES), F32), pltpu.VMEM((L, LANES), F32)],
        compiler_params=_cparams(("arbitrary", "arbitrary")), name=f"hy_filter_{L}",
    )(fv, w1, padv(f_b1), w2, padv(f_b2), fr, w3, deltas)


DFT_RB = 64


def _dft_body(ca_ref, sa_ref, cb_ref, sb_ref, cf_ref, ci_ref, *, L):
    n = 2 * L
    ca, sa, cb, sb = ca_ref[...], sa_ref[...], cb_ref[...], sb_ref[...]
    cos = ca * cb - sa * sb
    sin = sa * cb + ca * sb
    r = pl.program_id(0) * DFT_RB + lax.broadcasted_iota(I32, cos.shape, 0)
    j = lax.broadcasted_iota(I32, cos.shape, 1)
    alt_j = jnp.where(j % 2 == 0, 1.0, -1.0)
    alt_r = jnp.where(r % 2 == 0, 1.0, -1.0)
    cf_ref[0] = cos.astype(BF16)
    cf_ref[1] = jnp.where(r == 0, alt_j, -sin).astype(BF16)
    w = jnp.where(j == 0, 1.0, 2.0) / n
    ci_ref[0] = (w * cos).astype(BF16)
    ci_ref[1] = jnp.where(j == 0, alt_r / n, -w * sin).astype(BF16)


def _dft_mats(L):
    n = 2 * L
    j = jnp.arange(L, dtype=I32)[None, :]
    ang = lambda r: ((r[:, None] * j) % n).astype(F32) * (2.0 * math.pi / n)
    ang_a = ang(jnp.arange(L // DFT_RB, dtype=I32) * DFT_RB)
    ang_b = ang(jnp.arange(DFT_RB, dtype=I32))
    row = pl.BlockSpec((None, 1, L), lambda a: (a, 0, 0))
    full = pl.BlockSpec((DFT_RB, L), lambda a: (0, 0))
    out = pl.BlockSpec((2, DFT_RB, L), lambda a: (0, a, 0))
    return pl.pallas_call(
        functools.partial(_dft_body, L=L), grid=(L // DFT_RB,), in_specs=[row, row, full, full],
        out_specs=[out, out], out_shape=[jax.ShapeDtypeStruct((2, L, L), BF16)] * 2,
        compiler_params=_cparams(("arbitrary",)), name=f"dft_tables_{L}",
    )(jnp.cos(ang_a)[:, None, :], jnp.sin(ang_a)[:, None, :], jnp.cos(ang_b), jnp.sin(ang_b))


def _spec_body(cf_ref, c1_ref, c2_ref, o_ref):
    f = pl.program_id(1)
    h = cf_ref.shape[1]
    k = f * h + lax.broadcasted_iota(I32, (h, c1_ref.shape[1]), 0)
    odd = k % 2 == 1
    for part in range(2):
        z1 = jnp.dot(cf_ref[part], c1_ref[...], preferred_element_type=F32)
        z2 = jnp.dot(cf_ref[part], c2_ref[...], preferred_element_type=F32)
        flip = odd if part == 0 else odd & (k != 0)
        o_ref[part * h:(part + 1) * h, :] = z1 + jnp.where(flip, -z2, z2)


def _hy_spectra(filt, cf, h):
    L = cf.shape[1]
    nf = L // h
    td = HY_TD
    per = D // td
    return pl.pallas_call(
        _spec_body, grid=(2 * per, nf),
        in_specs=[pl.BlockSpec((2, h, L), lambda c, f: (0, f, 0)),
                  pl.BlockSpec((L, td), lambda c, f: (0, (c // per) * 2 * per + c % per)),
                  pl.BlockSpec((L, td), lambda c, f: (0, (c // per) * 2 * per + per + c % per))],
        out_specs=pl.BlockSpec((None, 2 * h, td), lambda c, f: (f, 0, c)),
        out_shape=jax.ShapeDtypeStruct((nf, 2 * h, 2 * D), F32),
        compiler_params=_cparams(("arbitrary", "arbitrary")), name=f"hy_spectra_{L}")(cf, filt, filt)


HY_TC = 256
HY_TR = 2048


def _hy_conv(u_ref, cw_ref, cb_ref, L):
    u = u_ref[...]
    rows = u.shape[0]
    pos = lax.broadcasted_iota(I32, u.shape, 0) % L
    uc = cb_ref[...] + u * cw_ref[1:2, :]
    uc = uc + jnp.where(pos >= 1, pltpu.roll(u, 1, 0), 0.0) * cw_ref[0:1, :]
    return uc + jnp.where(pos < L - 1, pltpu.roll(u, rows - 1, 0), 0.0) * cw_ref[2:3, :]


def _hy_third_specs(third, row0, L):
    tr = max(L, HY_TR)
    per = D // HY_TC
    col = lambda i, c: third * per + c
    return [pl.BlockSpec((tr, HY_TC), lambda i, c: (row0 // tr + i, col(i, c))),
            pl.BlockSpec((3, HY_TC), lambda i, c: (0, col(i, c))),
            pl.BlockSpec((1, HY_TC), lambda i, c: (0, col(i, c)))]


def _hy_prep_body(u_ref, cw_ref, cb_ref, zb_ref, *, L):
    zb_ref[...] = _hy_conv(u_ref, cw_ref, cb_ref, L).astype(BF16)


def _hy_prep(u, conv_w, conv_b, row0, rows, L):
    tr = max(L, HY_TR)
    return pl.pallas_call(
        functools.partial(_hy_prep_body, L=L), grid=(rows // tr, D // HY_TC),
        in_specs=_hy_third_specs(0, row0, L), out_specs=pl.BlockSpec((tr, HY_TC), lambda i, c: (i, c)),
        out_shape=jax.ShapeDtypeStruct((rows, D), BF16),
        compiler_params=_cparams(("arbitrary", "arbitrary")), name=f"hy_prep_{L}",
    )(u, conv_w, conv_b.reshape(1, 3 * D))


def _longconv_body(z_ref, cf_ref, ci_ref, s_ref, o_ref, acc_ref):
    f = pl.program_id(2)
    h = cf_ref.shape[1]
    z = z_ref[...]
    zre = jnp.dot(cf_ref[0], z, preferred_element_type=F32)
    zim = jnp.dot(cf_ref[1], z, preferred_element_type=F32)
    sre, sim = s_ref[:h, :], s_ref[h:, :]
    first = (lax.broadcasted_iota(I32, zre.shape, 0) == 0) & (f == 0)
    yre = zre * sre - jnp.where(first, 0.0, zim * sim)
    yim = jnp.where(first, zim * sim, zre * sim + zim * sre)
    contrib = (jnp.dot(ci_ref[0], yre.astype(BF16), preferred_element_type=F32)
               + jnp.dot(ci_ref[1], yim.astype(BF16), preferred_element_type=F32))

    @pl.when(f == 0)
    def _():
        acc_ref[...] = contrib

    @pl.when(f > 0)
    def _():
        acc_ref[...] += contrib

    @pl.when(f == pl.num_programs(2) - 1)
    def _():
        o_ref[...] = acc_ref[...]


def _longconv(z, n_seq, L, cf, ci, spec, order):
    nf, fc, _ = spec.shape
    h = fc // 2
    td = HY_TD if L > TM else D
    per = D // td
    return pl.pallas_call(
        _longconv_body, grid=(n_seq, per, nf),
        in_specs=[pl.BlockSpec((L, td), lambda b, c, f: (b, c)),
                  pl.BlockSpec((2, h, L), lambda b, c, f: (0, f, 0)),
                  pl.BlockSpec((2, L, h), lambda b, c, f: (0, 0, f)),
                  pl.BlockSpec((None, fc, td), lambda b, c, f: (f, 0, order * per + c))],
        out_specs=pl.BlockSpec((L, td), lambda b, c, f: (b, c)),
        out_shape=jax.ShapeDtypeStruct((n_seq * L, D), F32),
        scratch_shapes=[pltpu.VMEM((L, td), F32)],
        compiler_params=_cparams(("arbitrary", "arbitrary", "arbitrary")), name=f"longconv_{L}_{order}",
    )(z, cf, ci, spec)


def _hy_gate_body(zc_ref, b_ref, *refs, L, first):
    if first:
        z = _hy_conv(*refs[0:3], L)
        gate_refs, out_refs = refs[3:6], refs[6:]
    else:
        z = refs[0][...]
        gate_refs, out_refs = refs[1:4], refs[4:]
    z = _hy_conv(*gate_refs, L) * (zc_ref[...] + z * b_ref[...])
    for o_ref in out_refs:
        o_ref[...] = z.astype(o_ref.dtype)


def _hy_gate(zc, z, bias, u, conv_w, conv_b, order, row0, L):
    rows = zc.shape[0]
    tr = max(L, HY_TR)
    blk = pl.BlockSpec((tr, HY_TC), lambda i, c: (i, c))
    first = z is None
    uargs = (u, conv_w, conv_b.reshape(1, 3 * D))
    ins = [zc, bias.reshape(1, D)] + (list(uargs) if first else [z]) + list(uargs)
    in_specs = ([blk, pl.BlockSpec((1, HY_TC), lambda i, c: (0, c))]
                + (_hy_third_specs(0, row0, L) if first else [blk]) + _hy_third_specs(order + 1, row0, L))
    dts = ([F32] if first else []) + [BF16]
    return pl.pallas_call(
        functools.partial(_hy_gate_body, L=L, first=first), grid=(rows // tr, D // HY_TC),
        in_specs=in_specs, out_specs=[blk] * len(dts),
        out_shape=[jax.ShapeDtypeStruct((rows, D), dt) for dt in dts],
        compiler_params=_cparams(("arbitrary", "arbitrary")), name=f"hy_gate_{L}_{order}",
    )(*ins)


DISPATCH_ROWS = 256
COMBINE_ROWS = 128
DMA_UNROLL = 8


def _dispatch_body(d0_ref, d1_ref, zs_ref, h_ref, o_ref, zbuf, sem, zsem):
    base = pl.program_id(0) * DISPATCH_ROWS

    @pl.when(pl.program_id(0) == 0)
    def _():
        zbuf[...] = jnp.zeros_like(zbuf)
        for k in range(2 * N_EXP):
            fill = pltpu.make_async_copy(zbuf, o_ref.at[pl.ds(zs_ref[k], MOE_TM)], zsem)
            fill.start()
            fill.wait()

    def issue(q, c):
        for u in range(DMA_UNROLL):
            r = q * DMA_UNROLL + u
            pltpu.make_async_copy(h_ref.at[r], o_ref.at[d0_ref[base + r]], sem).start(priority=0)
            pltpu.make_async_copy(h_ref.at[r], o_ref.at[d1_ref[base + r]], sem).start(priority=1)
        return c

    lax.fori_loop(0, DISPATCH_ROWS // DMA_UNROLL, issue, 0)
    for _ in range(2):
        pltpu.make_async_copy(h_ref, h_ref, sem).wait()


def _moe_dispatch(h, dest0, dest1, zero_starts):
    sub = D // LANES
    gs = pltpu.PrefetchScalarGridSpec(
        num_scalar_prefetch=3, grid=(T // DISPATCH_ROWS,),
        in_specs=[pl.BlockSpec((DISPATCH_ROWS, sub, LANES), lambda i, *p: (i, 0, 0))],
        out_specs=pl.BlockSpec(memory_space=pl.ANY),
        scratch_shapes=[pltpu.VMEM((MOE_TM, sub, LANES), BF16), pltpu.SemaphoreType.DMA(()),
                        pltpu.SemaphoreType.DMA(())])
    out = pl.pallas_call(_dispatch_body, grid_spec=gs, out_shape=jax.ShapeDtypeStruct((R_MOE, sub, LANES), BF16),
                         compiler_params=_cparams(("arbitrary",)), name="moe_dispatch",
                         )(dest0, dest1, zero_starts, h.reshape(T, sub, LANES))
    return out.reshape(R_MOE, D)


def _combine_body(d0_ref, d1_ref, y_ref, x_ref, g0_ref, g1_ref, gate_ref, fg_ref, oc_ref, ol_ref, buf, sems):
    i = pl.program_id(0)

    def issue(step, slot):
        base = step * COMBINE_ROWS

        def body(q, c):
            for u in range(DMA_UNROLL):
                r = q * DMA_UNROLL + u
                pltpu.make_async_copy(y_ref.at[pl.ds(d0_ref[base + r], 1), :], buf.at[slot, 0, pl.ds(r, 1), :],
                                      sems.at[slot]).start(priority=0)
                pltpu.make_async_copy(y_ref.at[pl.ds(d1_ref[base + r], 1), :], buf.at[slot, 1, pl.ds(r, 1), :],
                                      sems.at[slot]).start(priority=1)
            return c

        lax.fori_loop(0, COMBINE_ROWS // DMA_UNROLL, body, 0)

    slot = i % 2

    @pl.when(i == 0)
    def _():
        issue(0, 0)

    @pl.when(i + 1 < pl.num_programs(0))
    def _():
        issue(i + 1, 1 - slot)

    pltpu.make_async_copy(buf.at[slot], buf.at[slot], sems.at[slot]).wait()
    y = g0_ref[:, 0:1] * buf[slot, 0] + g1_ref[:, 0:1] * buf[slot, 1]
    x = x_ref[...] + gate_ref[...] * y
    out = x * lax.rsqrt(jnp.mean(x * x, axis=-1, keepdims=True) + EPS) * fg_ref[...]
    nbc = T_CTX // COMBINE_ROWS

    @pl.when(i < nbc)
    def _():
        oc_ref[...] = out

    @pl.when(i >= nbc)
    def _():
        ol_ref[...] = out


def _moe_combine_norm(ybuf, dest0, dest1, g0, g1, x, mods, slot, final_g):
    nbc = T_CTX // COMBINE_ROWS
    tok = pl.BlockSpec((COMBINE_ROWS, D), lambda i, *p: (i, 0))
    gsp = pl.BlockSpec((COMBINE_ROWS, LANES), lambda i, *p: (i, 0))
    gs = pltpu.PrefetchScalarGridSpec(
        num_scalar_prefetch=2, grid=(T // COMBINE_ROWS,),
        in_specs=[pl.BlockSpec(memory_space=pl.ANY), tok, gsp, gsp,
                  pl.BlockSpec((None, None, 1, D), lambda i, *p: (slot, _cond_of_block(i, COMBINE_ROWS), 0, 0)),
                  pl.BlockSpec((1, D), lambda i, *p: (0, 0))],
        out_specs=[pl.BlockSpec((COMBINE_ROWS, D), lambda i, *p: (jnp.minimum(i, nbc - 1), 0)),
                   pl.BlockSpec((COMBINE_ROWS, D), lambda i, *p: (jnp.maximum(i - nbc, 0), 0))],
        scratch_shapes=[pltpu.VMEM((2, 2, COMBINE_ROWS, D), F32), pltpu.SemaphoreType.DMA((2,))])
    return pl.pallas_call(_combine_body, grid_spec=gs,
                          out_shape=[jax.ShapeDtypeStruct((T_CTX, D), F32), jax.ShapeDtypeStruct((T_LAT, D), F32)],
                          compiler_params=_cparams(("arbitrary",)), name="moe_combine_norm",
                          )(dest0, dest1, ybuf, x, g0, g1, mods, final_g.reshape(1, D))


def _moe_plan(idx, cnt):
    counts = cnt[0, :N_EXP]
    padded = (counts + MOE_TM - 1) // MOE_TM * MOE_TM
    p_ends = jnp.cumsum(padded)
    p_starts = p_ends - padded
    experts = jnp.arange(N_EXP, dtype=I32)[None, :]

    def dest(e, rank):
        return jnp.sum(jnp.where(e[:, None] == experts, p_starts[None, :], 0), axis=1) + rank

    def groups(tm):
        nblk = R_MOE // tm
        blk_start = jnp.arange(nblk, dtype=I32) * tm
        gid = jnp.minimum(jnp.sum((blk_start[:, None] >= p_ends[None, :]).astype(I32), axis=1), N_EXP - 1)
        nact = (p_ends[-1] // tm).astype(I32).reshape(1)
        gid = jnp.where(jnp.arange(nblk) < nact[0], gid, gid[jnp.maximum(nact[0] - 1, 0)])
        chg = jnp.concatenate([jnp.ones((1,), I32), (gid[1:] != gid[:-1]).astype(I32)])
        blk = jnp.arange(nblk, dtype=I32)
        starts = (chg == 1) & (blk < nact[0])
        later = jnp.where(starts[None, :] & (blk[None, :] > blk[:, None]), blk[None, :], nblk)
        nxt = jnp.min(later, axis=1)
        return chg, gid, nact, jnp.where(nxt == nblk, -1, nxt).astype(I32)

    tail = jnp.minimum(p_ends[-1] + jnp.arange(N_EXP, dtype=I32) * MOE_TM, R_MOE - MOE_TM)
    zero_starts = jnp.concatenate([jnp.maximum(p_ends - MOE_TM, 0), tail]).astype(I32)
    return dest(idx[:, 0], idx[:, 2]), dest(idx[:, 1], idx[:, 3]), groups, zero_starts


def _rope_tables():
    quarter = HD // 4
    inv_freq = ROPE_BASE ** (-jnp.arange(quarter, dtype=F32) / quarter)
    t = jnp.arange(L_LAT)
    row = (t // GRID_W).astype(F32)[:, None] * inv_freq
    col = (t % GRID_W).astype(F32)[:, None] * inv_freq
    ang = jnp.concatenate([row, row, col, col], axis=1)
    sign = jnp.tile(jnp.concatenate([-jnp.ones((quarter,), F32), jnp.ones((quarter,), F32)]), 2)
    cos = jnp.concatenate([jnp.ones((T_CTX, HD), F32), jnp.tile(jnp.cos(ang), (B_LAT, 1))], axis=0)
    sin = jnp.concatenate([jnp.zeros((T_CTX, HD), F32), jnp.tile(jnp.sin(ang) * sign, (B_LAT, 1))], axis=0)
    return cos, sin


def kernel(x_prompt, x_sample, cache_k, cache_v, state_rglru, c, c_ctx, w_mod, b_mod, norm_g, final_g, a_w_in, a_w_out, rnn_conv_w, rnn_conv_b, rnn_w_a, rnn_b_a, rnn_w_x, rnn_b_x, rnn_lam, attn_sink, ffn_w1, ffn_w3, ffn_w2, h_w_in, h_w_out, h_conv_w, h_conv_b, hf_w1, hf_b1, hf_w2, hf_b2, hf_freq, hf_w3, h_bias, moe_router, moe_router_b, moe_w_gate, moe_w_up, moe_w_down):
    x = (x_prompt.reshape(T_CTX, D), x_sample.reshape(T_LAT, D))
    cond = jnp.concatenate([c_ctx[None, :], c, jnp.zeros((N_COND - 1 - B_LAT, D), F32)], axis=0)
    mods_all = _ada_params(cond, w_mod, b_mod)
    mods_all = mods_all.reshape(-1, N_COND, 6, D).transpose(0, 2, 1, 3).reshape(-1, 6, N_COND, 1, D)
    cos_t, sin_t = _rope_tables()

    mods = mods_all[0]
    h = _norm(x, norm_g[0, 0], mods=mods, slots=(0, 1), out_dtype=BF16, name="norm_mix0")
    w_in = a_w_in[0]
    mm1 = functools.partial(_mm, [h], [w_in], [0], tm=2 * TM)
    q = mm1(n_cols=D_ATTN, col_off=0, tn=1024, out_dtype=F32, name="proj_q")
    kv = mm1(n_cols=2 * D_KV, col_off=D_ATTN, tn=512, out_dtype=F32, name="proj_kv")
    xy = mm1(n_cols=2 * D_RNN, col_off=D_ATTN + 2 * D_KV, tn=512, out_dtype=F32, name="proj_rnn")
    qb, kb, vb = _qk_prep(q, kv, cos_t, sin_t)
    ck = cache_k[:, 0].reshape(B_LAT, -1, D_KV)
    cv = cache_v[:, 0].reshape(B_LAT, -1, D_KV)
    o_ctx, o_lat = _attention(qb, kb, vb, ck, cv, attn_sink[0])
    rnn_w = (rnn_conv_w[0], rnn_conv_b[0], rnn_w_a[0], rnn_b_a[0], rnn_w_x[0], rnn_b_x[0], rnn_lam[0])
    r_ctx, s_ctx = _rglru(xy, jnp.zeros((B_CTX, 2, D_RNN), F32), B_CTX, L_CTX, 0, *rnn_w, name="rglru_ctx")
    r_lat, _ = _rglru(xy, state_rglru[:, 0], B_LAT, L_LAT, T_CTX, *rnn_w, name="rglru_lat")
    w_out = a_w_out[0]
    x = _mm([(o_ctx, o_lat), (r_ctx, r_lat)], [w_out, w_out], [0, 1], n_cols=D, col_off=0, tn=1024, tm=TM,
            out_dtype=F32, residual=(x, mods, 2), name="proj_out0")
    h = _norm(x, norm_g[0, 1], mods=mods, slots=(3, 4), out_dtype=BF16, name="norm_ffn0")
    hid = _mm([h], [ffn_w1[0]], [0], dual_w=[ffn_w3[0]], n_cols=D_FF, col_off=0, tn=512, tm=2 * TM,
              out_dtype=BF16, name="ffn_up")
    x = _mm([hid], [ffn_w2[0]], [0], n_cols=D, col_off=0, tn=512, tm=TM, out_dtype=F32, residual=(x, mods, 5),
            name="ffn_down")

    mods = mods_all[1]
    h = _norm(x, norm_g[1, 0], mods=mods, slots=(0, 1), out_dtype=BF16, name="norm_mix1")
    u = _mm([h], [h_w_in[0]], [0], n_cols=3 * D, col_off=0, tn=1024, tm=2 * TM, out_dtype=F32, name="hy_in")
    zs = []
    for row0, n_seq, L in ((0, B_CTX, L_CTX), (T_CTX, B_LAT, L_LAT)):
        zb = _hy_prep(u, h_conv_w[0], h_conv_b[0], row0, n_seq * L, L)
        cf, ci = _dft_mats(L)
        filt = _hy_filter(L, hf_w1[0], hf_b1[0], hf_w2[0], hf_b2[0], hf_freq[0], hf_w3[0])
        spec = _hy_spectra(filt, cf, min(HY_FC // 2, L))
        zc = _longconv(zb, n_seq, L, cf, ci, spec, 0)
        zf, zb = _hy_gate(zc, None, h_bias[0, 0], u, h_conv_w[0], h_conv_b[0], 0, row0, L)
        zc = _longconv(zb, n_seq, L, cf, ci, spec, 1)
        (zb,) = _hy_gate(zc, zf, h_bias[0, 1], u, h_conv_w[0], h_conv_b[0], 1, row0, L)
        zs.append(zb)
    x = _mm([tuple(zs)], [h_w_out[0]], [0], n_cols=D, col_off=0, tn=1024, tm=TM, out_dtype=F32,
            residual=(x, mods, 2), name="hy_out")
    h, idx, g0, g1, cnt = _norm(x, norm_g[1, 1], mods=mods, slots=(3, 4), router=(moe_router[0], moe_router_b[0]),
                                out_dtype=BF16, name="norm_moe")
    dest0, dest1, groups, zero_starts = _moe_plan(idx, cnt)
    xs = _moe_dispatch(h, dest0, dest1, zero_starts)
    group = groups(MOE_TM)
    hid = _gmm(xs, [moe_w_gate[0], moe_w_up[0]], group, tn=1024, tm=MOE_TM, out_dtype=BF16, name="moe_up")
    ybuf = _gmm(hid, [moe_w_down[0]], group, tn=512, tm=MOE_TM, out_dtype=F32, name="moe_down")
    y_prompt, y_sample = _moe_combine_norm(ybuf, dest0, dest1, g0, g1, x, mods, 5, final_g)
    y_prompt = y_prompt.reshape(B_CTX, L_CTX, D)
    y_sample = y_sample.reshape(B_LAT, L_LAT, D)
    new_k = kv[:T_CTX, :D_KV].reshape(B_CTX, 1, L_CTX, N_KV, HD)
    new_v = kv[:T_CTX, D_KV:].reshape(B_CTX, 1, L_CTX, N_KV, HD)
    new_s = s_ctx.reshape(B_CTX, 1, 2, D_RNN)
    return (y_prompt, y_sample, new_k, new_v, new_s)
```

```python
import functools
import math

import jax
import jax.numpy as jnp
from jax import lax
from jax.experimental import pallas as pl
from jax.experimental.pallas import tpu as pltpu

F32 = jnp.float32
BF16 = jnp.bfloat16
I32 = jnp.int32

D = 2048
B_CTX, L_CTX = 32, 256
B_LAT, L_LAT = 4, 2048
T_CTX = B_CTX * L_CTX
T_LAT = B_LAT * L_LAT
T = T_CTX + T_LAT
GRID_W = 64
N_HEADS, N_KV, HD = 8, 2, 128
Q_PER_KV = N_HEADS // N_KV
D_ATTN = N_HEADS * HD
D_KV = N_KV * HD
WINDOW = 128
ROPE_BASE = 10000.0
D_RNN = D // 2
RNN_BLOCKS = 8
RNN_W = D_RNN // RNN_BLOCKS
RG_LRU_C = 8.0
HY_EMB = 33
HY_W = 64
HY_MIN_DECAY = math.log(1e-2) / 1.5
HY_MAX_DECAY = math.log(1e-2) / 0.3
D_FF = 5632
N_EXP = 8
D_FFE = 7168
EPS = 1e-6
NEG_INF = -1e30

LANES = 128
SUBLANES = 8
VMEM_LIMIT = 52 * 1024 * 1024
TM = 512
N_COND = 8
MOE_TM = 512
R_MOE = 2 * T + N_EXP * MOE_TM
HY_TD = 512
HY_FC = 1024


def _cparams(sem):
    return pltpu.CompilerParams(dimension_semantics=sem, vmem_limit_bytes=VMEM_LIMIT)


def _cond_of_block(i, tm):
    nb_ctx = T_CTX // tm
    return jnp.where(i < nb_ctx, 0, 1 + (i - nb_ctx) // (L_LAT // tm))


def _rows_value(refs, i, nbc):
    if len(refs) == 1:
        return refs[0][...]
    return jnp.where(i < nbc, refs[0][...], refs[1][...])


def _rows_specs(src, tm, bw, col, clamp=None):
    nbc = T_CTX // tm
    if isinstance(src, tuple):
        return ([pl.BlockSpec((tm, bw), lambda j, i, *p: (jnp.minimum(i, nbc - 1), col(j))),
                 pl.BlockSpec((tm, bw), lambda j, i, *p: (jnp.maximum(i - nbc, 0), col(j)))], list(src))
    row = (lambda i, p: i) if clamp is None else clamp
    return [pl.BlockSpec((tm, bw), lambda j, i, *p: (row(i, p), col(j)))], [src]


def _mm_body(*refs, a_counts, dual, residual, nbc):
    it = iter(refs)
    n_a = len(a_counts)
    a_refs = [[next(it) for _ in range(c)] for c in a_counts]
    w_refs = [[next(it) for _ in range(n_a)] for _ in range(2 if dual else 1)]
    if residual:
        x_refs = [next(it) for _ in range(residual)]
        g_ref = next(it)
    o_ref = next(it)
    wb_refs = [[next(it) for _ in range(n_a)] for _ in range(2 if dual else 1)]
    i = pl.program_id(1)

    @pl.when(i == 0)
    def _():
        for ws, wbs in zip(w_refs, wb_refs):
            for w, wb in zip(ws, wbs):
                wb[...] = w[...].astype(BF16)

    def prod(wbs):
        acc = None
        for a, wb in zip(a_refs, wbs):
            p = jnp.dot(_rows_value(a, i, nbc), wb[...], preferred_element_type=F32)
            acc = p if acc is None else acc + p
        return acc

    y = prod(wb_refs[0])
    if dual:
        y = y * jax.nn.sigmoid(y) * prod(wb_refs[1])
    if residual:
        y = _rows_value(x_refs, i, nbc) + g_ref[...] * y
    o_ref[...] = y.astype(o_ref.dtype)


def _mm(a_list, w_list, w_row_blocks, *, n_cols, col_off, tn, tm, out_dtype, name, dual_w=None, residual=None):
    width = lambda a: (a[0] if isinstance(a, tuple) else a).shape[1]
    ks = [width(a) for a in a_list]
    dual = dual_w is not None
    assert n_cols % tn == 0 and col_off % tn == 0 and T % tm == 0
    cb = col_off // tn
    in_specs, ins, a_counts = [], [], []
    for a, ka in zip(a_list, ks):
        sp, ops = _rows_specs(a, tm, ka, lambda j: 0)
        in_specs += sp
        ins += ops
        a_counts.append(len(ops))
    for ws in ([w_list, dual_w] if dual else [w_list]):
        ins += list(ws)
        in_specs += [pl.BlockSpec((ka, tn), lambda j, i, rb=rb: (rb, cb + j)) for rb, ka in zip(w_row_blocks, ks)]
    n_x = 0
    if residual is not None:
        x, mods, slot = residual
        sp, ops = _rows_specs(x, tm, tn, lambda j: j)
        n_x = len(ops)
        ins += ops + [mods]
        in_specs += sp + [pl.BlockSpec((None, None, 1, tn), lambda j, i: (slot, _cond_of_block(i, tm), 0, j))]
    scratch = [pltpu.VMEM((ka, tn), BF16) for ka in ks] * (2 if dual else 1)
    body = functools.partial(_mm_body, a_counts=tuple(a_counts), dual=dual, residual=n_x, nbc=T_CTX // tm)
    return pl.pallas_call(body, grid=(n_cols // tn, T // tm), in_specs=in_specs,
                          out_specs=pl.BlockSpec((tm, tn), lambda j, i: (i, j)), scratch_shapes=scratch,
                          out_shape=jax.ShapeDtypeStruct((T, n_cols), out_dtype),
                          compiler_params=_cparams(("arbitrary", "arbitrary")), name=name)(*ins)


def _gmm_body(chg_ref, gid_ref, nact_ref, nxt_ref, a_ref, *rest, n_w, tn):
    w_refs, (o_ref, stage, wb, sem) = rest[:n_w], rest[n_w:]
    j, i = pl.program_id(0), pl.program_id(1)
    nact = nact_ref[0]

    def fetch(jj, ii):
        cols = pl.ds(pl.multiple_of(jj * tn, tn), tn)
        for m, w in enumerate(w_refs):
            pltpu.make_async_copy(w.at[gid_ref[ii], :, cols], stage.at[m], sem).start()

    refresh = (i < nact) & (chg_ref[i] != 0)

    @pl.when(refresh & (j == 0) & (i == 0))
    def _():
        fetch(0, 0)

    @pl.when(refresh)
    def _():
        for m, w in enumerate(w_refs):
            pltpu.make_async_copy(w.at[0, :, pl.ds(0, tn)], stage.at[m], sem).wait()
        for m in range(n_w):
            wb[m] = stage[m].astype(BF16)
        nxt = nxt_ref[i]

        @pl.when(nxt >= 0)
        def _():
            fetch(j, nxt)

        @pl.when((nxt < 0) & (j + 1 < pl.num_programs(0)))
        def _():
            fetch(j + 1, 0)

    @pl.when(i < nact)
    def _():
        a = a_ref[...]
        y = jnp.dot(a, wb[0], preferred_element_type=F32)
        if n_w == 2:
            y = y * jax.nn.sigmoid(y) * jnp.dot(a, wb[1], preferred_element_type=F32)
        o_ref[...] = y.astype(o_ref.dtype)

    @pl.when(i >= nact)
    def _():
        o_ref[...] = jnp.zeros_like(o_ref)


def _gmm(a, w_list, group, *, tn, tm, out_dtype, name):
    K = a.shape[1]
    n = w_list[0].shape[2]
    n_w = len(w_list)
    rowblk = lambda i, p: jnp.maximum(jnp.minimum(i, p[2][0] - 1), 0)
    gs = pltpu.PrefetchScalarGridSpec(
        num_scalar_prefetch=4, grid=(n // tn, R_MOE // tm),
        in_specs=[pl.BlockSpec((tm, K), lambda j, i, *p: (rowblk(i, p), 0))]
        + [pl.BlockSpec(memory_space=pl.ANY)] * n_w,
        out_specs=pl.BlockSpec((tm, tn), lambda j, i, *p: (i, j)),
        scratch_shapes=[pltpu.VMEM((n_w, K, tn), F32), pltpu.VMEM((n_w, K, tn), BF16), pltpu.SemaphoreType.DMA(())])
    return pl.pallas_call(functools.partial(_gmm_body, n_w=n_w, tn=tn), grid_spec=gs,
                          out_shape=jax.ShapeDtypeStruct((R_MOE, n), out_dtype),
                          compiler_params=_cparams(("arbitrary", "arbitrary")), name=name)(*group, a, *w_list)


def _ada_body(c_ref, w_ref, b_ref, o_ref):
    c = c_ref[...]
    s = (c * jax.nn.sigmoid(c)).astype(BF16)
    o_ref[...] = jnp.dot(s, w_ref[...].astype(BF16), preferred_element_type=F32) + b_ref[...]


def _ada_params(cond, w_mod, b_mod):
    depth = w_mod.shape[0]
    tn = 1024
    return pl.pallas_call(
        _ada_body, grid=(depth, 6 * D // tn),
        in_specs=[pl.BlockSpec((N_COND, D), lambda l, j: (0, 0)),
                  pl.BlockSpec((None, D, tn), lambda l, j: (l, 0, j)),
                  pl.BlockSpec((None, 1, tn), lambda l, j: (l, 0, j))],
        out_specs=pl.BlockSpec((None, N_COND, tn), lambda l, j: (l, 0, j)),
        out_shape=jax.ShapeDtypeStruct((depth, N_COND, 6 * D), F32),
        compiler_params=_cparams(("arbitrary", "arbitrary")), name="ada_params",
    )(cond, w_mod, b_mod.reshape(depth, 1, 6 * D))


def _norm_body(*refs, n_x, modulate, router):
    it = iter(refs)
    x_refs = [next(it) for _ in range(n_x)]
    g_ref = next(it)
    if modulate:
        sh_ref, sc_ref = next(it), next(it)
    if router:
        wr_ref, br_ref = next(it), next(it)
    o_ref = next(it)
    pid = pl.program_id(1)
    x = _rows_value(x_refs, pid, T_CTX // TM)
    y = x * lax.rsqrt(jnp.mean(x * x, axis=-1, keepdims=True) + EPS) * g_ref[...]
    if modulate:
        y = y * (1.0 + sc_ref[...]) + sh_ref[...]
    o_ref[...] = y.astype(o_ref.dtype)
    if router:
        idx_ref, g0_ref, g1_ref, cnt_ref, carry_ref = next(it), next(it), next(it), next(it), next(it)
        logits = jnp.dot(y, wr_ref[...], preferred_element_type=F32, precision=lax.Precision.HIGHEST) + br_ref[...]
        lane = lax.broadcasted_iota(I32, logits.shape, 1)
        logits = jnp.where(lane < N_EXP, logits, -jnp.inf)
        lanef = lane.astype(F32)
        m1 = jnp.max(logits, axis=-1, keepdims=True)
        i1 = jnp.min(jnp.where(logits == m1, lanef, float(LANES)), axis=-1, keepdims=True)
        rest = jnp.where(lanef == i1, -jnp.inf, logits)
        m2 = jnp.max(rest, axis=-1, keepdims=True)
        i2 = jnp.min(jnp.where(rest == m2, lanef, float(LANES)), axis=-1, keepdims=True)
        e21 = jnp.exp(m2 - m1)
        gate1 = 1.0 / (1.0 + e21)
        g0_ref[...] = jnp.broadcast_to(gate1, logits.shape)
        g1_ref[...] = jnp.broadcast_to(e21 * gate1, logits.shape)

        @pl.when(pid == 0)
        def _():
            carry_ref[...] = jnp.zeros_like(carry_ref)

        hot1 = jnp.where(lanef == i1, 1.0, 0.0)
        hot2 = jnp.where(lanef == i2, 1.0, 0.0)
        n = logits.shape[0]
        tri = jnp.where(lax.broadcasted_iota(I32, (n, n), 1) < lax.broadcasted_iota(I32, (n, n), 0), 1.0, 0.0)
        tri = tri.astype(BF16)
        before1 = jnp.dot(tri, hot1.astype(BF16), preferred_element_type=F32)
        before2 = jnp.dot(tri, hot2.astype(BF16), preferred_element_type=F32)
        tot1 = jnp.sum(hot1, axis=0, keepdims=True)
        tot2 = jnp.sum(hot2, axis=0, keepdims=True)
        carry = carry_ref[...]
        rank1 = jnp.sum(hot1 * (before1 + carry), axis=-1, keepdims=True)
        rank2 = jnp.sum(hot2 * (before2 + carry + tot1), axis=-1, keepdims=True)
        carry = carry + tot1 + tot2
        carry_ref[...] = carry
        cnt_ref[...] = jnp.broadcast_to(carry, cnt_ref.shape).astype(I32)
        packed = jnp.where(lane == 0, i1, jnp.where(lane == 1, i2, jnp.where(lane == 2, rank1,
                                                                             jnp.where(lane == 3, rank2, 0.0))))
        idx_ref[...] = packed.astype(I32)


def _norm(x, g, *, mods=None, slots=None, router=None, out_dtype, name, row0=0, rows=None):
    pair = isinstance(x, tuple)
    rows = (T if pair else x.shape[0]) if rows is None else rows
    rb0 = row0 // TM
    modulate = mods is not None
    in_specs, ins = _rows_specs(x, TM, D, lambda j: 0, clamp=lambda i, p: rb0 + i)
    n_x = len(ins)
    ins.append(g.reshape(1, D))
    in_specs.append(pl.BlockSpec((1, D), lambda j, i: (0, 0)))
    if modulate:
        for slot in slots:
            ins.append(mods)
            in_specs.append(pl.BlockSpec((None, None, 1, D),
                                         lambda j, i, slot=slot: (slot, _cond_of_block(i, TM), 0, 0)))
    out_shape = [jax.ShapeDtypeStruct((rows, D), out_dtype)]
    out_specs = [pl.BlockSpec((TM, D), lambda j, i: (i, 0))]
    scratch = []
    if router is not None:
        w_r, b_r = router
        ins += [jnp.pad(w_r, ((0, 0), (0, LANES - N_EXP))), jnp.pad(b_r, (0, LANES - N_EXP)).reshape(1, LANES)]
        in_specs += [pl.BlockSpec((D, LANES), lambda j, i: (0, 0)), pl.BlockSpec((1, LANES), lambda j, i: (0, 0))]
        out_shape += [jax.ShapeDtypeStruct((rows, LANES), I32), jax.ShapeDtypeStruct((rows, LANES), F32),
                      jax.ShapeDtypeStruct((rows, LANES), F32), jax.ShapeDtypeStruct((SUBLANES, LANES), I32)]
        out_specs += [pl.BlockSpec((TM, LANES), lambda j, i: (i, 0))] * 3
        out_specs += [pl.BlockSpec((SUBLANES, LANES), lambda j, i: (0, 0))]
        scratch = [pltpu.VMEM((1, LANES), F32)]
    body = functools.partial(_norm_body, n_x=n_x, modulate=modulate, router=router is not None)
    res = pl.pallas_call(body, grid=(1, rows // TM), in_specs=in_specs, out_specs=out_specs, out_shape=out_shape,
                         scratch_shapes=scratch, compiler_params=_cparams(("arbitrary", "arbitrary")), name=name)(*ins)
    return res if router is not None else res[0]


def _swap32(x):
    up = jnp.concatenate([x[:, 32:], x[:, :32]], axis=1)
    down = jnp.concatenate([x[:, 96:], x[:, :96]], axis=1)
    lane = lax.broadcasted_iota(I32, x.shape, 1)
    return jnp.where((lane % 64) < 32, up, down)


def _qkprep_body(q_ref, kv_ref, cos_ref, sin_ref, qo_ref, ko_ref, vo_ref):
    cos, sin = cos_ref[...], sin_ref[...]
    for h in range(N_HEADS):
        x = q_ref[:, h * HD:(h + 1) * HD]
        qo_ref[:, h * HD:(h + 1) * HD] = (x * cos + _swap32(x) * sin).astype(BF16)
    for h in range(N_KV):
        x = kv_ref[:, h * HD:(h + 1) * HD]
        ko_ref[:, h * HD:(h + 1) * HD] = (x * cos + _swap32(x) * sin).astype(BF16)
    vo_ref[...] = kv_ref[:, D_KV:].astype(BF16)


def _qk_prep(q, kv, cos_t, sin_t):
    return pl.pallas_call(
        _qkprep_body, grid=(T // TM,),
        in_specs=[pl.BlockSpec((TM, D_ATTN), lambda i: (i, 0)), pl.BlockSpec((TM, 2 * D_KV), lambda i: (i, 0)),
                  pl.BlockSpec((TM, HD), lambda i: (i, 0)), pl.BlockSpec((TM, HD), lambda i: (i, 0))],
        out_specs=[pl.BlockSpec((TM, D_ATTN), lambda i: (i, 0)), pl.BlockSpec((TM, D_KV), lambda i: (i, 0)),
                   pl.BlockSpec((TM, D_KV), lambda i: (i, 0))],
        out_shape=[jax.ShapeDtypeStruct((T, D_ATTN), BF16), jax.ShapeDtypeStruct((T, D_KV), BF16),
                   jax.ShapeDtypeStruct((T, D_KV), BF16)],
        compiler_params=_cparams(("arbitrary",)), name="qk_prep")(q, kv, cos_t, sin_t)


def _attn_core(q_ref, o_ref, sink_ref, kvh, kall, vall, mask):
    scale = HD ** -0.5
    for g in range(Q_PER_KV):
        head = kvh * Q_PER_KV + g
        qh = q_ref[:, head * HD:(head + 1) * HD]
        s = lax.dot_general(qh, kall, (((1,), (1,)), ((), ())), preferred_element_type=F32) * scale
        if mask is not None:
            s = jnp.where(mask, s, NEG_INF)
        sk = sink_ref[head]
        m = jnp.maximum(jnp.max(s, axis=-1, keepdims=True), sk)
        p = jnp.exp(s - m)
        denom = jnp.sum(p, axis=-1, keepdims=True) + jnp.exp(sk - m)
        o = jnp.dot(p.astype(BF16), vall, preferred_element_type=F32) / denom
        o_ref[:, head * HD:(head + 1) * HD] = o.astype(o_ref.dtype)


def _attn_ctx_body(sink_ref, q_ref, k_ref, v_ref, o_ref):
    for kvh in range(N_KV):
        cols = slice(kvh * HD, (kvh + 1) * HD)
        _attn_core(q_ref, o_ref, sink_ref, kvh, k_ref[:, cols], v_ref[:, cols], None)


def _attn_lat_body(sink_ref, q_ref, kp_ref, kc_ref, kn_ref, vp_ref, vc_ref, vn_ref, ck_ref, cv_ref, o_ref):
    qb = pl.program_id(1)
    nk = 3 * WINDOW + ck_ref.shape[0]
    qpos = qb * WINDOW + lax.broadcasted_iota(I32, (WINDOW, nk), 0)
    col = lax.broadcasted_iota(I32, (WINDOW, nk), 1)
    kpos = (qb - 1) * WINDOW + col
    in_win = (jnp.abs(kpos - qpos) <= WINDOW) & (kpos >= 0) & (kpos < L_LAT)
    mask = in_win | (col >= 3 * WINDOW)
    for kvh in range(N_KV):
        cols = slice(kvh * HD, (kvh + 1) * HD)
        kall = jnp.concatenate([kp_ref[:, cols], kc_ref[:, cols], kn_ref[:, cols], ck_ref[:, cols].astype(BF16)], axis=0)
        vall = jnp.concatenate([vp_ref[:, cols], vc_ref[:, cols], vn_ref[:, cols], cv_ref[:, cols].astype(BF16)], axis=0)
        _attn_core(q_ref, o_ref, sink_ref, kvh, kall, vall, mask)


def _attention(qb, kb, vb, cache_k, cache_v, sink):
    smem = pl.BlockSpec(memory_space=pltpu.SMEM)
    o_ctx = pl.pallas_call(
        _attn_ctx_body, grid=(B_CTX,),
        in_specs=[smem, pl.BlockSpec((L_CTX, D_ATTN), lambda b: (b, 0)),
                  pl.BlockSpec((L_CTX, D_KV), lambda b: (b, 0)), pl.BlockSpec((L_CTX, D_KV), lambda b: (b, 0))],
        out_specs=pl.BlockSpec((L_CTX, D_ATTN), lambda b: (b, 0)),
        out_shape=jax.ShapeDtypeStruct((T_CTX, D_ATTN), BF16),
        compiler_params=_cparams(("arbitrary",)), name="attn_ctx")(sink, qb, kb, vb)
    nb = L_LAT // WINDOW
    base = T_CTX // WINDOW
    cur = lambda b, i: (base + b * nb + i, 0)
    prv = lambda b, i: (base + b * nb + jnp.maximum(i - 1, 0), 0)
    nxt = lambda b, i: (base + b * nb + jnp.minimum(i + 1, nb - 1), 0)
    blk = lambda f: pl.BlockSpec((WINDOW, D_KV), f)
    cspec = pl.BlockSpec((None, cache_k.shape[1], D_KV), lambda b, i: (b, 0, 0))
    o_lat = pl.pallas_call(
        _attn_lat_body, grid=(B_LAT, nb),
        in_specs=[smem, pl.BlockSpec((WINDOW, D_ATTN), cur), blk(prv), blk(cur), blk(nxt), blk(prv), blk(cur), blk(nxt),
                  cspec, cspec],
        out_specs=pl.BlockSpec((WINDOW, D_ATTN), lambda b, i: (b * nb + i, 0)),
        out_shape=jax.ShapeDtypeStruct((T_LAT, D_ATTN), BF16),
        compiler_params=_cparams(("arbitrary", "arbitrary")), name="attn_lat",
    )(sink, qb, kb, kb, kb, vb, vb, vb, cache_k, cache_v)
    return o_ctx, o_lat


def _scan_pitch(L):
    return L // SUBLANES + 4


def _rglru_body(x_ref, y_ref, h0_ref, cw_ref, cb_ref, wa_ref, ba_ref, wx_ref, bx_ref, lam_ref,
                o_ref, fin_ref, a_s, b_s, p_s, h_s):
    L = x_ref.shape[0]
    lc = L // SUBLANES
    pitch = _scan_pitch(L)
    x = x_ref[...]
    row = lax.broadcasted_iota(I32, x.shape, 0)
    xc = cb_ref[...] + x * cw_ref[2:3, :]
    xc = xc + jnp.where(row >= 2, pltpu.roll(x, 2, 0), 0.0) * cw_ref[0:1, :]
    xc = xc + jnp.where(row >= 1, pltpu.roll(x, 1, 0), 0.0) * cw_ref[1:2, :]
    xc = xc + jnp.where(row < L - 1, pltpu.roll(x, L - 1, 0), 0.0) * cw_ref[3:4, :]
    xcb = xc.astype(BF16)
    for d in range(2):
        ga = jnp.dot(xcb, wa_ref[d].astype(BF16), preferred_element_type=F32) + ba_ref[d:d + 1, :]
        gx = jnp.dot(xcb, wx_ref[d].astype(BF16), preferred_element_type=F32) + bx_ref[d:d + 1, :]
        sigmoid = lambda v: 0.5 * jnp.tanh(0.5 * v) + 0.5
        log_a = -RG_LRU_C * sigmoid(ga) * jax.nn.softplus(-lam_ref[d:d + 1, :])
        a = jnp.exp(log_a)
        b = jnp.sqrt(-jnp.tanh(log_a) * (1.0 + a * a)) * sigmoid(gx) * xc
        for s in range(SUBLANES):
            a_s[d, s * pitch:s * pitch + lc, :] = a[s * lc:(s + 1) * lc]
            b_s[d, s * pitch:s * pitch + lc, :] = b[s * lc:(s + 1) * lc]

    def step(i, carry):
        out = []
        for d, t in ((0, i), (1, lc - 1 - i)):
            rows = pl.ds(t, SUBLANES, stride=pitch)
            a = a_s[d, rows, :]
            h = a * carry[2 * d] + b_s[d, rows, :]
            p = a * carry[2 * d + 1]
            h_s[d, rows, :] = h
            p_s[d, rows, :] = p
            out += [h, p]
        return tuple(out)

    zero = jnp.zeros((SUBLANES, RNN_W), F32)
    one = jnp.ones((SUBLANES, RNN_W), F32)
    hf, pf, hb, pb = lax.fori_loop(0, lc, step, (zero, one, zero, one), unroll=4)

    r8 = lax.broadcasted_iota(I32, (SUBLANES, RNN_W), 0)

    def chunk_carry(p, h, h0, reverse):
        for k in (1, 2, 4):
            sh = SUBLANES - k if reverse else k
            m = (r8 < SUBLANES - k) if reverse else (r8 >= k)
            h = jnp.where(m, p * pltpu.roll(h, sh, 0) + h, h)
            p = jnp.where(m, p * pltpu.roll(p, sh, 0), p)
        h0 = jnp.broadcast_to(h0, (SUBLANES, RNN_W))
        state = p * h0 + h
        if reverse:
            return state, jnp.where(r8 < SUBLANES - 1, pltpu.roll(state, SUBLANES - 1, 0), h0)
        return state, jnp.where(r8 >= 1, pltpu.roll(state, 1, 0), h0)

    sf, cf = chunk_carry(pf, hf, h0_ref[0:1, :], False)
    sb, cb = chunk_carry(pb, hb, h0_ref[1:2, :], True)

    for s in range(SUBLANES):
        rows = slice(s * pitch, s * pitch + lc)
        hsum = (h_s[0, rows, :] + p_s[0, rows, :] * cf[s:s + 1, :]) + (h_s[1, rows, :] + p_s[1, rows, :] * cb[s:s + 1, :])
        o_ref[s * lc:(s + 1) * lc, :] = (hsum * jax.nn.gelu(y_ref[s * lc:(s + 1) * lc, :])).astype(o_ref.dtype)
    fin_ref[0:1, :] = sf[SUBLANES - 1:SUBLANES, :]
    fin_ref[1:2, :] = sb[0:1, :]


def _rglru(xy, h0, n_seq, L, row0, conv_w, conv_b, w_a, b_a, w_x, b_x, lam, name):
    rb0 = row0 // L
    nb = RNN_BLOCKS
    vec = lambda r: pl.BlockSpec((r, RNN_W), lambda b, n: (0, n))
    wsp = pl.BlockSpec((2, None, RNN_W, RNN_W), lambda b, n: (0, n, 0, 0))
    return pl.pallas_call(
        _rglru_body, grid=(n_seq, nb),
        in_specs=[pl.BlockSpec((L, RNN_W), lambda b, n: (rb0 + b, n)),
                  pl.BlockSpec((L, RNN_W), lambda b, n: (rb0 + b, nb + n)),
                  pl.BlockSpec((None, 2, RNN_W), lambda b, n: (b, 0, n)),
                  vec(4), vec(1), wsp, vec(2), wsp, vec(2), vec(2)],
        out_specs=[pl.BlockSpec((L, RNN_W), lambda b, n: (b, n)),
                   pl.BlockSpec((None, 2, RNN_W), lambda b, n: (b, 0, n))],
        out_shape=[jax.ShapeDtypeStruct((n_seq * L, D_RNN), BF16), jax.ShapeDtypeStruct((n_seq, 2, D_RNN), F32)],
        scratch_shapes=[pltpu.VMEM((2, SUBLANES * _scan_pitch(L), RNN_W), F32)] * 4,
        compiler_params=_cparams(("arbitrary", "arbitrary")), name=name,
    )(xy, xy, h0, conv_w, conv_b.reshape(1, D_RNN), w_a, b_a, w_x, b_x, lam)


def _hy_filter_body(fv_ref, w1_ref, b1_ref, w2_ref, b2_ref, fr_ref, w3_ref, dl_ref, o_ref, hid_s, tt_s, *, L):
    hp = lax.Precision.HIGHEST
    rowi = lax.broadcasted_iota(I32, (L, LANES), 0)
    lane = lax.broadcasted_iota(I32, (L, LANES), 1)

    def features(pos):
        posf = pos.astype(F32)
        tt = posf / (L - 1)
        ang = fv_ref[...] * (2.0 * math.pi * posf / L)
        z = jnp.where(lane == 0, tt, jnp.where(lane <= 16, jnp.cos(ang), jnp.where(lane <= 32, -jnp.sin(ang), 0.0)))
        return z, tt[:, 0:1]

    is_bwd = pl.program_id(0) == 1

    @pl.when(pl.program_id(1) == 0)
    def _():
        z, tt = features(jnp.where(is_bwd, L - rowi, rowi))
        h = jnp.sin(fr_ref[0:1, :] * (jnp.dot(z, w1_ref[...], preferred_element_type=F32, precision=hp) + b1_ref[...]))
        h = jnp.sin(fr_ref[1:2, :] * (jnp.dot(h, w2_ref[...], preferred_element_type=F32, precision=hp) + b2_ref[...]))
        hid_s[...] = h
        tt_s[...] = jnp.broadcast_to(tt, tt_s.shape)

    filt = jnp.dot(hid_s[...], w3_ref[...], preferred_element_type=F32, precision=hp)
    filt = filt * jnp.exp(-tt_s[:, 0:1] * dl_ref[...])
    dead = is_bwd & (lax.broadcasted_iota(I32, filt.shape, 0) == 0)
    o_ref[...] = jnp.where(dead, 0.0, filt).astype(o_ref.dtype)


def _hy_filter(L, f_w1, f_b1, f_w2, f_b2, f_freq, f_w3):
    bands = (HY_EMB - 1) // 2
    f = jnp.linspace(1e-4, bands - 1, bands, dtype=F32)
    fv = jnp.zeros((LANES,), F32).at[1:1 + bands].set(f).at[1 + bands:1 + 2 * bands].set(f).reshape(1, LANES)
    padw = lambda w, r, c: jnp.pad(w.astype(F32), ((0, r - w.shape[0]), (0, c - w.shape[1])))
    padv = lambda v: jnp.pad(v.astype(F32), (0, LANES - v.shape[0])).reshape(1, LANES)
    w1, w2 = padw(f_w1, LANES, LANES), padw(f_w2, LANES, LANES)
    w3 = padw(f_w3, LANES, f_w3.shape[1])
    fr = jnp.pad(f_freq.astype(F32), ((0, 0), (0, LANES - HY_W)))
    deltas = jnp.abs(jnp.linspace(HY_MIN_DECAY, HY_MAX_DECAY, D, dtype=F32)).reshape(1, D)
    tn = 1024
    per = D // tn
    full = lambda r: pl.BlockSpec((r, LANES), lambda d, j: (0, 0))
    col = lambda d, j: ((j // per) * 2 + d) * per + j % per
    return pl.pallas_call(
        functools.partial(_hy_filter_body, L=L), grid=(2, 2 * per),
        in_specs=[full(1), full(LANES), full(1), full(LANES), full(1), full(2),
                  pl.BlockSpec((LANES, tn), lambda d, j: (0, col(d, j))),
                  pl.BlockSpec((1, tn), lambda d, j: (0, j % per))],
        out_specs=pl.BlockSpec((L, tn), lambda d, j: (0, col(d, j))),
        out_shape=jax.ShapeDtypeStruct((L, 4 * D), BF16),
        scratch_shapes=[pltpu.VMEM((L, LANES), F32), pltpu.VMEM((L, LANES), F32)],
        compiler_params=_cparams(("arbitrary", "arbitrary")), name=f"hy_filter_{L}",
    )(fv, w1, padv(f_b1), w2, padv(f_b2), fr, w3, deltas)


DFT_RB = 64


def _dft_body(ca_ref, sa_ref, cb_ref, sb_ref, cf_ref, ci_ref, *, L):
    n = 2 * L
    ca, sa, cb, sb = ca_ref[...], sa_ref[...], cb_ref[...], sb_ref[...]
    cos = ca * cb - sa * sb
    sin = sa * cb + ca * sb
    r = pl.program_id(0) * DFT_RB + lax.broadcasted_iota(I32, cos.shape, 0)
    j = lax.broadcasted_iota(I32, cos.shape, 1)
    alt_j = jnp.where(j % 2 == 0, 1.0, -1.0)
    alt_r = jnp.where(r % 2 == 0, 1.0, -1.0)
    cf_ref[0] = cos.astype(BF16)
    cf_ref[1] = jnp.where(r == 0, alt_j, -sin).astype(BF16)
    w = jnp.where(j == 0, 1.0, 2.0) / n
    ci_ref[0] = (w * cos).astype(BF16)
    ci_ref[1] = jnp.where(j == 0, alt_r / n, -w * sin).astype(BF16)


def _dft_mats(L):
    n = 2 * L
    j = jnp.arange(L, dtype=I32)[None, :]
    ang = lambda r: ((r[:, None] * j) % n).astype(F32) * (2.0 * math.pi / n)
    ang_a = ang(jnp.arange(L // DFT_RB, dtype=I32) * DFT_RB)
    ang_b = ang(jnp.arange(DFT_RB, dtype=I32))
    row = pl.BlockSpec((None, 1, L), lambda a: (a, 0, 0))
    full = pl.BlockSpec((DFT_RB, L), lambda a: (0, 0))
    out = pl.BlockSpec((2, DFT_RB, L), lambda a: (0, a, 0))
    return pl.pallas_call(
        functools.partial(_dft_body, L=L), grid=(L // DFT_RB,), in_specs=[row, row, full, full],
        out_specs=[out, out], out_shape=[jax.ShapeDtypeStruct((2, L, L), BF16)] * 2,
        compiler_params=_cparams(("arbitrary",)), name=f"dft_tables_{L}",
    )(jnp.cos(ang_a)[:, None, :], jnp.sin(ang_a)[:, None, :], jnp.cos(ang_b), jnp.sin(ang_b))


def _spec_body(cf_ref, c1_ref, c2_ref, o_ref):
    f = pl.program_id(1)
    h = cf_ref.shape[1]
    k = f * h + lax.broadcasted_iota(I32, (h, c1_ref.shape[1]), 0)
    odd = k % 2 == 1
    for part in range(2):
        z1 = jnp.dot(cf_ref[part], c1_ref[...], preferred_element_type=F32)
        z2 = jnp.dot(cf_ref[part], c2_ref[...], preferred_element_type=F32)
        flip = odd if part == 0 else odd & (k != 0)
        o_ref[part * h:(part + 1) * h, :] = z1 + jnp.where(flip, -z2, z2)


def _hy_spectra(filt, cf, h):
    L = cf.shape[1]
    nf = L // h
    td = HY_TD
    per = D // td
    return pl.pallas_call(
        _spec_body, grid=(2 * per, nf),
        in_specs=[pl.BlockSpec((2, h, L), lambda c, f: (0, f, 0)),
                  pl.BlockSpec((L, td), lambda c, f: (0, (c // per) * 2 * per + c % per)),
                  pl.BlockSpec((L, td), lambda c, f: (0, (c // per) * 2 * per + per + c % per))],
        out_specs=pl.BlockSpec((None, 2 * h, td), lambda c, f: (f, 0, c)),
        out_shape=jax.ShapeDtypeStruct((nf, 2 * h, 2 * D), F32),
        compiler_params=_cparams(("arbitrary", "arbitrary")), name=f"hy_spectra_{L}")(cf, filt, filt)


HY_TC = 256
HY_TR = 2048


def _hy_conv(u_ref, cw_ref, cb_ref, L):
    u = u_ref[...]
    rows = u.shape[0]
    pos = lax.broadcasted_iota(I32, u.shape, 0) % L
    uc = cb_ref[...] + u * cw_ref[1:2, :]
    uc = uc + jnp.where(pos >= 1, pltpu.roll(u, 1, 0), 0.0) * cw_ref[0:1, :]
    return uc + jnp.where(pos < L - 1, pltpu.roll(u, rows - 1, 0), 0.0) * cw_ref[2:3, :]


def _hy_third_specs(third, row0, L):
    tr = max(L, HY_TR)
    per = D // HY_TC
    col = lambda i, c: third * per + c
    return [pl.BlockSpec((tr, HY_TC), lambda i, c: (row0 // tr + i, col(i, c))),
            pl.BlockSpec((3, HY_TC), lambda i, c: (0, col(i, c))),
            pl.BlockSpec((1, HY_TC), lambda i, c: (0, col(i, c)))]


def _hy_prep_body(u_ref, cw_ref, cb_ref, zb_ref, *, L):
    zb_ref[...] = _hy_conv(u_ref, cw_ref, cb_ref, L).astype(BF16)


def _hy_prep(u, conv_w, conv_b, row0, rows, L):
    tr = max(L, HY_TR)
    return pl.pallas_call(
        functools.partial(_hy_prep_body, L=L), grid=(rows // tr, D // HY_TC),
        in_specs=_hy_third_specs(0, row0, L), out_specs=pl.BlockSpec((tr, HY_TC), lambda i, c: (i, c)),
        out_shape=jax.ShapeDtypeStruct((rows, D), BF16),
        compiler_params=_cparams(("arbitrary", "arbitrary")), name=f"hy_prep_{L}",
    )(u, conv_w, conv_b.reshape(1, 3 * D))


def _longconv_body(z_ref, cf_ref, ci_ref, s_ref, o_ref, acc_ref):
    f = pl.program_id(2)
    h = cf_ref.shape[1]
    z = z_ref[...]
    zre = jnp.dot(cf_ref[0], z, preferred_element_type=F32)
    zim = jnp.dot(cf_ref[1], z, preferred_element_type=F32)
    sre, sim = s_ref[:h, :], s_ref[h:, :]
    first = (lax.broadcasted_iota(I32, zre.shape, 0) == 0) & (f == 0)
    yre = zre * sre - jnp.where(first, 0.0, zim * sim)
    yim = jnp.where(first, zim * sim, zre * sim + zim * sre)
    contrib = (jnp.dot(ci_ref[0], yre.astype(BF16), preferred_element_type=F32)
               + jnp.dot(ci_ref[1], yim.astype(BF16), preferred_element_type=F32))

    @pl.when(f == 0)
    def _():
        acc_ref[...] = contrib

    @pl.when(f > 0)
    def _():
        acc_ref[...] += contrib

    @pl.when(f == pl.num_programs(2) - 1)
    def _():
        o_ref[...] = acc_ref[...]


def _longconv(z, n_seq, L, cf, ci, spec, order):
    nf, fc, _ = spec.shape
    h = fc // 2
    td = HY_TD if L > TM else D
    per = D // td
    return pl.pallas_call(
        _longconv_body, grid=(n_seq, per, nf),
        in_specs=[pl.BlockSpec((L, td), lambda b, c, f: (b, c)),
                  pl.BlockSpec((2, h, L), lambda b, c, f: (0, f, 0)),
                  pl.BlockSpec((2, L, h), lambda b, c, f: (0, 0, f)),
                  pl.BlockSpec((None, fc, td), lambda b, c, f: (f, 0, order * per + c))],
        out_specs=pl.BlockSpec((L, td), lambda b, c, f: (b, c)),
        out_shape=jax.ShapeDtypeStruct((n_seq * L, D), F32),
        scratch_shapes=[pltpu.VMEM((L, td), F32)],
        compiler_params=_cparams(("arbitrary", "arbitrary", "arbitrary")), name=f"longconv_{L}_{order}",
    )(z, cf, ci, spec)


def _hy_gate_body(zc_ref, b_ref, *refs, L, first):
    if first:
        z = _hy_conv(*refs[0:3], L)
        gate_refs, out_refs = refs[3:6], refs[6:]
    else:
        z = refs[0][...]
        gate_refs, out_refs = refs[1:4], refs[4:]
    z = _hy_conv(*gate_refs, L) * (zc_ref[...] + z * b_ref[...])
    for o_ref in out_refs:
        o_ref[...] = z.astype(o_ref.dtype)


def _hy_gate(zc, z, bias, u, conv_w, conv_b, order, row0, L):
    rows = zc.shape[0]
    tr = max(L, HY_TR)
    blk = pl.BlockSpec((tr, HY_TC), lambda i, c: (i, c))
    first = z is None
    uargs = (u, conv_w, conv_b.reshape(1, 3 * D))
    ins = [zc, bias.reshape(1, D)] + (list(uargs) if first else [z]) + list(uargs)
    in_specs = ([blk, pl.BlockSpec((1, HY_TC), lambda i, c: (0, c))]
                + (_hy_third_specs(0, row0, L) if first else [blk]) + _hy_third_specs(order + 1, row0, L))
    dts = ([F32] if first else []) + [BF16]
    return pl.pallas_call(
        functools.partial(_hy_gate_body, L=L, first=first), grid=(rows // tr, D // HY_TC),
        in_specs=in_specs, out_specs=[blk] * len(dts),
        out_shape=[jax.ShapeDtypeStruct((rows, D), dt) for dt in dts],
        compiler_params=_cparams(("arbitrary", "arbitrary")), name=f"hy_gate_{L}_{order}",
    )(*ins)


DISPATCH_ROWS = 256
COMBINE_ROWS = 128
DMA_UNROLL = 8


def _dispatch_body(d0_ref, d1_ref, zs_ref, h_ref, o_ref, zbuf, sem, zsem):
    base = pl.program_id(0) * DISPATCH_ROWS

    @pl.when(pl.program_id(0) == 0)
    def _():
        zbuf[...] = jnp.zeros_like(zbuf)
        for k in range(2 * N_EXP):
            fill = pltpu.make_async_copy(zbuf, o_ref.at[pl.ds(zs_ref[k], MOE_TM)], zsem)
            fill.start()
            fill.wait()

    def issue(q, c):
        for u in range(DMA_UNROLL):
            r = q * DMA_UNROLL + u
            pltpu.make_async_copy(h_ref.at[r], o_ref.at[d0_ref[base + r]], sem).start(priority=0)
            pltpu.make_async_copy(h_ref.at[r], o_ref.at[d1_ref[base + r]], sem).start(priority=1)
        return c

    lax.fori_loop(0, DISPATCH_ROWS // DMA_UNROLL, issue, 0)
    for _ in range(2):
        pltpu.make_async_copy(h_ref, h_ref, sem).wait()


def _moe_dispatch(h, dest0, dest1, zero_starts):
    sub = D // LANES
    gs = pltpu.PrefetchScalarGridSpec(
        num_scalar_prefetch=3, grid=(T // DISPATCH_ROWS,),
        in_specs=[pl.BlockSpec((DISPATCH_ROWS, sub, LANES), lambda i, *p: (i, 0, 0))],
        out_specs=pl.BlockSpec(memory_space=pl.ANY),
        scratch_shapes=[pltpu.VMEM((MOE_TM, sub, LANES), BF16), pltpu.SemaphoreType.DMA(()),
                        pltpu.SemaphoreType.DMA(())])
    out = pl.pallas_call(_dispatch_body, grid_spec=gs, out_shape=jax.ShapeDtypeStruct((R_MOE, sub, LANES), BF16),
                         compiler_params=_cparams(("arbitrary",)), name="moe_dispatch",
                         )(dest0, dest1, zero_starts, h.reshape(T, sub, LANES))
    return out.reshape(R_MOE, D)


def _combine_body(d0_ref, d1_ref, y_ref, x_ref, g0_ref, g1_ref, gate_ref, fg_ref, oc_ref, ol_ref, buf, sems):
    i = pl.program_id(0)

    def issue(step, slot):
        base = step * COMBINE_ROWS

        def body(q, c):
            for u in range(DMA_UNROLL):
                r = q * DMA_UNROLL + u
                pltpu.make_async_copy(y_ref.at[pl.ds(d0_ref[base + r], 1), :], buf.at[slot, 0, pl.ds(r, 1), :],
                                      sems.at[slot]).start(priority=0)
                pltpu.make_async_copy(y_ref.at[pl.ds(d1_ref[base + r], 1), :], buf.at[slot, 1, pl.ds(r, 1), :],
                                      sems.at[slot]).start(priority=1)
            return c

        lax.fori_loop(0, COMBINE_ROWS // DMA_UNROLL, body, 0)

    slot = i % 2

    @pl.when(i == 0)
    def _():
        issue(0, 0)

    @pl.when(i + 1 < pl.num_programs(0))
    def _():
        issue(i + 1, 1 - slot)

    pltpu.make_async_copy(buf.at[slot], buf.at[slot], sems.at[slot]).wait()
    y = g0_ref[:, 0:1] * buf[slot, 0] + g1_ref[:, 0:1] * buf[slot, 1]
    x = x_ref[...] + gate_ref[...] * y
    out = x * lax.rsqrt(jnp.mean(x * x, axis=-1, keepdims=True) + EPS) * fg_ref[...]
    nbc = T_CTX // COMBINE_ROWS

    @pl.when(i < nbc)
    def _():
        oc_ref[...] = out

    @pl.when(i >= nbc)
    def _():
        ol_ref[...] = out


def _moe_combine_norm(ybuf, dest0, dest1, g0, g1, x, mods, slot, final_g):
    nbc = T_CTX // COMBINE_ROWS
    tok = pl.BlockSpec((COMBINE_ROWS, D), lambda i, *p: (i, 0))
    gsp = pl.BlockSpec((COMBINE_ROWS, LANES), lambda i, *p: (i, 0))
    gs = pltpu.PrefetchScalarGridSpec(
        num_scalar_prefetch=2, grid=(T // COMBINE_ROWS,),
        in_specs=[pl.BlockSpec(memory_space=pl.ANY), tok, gsp, gsp,
                  pl.BlockSpec((None, None, 1, D), lambda i, *p: (slot, _cond_of_block(i, COMBINE_ROWS), 0, 0)),
                  pl.BlockSpec((1, D), lambda i, *p: (0, 0))],
        out_specs=[pl.BlockSpec((COMBINE_ROWS, D), lambda i, *p: (jnp.minimum(i, nbc - 1), 0)),
                   pl.BlockSpec((COMBINE_ROWS, D), lambda i, *p: (jnp.maximum(i - nbc, 0), 0))],
        scratch_shapes=[pltpu.VMEM((2, 2, COMBINE_ROWS, D), F32), pltpu.SemaphoreType.DMA((2,))])
    return pl.pallas_call(_combine_body, grid_spec=gs,
                          out_shape=[jax.ShapeDtypeStruct((T_CTX, D), F32), jax.ShapeDtypeStruct((T_LAT, D), F32)],
                          compiler_params=_cparams(("arbitrary",)), name="moe_combine_norm",
                          )(dest0, dest1, ybuf, x, g0, g1, mods, final_g.reshape(1, D))


def _moe_plan(idx, cnt):
    counts = cnt[0, :N_EXP]
    padded = (counts + MOE_TM - 1) // MOE_TM * MOE_TM
    p_ends = jnp.cumsum(padded)
    p_starts = p_ends - padded
    experts = jnp.arange(N_EXP, dtype=I32)[None, :]

    def dest(e, rank):
        return jnp.sum(jnp.where(e[:, None] == experts, p_starts[None, :], 0), axis=1) + rank

    def groups(tm):
        nblk = R_MOE // tm
        blk_start = jnp.arange(nblk, dtype=I32) * tm
        gid = jnp.minimum(jnp.sum((blk_start[:, None] >= p_ends[None, :]).astype(I32), axis=1), N_EXP - 1)
        nact = (p_ends[-1] // tm).astype(I32).reshape(1)
        gid = jnp.where(jnp.arange(nblk) < nact[0], gid, gid[jnp.maximum(nact[0] - 1, 0)])
        chg = jnp.concatenate([jnp.ones((1,), I32), (gid[1:] != gid[:-1]).astype(I32)])
        blk = jnp.arange(nblk, dtype=I32)
        starts = (chg == 1) & (blk < nact[0])
        later = jnp.where(starts[None, :] & (blk[None, :] > blk[:, None]), blk[None, :], nblk)
        nxt = jnp.min(later, axis=1)
        return chg, gid, nact, jnp.where(nxt == nblk, -1, nxt).astype(I32)

    tail = jnp.minimum(p_ends[-1] + jnp.arange(N_EXP, dtype=I32) * MOE_TM, R_MOE - MOE_TM)
    zero_starts = jnp.concatenate([jnp.maximum(p_ends - MOE_TM, 0), tail]).astype(I32)
    return dest(idx[:, 0], idx[:, 2]), dest(idx[:, 1], idx[:, 3]), groups, zero_starts


def _rope_tables():
    quarter = HD // 4
    inv_freq = ROPE_BASE ** (-jnp.arange(quarter, dtype=F32) / quarter)
    t = jnp.arange(L_LAT)
    row = (t // GRID_W).astype(F32)[:, None] * inv_freq
    col = (t % GRID_W).astype(F32)[:, None] * inv_freq
    ang = jnp.concatenate([row, row, col, col], axis=1)
    sign = jnp.tile(jnp.concatenate([-jnp.ones((quarter,), F32), jnp.ones((quarter,), F32)]), 2)
    cos = jnp.concatenate([jnp.ones((T_CTX, HD), F32), jnp.tile(jnp.cos(ang), (B_LAT, 1))], axis=0)
    sin = jnp.concatenate([jnp.zeros((T_CTX, HD), F32), jnp.tile(jnp.sin(ang) * sign, (B_LAT, 1))], axis=0)
    return cos, sin


def kernel(x_prompt, x_sample, cache_k, cache_v, state_rglru, c, c_ctx, w_mod, b_mod, norm_g, final_g, a_w_in, a_w_out, rnn_conv_w, rnn_conv_b, rnn_w_a, rnn_b_a, rnn_w_x, rnn_b_x, rnn_lam, attn_sink, ffn_w1, ffn_w3, ffn_w2, h_w_in, h_w_out, h_conv_w, h_conv_b, hf_w1, hf_b1, hf_w2, hf_b2, hf_freq, hf_w3, h_bias, moe_router, moe_router_b, moe_w_gate, moe_w_up, moe_w_down):
    x = (x_prompt.reshape(T_CTX, D), x_sample.reshape(T_LAT, D))
    cond = jnp.concatenate([c_ctx[None, :], c, jnp.zeros((N_COND - 1 - B_LAT, D), F32)], axis=0)
    mods_all = _ada_params(cond, w_mod, b_mod)
    mods_all = mods_all.reshape(-1, N_COND, 6, D).transpose(0, 2, 1, 3).reshape(-1, 6, N_COND, 1, D)
    cos_t, sin_t = _rope_tables()

    mods = mods_all[0]
    h = _norm(x, norm_g[0, 0], mods=mods, slots=(0, 1), out_dtype=BF16, name="norm_mix0")
    w_in = a_w_in[0]
    mm1 = functools.partial(_mm, [h], [w_in], [0], tm=2 * TM)
    q = mm1(n_cols=D_ATTN, col_off=0, tn=1024, out_dtype=F32, name="proj_q")
    kv = mm1(n_cols=2 * D_KV, col_off=D_ATTN, tn=512, out_dtype=F32, name="proj_kv")
    xy = _mm([h], [w_in[:, D_ATTN + 2 * D_KV:]], [0], n_cols=2 * D_RNN, col_off=0, tn=1024, tm=2 * TM,
             out_dtype=F32, name="proj_rnn")
    qb, kb, vb = _qk_prep(q, kv, cos_t, sin_t)
    ck = cache_k[:, 0].reshape(B_LAT, -1, D_KV)
    cv = cache_v[:, 0].reshape(B_LAT, -1, D_KV)
    o_ctx, o_lat = _attention(qb, kb, vb, ck, cv, attn_sink[0])
    rnn_w = (rnn_conv_w[0], rnn_conv_b[0], rnn_w_a[0], rnn_b_a[0], rnn_w_x[0], rnn_b_x[0], rnn_lam[0])
    r_ctx, s_ctx = _rglru(xy, jnp.zeros((B_CTX, 2, D_RNN), F32), B_CTX, L_CTX, 0, *rnn_w, name="rglru_ctx")
    r_lat, _ = _rglru(xy, state_rglru[:, 0], B_LAT, L_LAT, T_CTX, *rnn_w, name="rglru_lat")
    w_out = a_w_out[0]
    x = _mm([(o_ctx, o_lat), (r_ctx, r_lat)], [w_out, w_out], [0, 1], n_cols=D, col_off=0, tn=1024, tm=TM,
            out_dtype=F32, residual=(x, mods, 2), name="proj_out0")
    h = _norm(x, norm_g[0, 1], mods=mods, slots=(3, 4), out_dtype=BF16, name="norm_ffn0")
    hid = _mm([h], [ffn_w1[0]], [0], dual_w=[ffn_w3[0]], n_cols=D_FF, col_off=0, tn=512, tm=2 * TM,
              out_dtype=BF16, name="ffn_up")
    x = _mm([hid], [ffn_w2[0]], [0], n_cols=D, col_off=0, tn=512, tm=TM, out_dtype=F32, residual=(x, mods, 5),
            name="ffn_down")

    mods = mods_all[1]
    h = _norm(x, norm_g[1, 0], mods=mods, slots=(0, 1), out_dtype=BF16, name="norm_mix1")
    u = _mm([h], [h_w_in[0]], [0], n_cols=3 * D, col_off=0, tn=1024, tm=2 * TM, out_dtype=F32, name="hy_in")
    zs = []
    for row0, n_seq, L in ((0, B_CTX, L_CTX), (T_CTX, B_LAT, L_LAT)):
        zb = _hy_prep(u, h_conv_w[0], h_conv_b[0], row0, n_seq * L, L)
        cf, ci = _dft_mats(L)
        filt = _hy_filter(L, hf_w1[0], hf_b1[0], hf_w2[0], hf_b2[0], hf_freq[0], hf_w3[0])
        spec = _hy_spectra(filt, cf, min(HY_FC // 2, L))
        zc = _longconv(zb, n_seq, L, cf, ci, spec, 0)
        zf, zb = _hy_gate(zc, None, h_bias[0, 0], u, h_conv_w[0], h_conv_b[0], 0, row0, L)
        zc = _longconv(zb, n_seq, L, cf, ci, spec, 1)
        (zb,) = _hy_gate(zc, zf, h_bias[0, 1], u, h_conv_w[0], h_conv_b[0], 1, row0, L)
        zs.append(zb)
    x = _mm([tuple(zs)], [h_w_out[0]], [0], n_cols=D, col_off=0, tn=1024, tm=TM, out_dtype=F32,
            residual=(x, mods, 2), name="hy_out")
    h, idx, g0, g1, cnt = _norm(x, norm_g[1, 1], mods=mods, slots=(3, 4), router=(moe_router[0], moe_router_b[0]),
                                out_dtype=BF16, name="norm_moe")
    dest0, dest1, groups, zero_starts = _moe_plan(idx, cnt)
    xs = _moe_dispatch(h, dest0, dest1, zero_starts)
    group = groups(MOE_TM)
    hid = _gmm(xs, [moe_w_gate[0], moe_w_up[0]], group, tn=1024, tm=MOE_TM, out_dtype=BF16, name="moe_up")
    ybuf = _gmm(hid, [moe_w_down[0]], group, tn=512, tm=MOE_TM, out_dtype=F32, name="moe_down")
    y_prompt, y_sample = _moe_combine_norm(ybuf, dest0, dest1, g0, g1, x, mods, 5, final_g)
    y_prompt = y_prompt.reshape(B_CTX, L_CTX, D)
    y_sample = y_sample.reshape(B_LAT, L_LAT, D)
    new_k = kv[:T_CTX, :D_KV].reshape(B_CTX, 1, L_CTX, N_KV, HD)
    new_v = kv[:T_CTX, D_KV:].reshape(B_CTX, 1, L_CTX, N_KV, HD)
    new_s = s_ctx.reshape(B_CTX, 1, 2, D_RNN)
    return (y_prompt, y_sample, new_k, new_v, new_s)
```

```python
import functools
import math

import jax
import jax.numpy as jnp
from jax import lax
from jax.experimental import pallas as pl
from jax.experimental.pallas import tpu as pltpu

F32 = jnp.float32
BF16 = jnp.bfloat16
I32 = jnp.int32

D = 2048
B_CTX, L_CTX = 32, 256
B_LAT, L_LAT = 4, 2048
T_CTX = B_CTX * L_CTX
T_LAT = B_LAT * L_LAT
T = T_CTX + T_LAT
GRID_W = 64
N_HEADS, N_KV, HD = 8, 2, 128
Q_PER_KV = N_HEADS // N_KV
D_ATTN = N_HEADS * HD
D_KV = N_KV * HD
WINDOW = 128
ROPE_BASE = 10000.0
D_RNN = D // 2
RNN_BLOCKS = 8
RNN_W = D_RNN // RNN_BLOCKS
RG_LRU_C = 8.0
HY_EMB = 33
HY_W = 64
HY_MIN_DECAY = math.log(1e-2) / 1.5
HY_MAX_DECAY = math.log(1e-2) / 0.3
D_FF = 5632
N_EXP = 8
D_FFE = 7168
EPS = 1e-6
NEG_INF = -1e30

LANES = 128
SUBLANES = 8
VMEM_LIMIT = 52 * 1024 * 1024
TM = 512
N_COND = 8
MOE_TM = 512
R_MOE = 2 * T + N_EXP * MOE_TM
HY_TD = 512
HY_FC = 1024


def _cparams(sem):
    return pltpu.CompilerParams(dimension_semantics=sem, vmem_limit_bytes=VMEM_LIMIT)


def _cond_of_block(i, tm):
    nb_ctx = T_CTX // tm
    return jnp.where(i < nb_ctx, 0, 1 + (i - nb_ctx) // (L_LAT // tm))


def _rows_value(refs, i, nbc):
    if len(refs) == 1:
        return refs[0][...]
    return jnp.where(i < nbc, refs[0][...], refs[1][...])


def _rows_specs(src, tm, bw, col, clamp=None):
    nbc = T_CTX // tm
    if isinstance(src, tuple):
        return ([pl.BlockSpec((tm, bw), lambda j, i, *p: (jnp.minimum(i, nbc - 1), col(j))),
                 pl.BlockSpec((tm, bw), lambda j, i, *p: (jnp.maximum(i - nbc, 0), col(j)))], list(src))
    row = (lambda i, p: i) if clamp is None else clamp
    return [pl.BlockSpec((tm, bw), lambda j, i, *p: (row(i, p), col(j)))], [src]


def _mm_body(*refs, a_counts, dual, residual, nbc):
    it = iter(refs)
    n_a = len(a_counts)
    a_refs = [[next(it) for _ in range(c)] for c in a_counts]
    w_refs = [[next(it) for _ in range(n_a)] for _ in range(2 if dual else 1)]
    if residual:
        x_refs = [next(it) for _ in range(residual)]
        g_ref = next(it)
    o_ref = next(it)
    wb_refs = [[next(it) for _ in range(n_a)] for _ in range(2 if dual else 1)]
    i = pl.program_id(1)

    @pl.when(i == 0)
    def _():
        for ws, wbs in zip(w_refs, wb_refs):
            for w, wb in zip(ws, wbs):
                wb[...] = w[...].astype(BF16)

    def prod(wbs):
        acc = None
        for a, wb in zip(a_refs, wbs):
            p = jnp.dot(_rows_value(a, i, nbc), wb[...], preferred_element_type=F32)
            acc = p if acc is None else acc + p
        return acc

    y = prod(wb_refs[0])
    if dual:
        y = y * jax.nn.sigmoid(y) * prod(wb_refs[1])
    if residual:
        y = _rows_value(x_refs, i, nbc) + g_ref[...] * y
    o_ref[...] = y.astype(o_ref.dtype)


def _mm(a_list, w_list, w_row_blocks, *, n_cols, col_off, tn, tm, out_dtype, name, dual_w=None, residual=None):
    width = lambda a: (a[0] if isinstance(a, tuple) else a).shape[1]
    ks = [width(a) for a in a_list]
    dual = dual_w is not None
    assert n_cols % tn == 0 and col_off % tn == 0 and T % tm == 0
    cb = col_off // tn
    in_specs, ins, a_counts = [], [], []
    for a, ka in zip(a_list, ks):
        sp, ops = _rows_specs(a, tm, ka, lambda j: 0)
        in_specs += sp
        ins += ops
        a_counts.append(len(ops))
    for ws in ([w_list, dual_w] if dual else [w_list]):
        ins += list(ws)
        in_specs += [pl.BlockSpec((ka, tn), lambda j, i, rb=rb: (rb, cb + j)) for rb, ka in zip(w_row_blocks, ks)]
    n_x = 0
    if residual is not None:
        x, mods, slot = residual
        sp, ops = _rows_specs(x, tm, tn, lambda j: j)
        n_x = len(ops)
        ins += ops + [mods]
        in_specs += sp + [pl.BlockSpec((None, None, 1, tn), lambda j, i: (slot, _cond_of_block(i, tm), 0, j))]
    scratch = [pltpu.VMEM((ka, tn), BF16) for ka in ks] * (2 if dual else 1)
    body = functools.partial(_mm_body, a_counts=tuple(a_counts), dual=dual, residual=n_x, nbc=T_CTX // tm)
    return pl.pallas_call(body, grid=(n_cols // tn, T // tm), in_specs=in_specs,
                          out_specs=pl.BlockSpec((tm, tn), lambda j, i: (i, j)), scratch_shapes=scratch,
                          out_shape=jax.ShapeDtypeStruct((T, n_cols), out_dtype),
                          compiler_params=_cparams(("arbitrary", "arbitrary")), name=name)(*ins)


def _gmm_body(chg_ref, gid_ref, nact_ref, nxt_ref, a_ref, *rest, n_w, tn):
    w_refs, (o_ref, stage, wb, sem) = rest[:n_w], rest[n_w:]
    j, i = pl.program_id(0), pl.program_id(1)
    nact = nact_ref[0]

    def fetch(jj, ii):
        cols = pl.ds(pl.multiple_of(jj * tn, tn), tn)
        for m, w in enumerate(w_refs):
            pltpu.make_async_copy(w.at[gid_ref[ii], :, cols], stage.at[m], sem).start()

    refresh = (i < nact) & (chg_ref[i] != 0)

    @pl.when(refresh & (j == 0) & (i == 0))
    def _():
        fetch(0, 0)

    @pl.when(refresh)
    def _():
        for m, w in enumerate(w_refs):
            pltpu.make_async_copy(w.at[0, :, pl.ds(0, tn)], stage.at[m], sem).wait()
        for m in range(n_w):
            wb[m] = stage[m].astype(BF16)
        nxt = nxt_ref[i]

        @pl.when(nxt >= 0)
        def _():
            fetch(j, nxt)

        @pl.when((nxt < 0) & (j + 1 < pl.num_programs(0)))
        def _():
            fetch(j + 1, 0)

    @pl.when(i < nact)
    def _():
        a = a_ref[...]
        y = jnp.dot(a, wb[0], preferred_element_type=F32)
        if n_w == 2:
            y = y * jax.nn.sigmoid(y) * jnp.dot(a, wb[1], preferred_element_type=F32)
        o_ref[...] = y.astype(o_ref.dtype)

    @pl.when(i >= nact)
    def _():
        o_ref[...] = jnp.zeros_like(o_ref)


def _gmm(a, w_list, group, *, tn, tm, out_dtype, name):
    K = a.shape[1]
    n = w_list[0].shape[2]
    n_w = len(w_list)
    rowblk = lambda i, p: jnp.maximum(jnp.minimum(i, p[2][0] - 1), 0)
    gs = pltpu.PrefetchScalarGridSpec(
        num_scalar_prefetch=4, grid=(n // tn, R_MOE // tm),
        in_specs=[pl.BlockSpec((tm, K), lambda j, i, *p: (rowblk(i, p), 0))]
        + [pl.BlockSpec(memory_space=pl.ANY)] * n_w,
        out_specs=pl.BlockSpec((tm, tn), lambda j, i, *p: (i, j)),
        scratch_shapes=[pltpu.VMEM((n_w, K, tn), F32), pltpu.VMEM((n_w, K, tn), BF16), pltpu.SemaphoreType.DMA(())])
    return pl.pallas_call(functools.partial(_gmm_body, n_w=n_w, tn=tn), grid_spec=gs,
                          out_shape=jax.ShapeDtypeStruct((R_MOE, n), out_dtype),
                          compiler_params=_cparams(("arbitrary", "arbitrary")), name=name)(*group, a, *w_list)


def _ada_body(c_ref, w_ref, b_ref, o_ref):
    c = c_ref[...]
    s = (c * jax.nn.sigmoid(c)).astype(BF16)
    o_ref[...] = jnp.dot(s, w_ref[...].astype(BF16), preferred_element_type=F32) + b_ref[...]


def _ada_params(cond, w_mod, b_mod):
    depth = w_mod.shape[0]
    tn = 1024
    return pl.pallas_call(
        _ada_body, grid=(depth, 6 * D // tn),
        in_specs=[pl.BlockSpec((N_COND, D), lambda l, j: (0, 0)),
                  pl.BlockSpec((None, D, tn), lambda l, j: (l, 0, j)),
                  pl.BlockSpec((None, 1, tn), lambda l, j: (l, 0, j))],
        out_specs=pl.BlockSpec((None, N_COND, tn), lambda l, j: (l, 0, j)),
        out_shape=jax.ShapeDtypeStruct((depth, N_COND, 6 * D), F32),
        compiler_params=_cparams(("arbitrary", "arbitrary")), name="ada_params",
    )(cond, w_mod, b_mod.reshape(depth, 1, 6 * D))


def _norm_body(*refs, n_x, modulate, router):
    it = iter(refs)
    x_refs = [next(it) for _ in range(n_x)]
    g_ref = next(it)
    if modulate:
        sh_ref, sc_ref = next(it), next(it)
    if router:
        wr_ref, br_ref = next(it), next(it)
    o_ref = next(it)
    pid = pl.program_id(1)
    x = _rows_value(x_refs, pid, T_CTX // TM)
    y = x * lax.rsqrt(jnp.mean(x * x, axis=-1, keepdims=True) + EPS) * g_ref[...]
    if modulate:
        y = y * (1.0 + sc_ref[...]) + sh_ref[...]
    o_ref[...] = y.astype(o_ref.dtype)
    if router:
        idx_ref, g0_ref, g1_ref, cnt_ref, carry_ref = next(it), next(it), next(it), next(it), next(it)
        logits = jnp.dot(y, wr_ref[...], preferred_element_type=F32, precision=lax.Precision.HIGHEST) + br_ref[...]
        lane = lax.broadcasted_iota(I32, logits.shape, 1)
        logits = jnp.where(lane < N_EXP, logits, -jnp.inf)
        lanef = lane.astype(F32)
        m1 = jnp.max(logits, axis=-1, keepdims=True)
        i1 = jnp.min(jnp.where(logits == m1, lanef, float(LANES)), axis=-1, keepdims=True)
        rest = jnp.where(lanef == i1, -jnp.inf, logits)
        m2 = jnp.max(rest, axis=-1, keepdims=True)
        i2 = jnp.min(jnp.where(rest == m2, lanef, float(LANES)), axis=-1, keepdims=True)
        e21 = jnp.exp(m2 - m1)
        gate1 = 1.0 / (1.0 + e21)
        g0_ref[...] = jnp.broadcast_to(gate1, logits.shape)
        g1_ref[...] = jnp.broadcast_to(e21 * gate1, logits.shape)

        @pl.when(pid == 0)
        def _():
            carry_ref[...] = jnp.zeros_like(carry_ref)

        hot1 = jnp.where(lanef == i1, 1.0, 0.0)
        hot2 = jnp.where(lanef == i2, 1.0, 0.0)
        n = logits.shape[0]
        tri = jnp.where(lax.broadcasted_iota(I32, (n, n), 1) < lax.broadcasted_iota(I32, (n, n), 0), 1.0, 0.0)
        tri = tri.astype(BF16)
        before1 = jnp.dot(tri, hot1.astype(BF16), preferred_element_type=F32)
        before2 = jnp.dot(tri, hot2.astype(BF16), preferred_element_type=F32)
        tot1 = jnp.sum(hot1, axis=0, keepdims=True)
        tot2 = jnp.sum(hot2, axis=0, keepdims=True)
        carry = carry_ref[...]
        rank1 = jnp.sum(hot1 * (before1 + carry), axis=-1, keepdims=True)
        rank2 = jnp.sum(hot2 * (before2 + carry + tot1), axis=-1, keepdims=True)
        carry = carry + tot1 + tot2
        carry_ref[...] = carry
        cnt_ref[...] = jnp.broadcast_to(carry, cnt_ref.shape).astype(I32)
        packed = jnp.where(lane == 0, i1, jnp.where(lane == 1, i2, jnp.where(lane == 2, rank1,
                                                                             jnp.where(lane == 3, rank2, 0.0))))
        idx_ref[...] = packed.astype(I32)


def _norm(x, g, *, mods=None, slots=None, router=None, out_dtype, name, row0=0, rows=None):
    pair = isinstance(x, tuple)
    rows = (T if pair else x.shape[0]) if rows is None else rows
    rb0 = row0 // TM
    modulate = mods is not None
    in_specs, ins = _rows_specs(x, TM, D, lambda j: 0, clamp=lambda i, p: rb0 + i)
    n_x = len(ins)
    ins.append(g.reshape(1, D))
    in_specs.append(pl.BlockSpec((1, D), lambda j, i: (0, 0)))
    if modulate:
        for slot in slots:
            ins.append(mods)
            in_specs.append(pl.BlockSpec((None, None, 1, D),
                                         lambda j, i, slot=slot: (slot, _cond_of_block(i, TM), 0, 0)))
    out_shape = [jax.ShapeDtypeStruct((rows, D), out_dtype)]
    out_specs = [pl.BlockSpec((TM, D), lambda j, i: (i, 0))]
    scratch = []
    if router is not None:
        w_r, b_r = router
        ins += [jnp.pad(w_r, ((0, 0), (0, LANES - N_EXP))), jnp.pad(b_r, (0, LANES - N_EXP)).reshape(1, LANES)]
        in_specs += [pl.BlockSpec((D, LANES), lambda j, i: (0, 0)), pl.BlockSpec((1, LANES), lambda j, i: (0, 0))]
        out_shape += [jax.ShapeDtypeStruct((rows, LANES), I32), jax.ShapeDtypeStruct((rows, LANES), F32),
                      jax.ShapeDtypeStruct((rows, LANES), F32), jax.ShapeDtypeStruct((SUBLANES, LANES), I32)]
        out_specs += [pl.BlockSpec((TM, LANES), lambda j, i: (i, 0))] * 3
        out_specs += [pl.BlockSpec((SUBLANES, LANES), lambda j, i: (0, 0))]
        scratch = [pltpu.VMEM((1, LANES), F32)]
    body = functools.partial(_norm_body, n_x=n_x, modulate=modulate, router=router is not None)
    res = pl.pallas_call(body, grid=(1, rows // TM), in_specs=in_specs, out_specs=out_specs, out_shape=out_shape,
                         scratch_shapes=scratch, compiler_params=_cparams(("arbitrary", "arbitrary")), name=name)(*ins)
    return res if router is not None else res[0]


def _swap32(x):
    up = jnp.concatenate([x[:, 32:], x[:, :32]], axis=1)
    down = jnp.concatenate([x[:, 96:], x[:, :96]], axis=1)
    lane = lax.broadcasted_iota(I32, x.shape, 1)
    return jnp.where((lane % 64) < 32, up, down)


def _qkprep_body(q_ref, kv_ref, cos_ref, sin_ref, qo_ref, ko_ref, vo_ref):
    cos, sin = cos_ref[...], sin_ref[...]
    for h in range(N_HEADS):
        x = q_ref[:, h * HD:(h + 1) * HD]
        qo_ref[:, h * HD:(h + 1) * HD] = (x * cos + _swap32(x) * sin).astype(BF16)
    for h in range(N_KV):
        x = kv_ref[:, h * HD:(h + 1) * HD]
        ko_ref[:, h * HD:(h + 1) * HD] = (x * cos + _swap32(x) * sin).astype(BF16)
    vo_ref[...] = kv_ref[:, D_KV:].astype(BF16)


def _qk_prep(q, kv, cos_t, sin_t):
    return pl.pallas_call(
        _qkprep_body, grid=(T // TM,),
        in_specs=[pl.BlockSpec((TM, D_ATTN), lambda i: (i, 0)), pl.BlockSpec((TM, 2 * D_KV), lambda i: (i, 0)),
                  pl.BlockSpec((TM, HD), lambda i: (i, 0)), pl.BlockSpec((TM, HD), lambda i: (i, 0))],
        out_specs=[pl.BlockSpec((TM, D_ATTN), lambda i: (i, 0)), pl.BlockSpec((TM, D_KV), lambda i: (i, 0)),
                   pl.BlockSpec((TM, D_KV), lambda i: (i, 0))],
        out_shape=[jax.ShapeDtypeStruct((T, D_ATTN), BF16), jax.ShapeDtypeStruct((T, D_KV), BF16),
                   jax.ShapeDtypeStruct((T, D_KV), BF16)],
        compiler_params=_cparams(("arbitrary",)), name="qk_prep")(q, kv, cos_t, sin_t)


def _attn_core(q_ref, o_ref, sink_ref, kvh, kall, vall, mask):
    scale = HD ** -0.5
    for g in range(Q_PER_KV):
        head = kvh * Q_PER_KV + g
        qh = q_ref[:, head * HD:(head + 1) * HD]
        s = lax.dot_general(qh, kall, (((1,), (1,)), ((), ())), preferred_element_type=F32) * scale
        if mask is not None:
            s = jnp.where(mask, s, NEG_INF)
        sk = sink_ref[head]
        m = jnp.maximum(jnp.max(s, axis=-1, keepdims=True), sk)
        p = jnp.exp(s - m)
        denom = jnp.sum(p, axis=-1, keepdims=True) + jnp.exp(sk - m)
        o = jnp.dot(p.astype(BF16), vall, preferred_element_type=F32) / denom
        o_ref[:, head * HD:(head + 1) * HD] = o.astype(o_ref.dtype)


def _attn_ctx_body(sink_ref, q_ref, k_ref, v_ref, o_ref):
    for kvh in range(N_KV):
        cols = slice(kvh * HD, (kvh + 1) * HD)
        _attn_core(q_ref, o_ref, sink_ref, kvh, k_ref[:, cols], v_ref[:, cols], None)


def _attn_lat_body(sink_ref, q_ref, kp_ref, kc_ref, kn_ref, vp_ref, vc_ref, vn_ref, ck_ref, cv_ref, o_ref):
    qb = pl.program_id(1)
    nk = 3 * WINDOW + ck_ref.shape[0]
    qpos = qb * WINDOW + lax.broadcasted_iota(I32, (WINDOW, nk), 0)
    col = lax.broadcasted_iota(I32, (WINDOW, nk), 1)
    kpos = (qb - 1) * WINDOW + col
    in_win = (jnp.abs(kpos - qpos) <= WINDOW) & (kpos >= 0) & (kpos < L_LAT)
    mask = in_win | (col >= 3 * WINDOW)
    for kvh in range(N_KV):
        cols = slice(kvh * HD, (kvh + 1) * HD)
        kall = jnp.concatenate([kp_ref[:, cols], kc_ref[:, cols], kn_ref[:, cols], ck_ref[:, cols].astype(BF16)], axis=0)
        vall = jnp.concatenate([vp_ref[:, cols], vc_ref[:, cols], vn_ref[:, cols], cv_ref[:, cols].astype(BF16)], axis=0)
        _attn_core(q_ref, o_ref, sink_ref, kvh, kall, vall, mask)


def _attention(qb, kb, vb, cache_k, cache_v, sink):
    smem = pl.BlockSpec(memory_space=pltpu.SMEM)
    o_ctx = pl.pallas_call(
        _attn_ctx_body, grid=(B_CTX,),
        in_specs=[smem, pl.BlockSpec((L_CTX, D_ATTN), lambda b: (b, 0)),
                  pl.BlockSpec((L_CTX, D_KV), lambda b: (b, 0)), pl.BlockSpec((L_CTX, D_KV), lambda b: (b, 0))],
        out_specs=pl.BlockSpec((L_CTX, D_ATTN), lambda b: (b, 0)),
        out_shape=jax.ShapeDtypeStruct((T_CTX, D_ATTN), BF16),
        compiler_params=_cparams(("arbitrary",)), name="attn_ctx")(sink, qb, kb, vb)
    nb = L_LAT // WINDOW
    base = T_CTX // WINDOW
    cur = lambda b, i: (base + b * nb + i, 0)
    prv = lambda b, i: (base + b * nb + jnp.maximum(i - 1, 0), 0)
    nxt = lambda b, i: (base + b * nb + jnp.minimum(i + 1, nb - 1), 0)
    blk = lambda f: pl.BlockSpec((WINDOW, D_KV), f)
    cspec = pl.BlockSpec((None, cache_k.shape[1], D_KV), lambda b, i: (b, 0, 0))
    o_lat = pl.pallas_call(
        _attn_lat_body, grid=(B_LAT, nb),
        in_specs=[smem, pl.BlockSpec((WINDOW, D_ATTN), cur), blk(prv), blk(cur), blk(nxt), blk(prv), blk(cur), blk(nxt),
                  cspec, cspec],
        out_specs=pl.BlockSpec((WINDOW, D_ATTN), lambda b, i: (b * nb + i, 0)),
        out_shape=jax.ShapeDtypeStruct((T_LAT, D_ATTN), BF16),
        compiler_params=_cparams(("arbitrary", "arbitrary")), name="attn_lat",
    )(sink, qb, kb, kb, kb, vb, vb, vb, cache_k, cache_v)
    return o_ctx, o_lat


SCAN_ROWS = 2048


def _scan_pitch(L):
    return L // SUBLANES + 4


def _rglru_body(x_ref, y_ref, h0_ref, cw_ref, cb_ref, wa_ref, ba_ref, wx_ref, bx_ref, lam_ref,
                o_ref, fin_ref, a_s, b_s, p_s, h_s, *, per_chunk_seq):
    rows_total = x_ref.shape[0]
    lc = rows_total // SUBLANES
    L = lc if per_chunk_seq else rows_total
    pitch = _scan_pitch(rows_total)
    x = x_ref[...]
    row = lax.broadcasted_iota(I32, x.shape, 0)
    pos = row % L if per_chunk_seq else row
    xc = cb_ref[...] + x * cw_ref[2:3, :]
    xc = xc + jnp.where(pos >= 2, pltpu.roll(x, 2, 0), 0.0) * cw_ref[0:1, :]
    xc = xc + jnp.where(pos >= 1, pltpu.roll(x, 1, 0), 0.0) * cw_ref[1:2, :]
    xc = xc + jnp.where(pos < L - 1, pltpu.roll(x, rows_total - 1, 0), 0.0) * cw_ref[3:4, :]
    xcb = xc.astype(BF16)
    for d in range(2):
        ga = jnp.dot(xcb, wa_ref[d].astype(BF16), preferred_element_type=F32) + ba_ref[d:d + 1, :]
        gx = jnp.dot(xcb, wx_ref[d].astype(BF16), preferred_element_type=F32) + bx_ref[d:d + 1, :]
        sigmoid = lambda v: 0.5 * jnp.tanh(0.5 * v) + 0.5
        log_a = -RG_LRU_C * sigmoid(ga) * jax.nn.softplus(-lam_ref[d:d + 1, :])
        a = jnp.exp(log_a)
        b = jnp.sqrt(-jnp.tanh(log_a) * (1.0 + a * a)) * sigmoid(gx) * xc
        for s in range(SUBLANES):
            a_s[d, s * pitch:s * pitch + lc, :] = a[s * lc:(s + 1) * lc]
            b_s[d, s * pitch:s * pitch + lc, :] = b[s * lc:(s + 1) * lc]

    def step(i, carry):
        out = []
        for d, t in ((0, i), (1, lc - 1 - i)):
            rows = pl.ds(t, SUBLANES, stride=pitch)
            a = a_s[d, rows, :]
            h = a * carry[2 * d] + b_s[d, rows, :]
            p = a * carry[2 * d + 1]
            h_s[d, rows, :] = h
            p_s[d, rows, :] = p
            out += [h, p]
        return tuple(out)

    zero = jnp.zeros((SUBLANES, RNN_W), F32)
    one = jnp.ones((SUBLANES, RNN_W), F32)
    hf, pf, hb, pb = lax.fori_loop(0, lc, step, (zero, one, zero, one), unroll=4)

    r8 = lax.broadcasted_iota(I32, (SUBLANES, RNN_W), 0)

    def chunk_carry(p, h, h0, reverse):
        for k in (1, 2, 4):
            sh = SUBLANES - k if reverse else k
            m = (r8 < SUBLANES - k) if reverse else (r8 >= k)
            h = jnp.where(m, p * pltpu.roll(h, sh, 0) + h, h)
            p = jnp.where(m, p * pltpu.roll(p, sh, 0), p)
        h0 = jnp.broadcast_to(h0, (SUBLANES, RNN_W))
        state = p * h0 + h
        if reverse:
            return state, jnp.where(r8 < SUBLANES - 1, pltpu.roll(state, SUBLANES - 1, 0), h0)
        return state, jnp.where(r8 >= 1, pltpu.roll(state, 1, 0), h0)

    if per_chunk_seq:
        cf, cb = h0_ref[0], h0_ref[1]
        sf, sb = pf * cf + hf, pb * cb + hb
    else:
        sf, cf = chunk_carry(pf, hf, h0_ref[0:1, :], False)
        sb, cb = chunk_carry(pb, hb, h0_ref[1:2, :], True)

    for s in range(SUBLANES):
        rows = slice(s * pitch, s * pitch + lc)
        hsum = (h_s[0, rows, :] + p_s[0, rows, :] * cf[s:s + 1, :]) + (h_s[1, rows, :] + p_s[1, rows, :] * cb[s:s + 1, :])
        o_ref[s * lc:(s + 1) * lc, :] = (hsum * jax.nn.gelu(y_ref[s * lc:(s + 1) * lc, :])).astype(o_ref.dtype)
    if per_chunk_seq:
        fin_ref[0] = sf
        fin_ref[1] = sb
    else:
        fin_ref[0:1, :] = sf[SUBLANES - 1:SUBLANES, :]
        fin_ref[1:2, :] = sb[0:1, :]


def _rglru(xy, h0, n_seq, L, row0, conv_w, conv_b, w_a, b_a, w_x, b_x, lam, name):
    assert L in (SCAN_ROWS, SCAN_ROWS // SUBLANES)
    multi = L != SCAN_ROWS
    rb0 = row0 // SCAN_ROWS
    nb = RNN_BLOCKS
    vec = lambda r: pl.BlockSpec((r, RNN_W), lambda b, n: (0, n))
    wsp = pl.BlockSpec((2, None, RNN_W, RNN_W), lambda b, n: (0, n, 0, 0))
    if multi:
        h0 = h0.transpose(1, 0, 2)
        state = pl.BlockSpec((2, SUBLANES, RNN_W), lambda b, n: (0, b, n))
        state_shape = (2, n_seq, D_RNN)
    else:
        state = pl.BlockSpec((None, 2, RNN_W), lambda b, n: (b, 0, n))
        state_shape = (n_seq, 2, D_RNN)
    out, fin = pl.pallas_call(
        functools.partial(_rglru_body, per_chunk_seq=multi), grid=(n_seq * L // SCAN_ROWS, nb),
        in_specs=[pl.BlockSpec((SCAN_ROWS, RNN_W), lambda b, n: (rb0 + b, n)),
                  pl.BlockSpec((SCAN_ROWS, RNN_W), lambda b, n: (rb0 + b, nb + n)),
                  state, vec(4), vec(1), wsp, vec(2), wsp, vec(2), vec(2)],
        out_specs=[pl.BlockSpec((SCAN_ROWS, RNN_W), lambda b, n: (b, n)), state],
        out_shape=[jax.ShapeDtypeStruct((n_seq * L, D_RNN), BF16), jax.ShapeDtypeStruct(state_shape, F32)],
        scratch_shapes=[pltpu.VMEM((2, SUBLANES * _scan_pitch(SCAN_ROWS), RNN_W), F32)] * 4,
        compiler_params=_cparams(("arbitrary", "arbitrary")), name=name,
    )(xy, xy, h0, conv_w, conv_b.reshape(1, D_RNN), w_a, b_a, w_x, b_x, lam)
    return out, (fin.transpose(1, 0, 2) if multi else fin)


def _hy_filter_body(fv_ref, w1_ref, b1_ref, w2_ref, b2_ref, fr_ref, w3_ref, dl_ref, o_ref, hid_s, tt_s, *, L):
    hp = lax.Precision.HIGHEST
    rowi = lax.broadcasted_iota(I32, (L, LANES), 0)
    lane = lax.broadcasted_iota(I32, (L, LANES), 1)

    def features(pos):
        posf = pos.astype(F32)
        tt = posf / (L - 1)
        ang = fv_ref[...] * (2.0 * math.pi * posf / L)
        z = jnp.where(lane == 0, tt, jnp.where(lane <= 16, jnp.cos(ang), jnp.where(lane <= 32, -jnp.sin(ang), 0.0)))
        return z, tt[:, 0:1]

    is_bwd = pl.program_id(0) == 1

    @pl.when(pl.program_id(1) == 0)
    def _():
        z, tt = features(jnp.where(is_bwd, L - rowi, rowi))
        h = jnp.sin(fr_ref[0:1, :] * (jnp.dot(z, w1_ref[...], preferred_element_type=F32, precision=hp) + b1_ref[...]))
        h = jnp.sin(fr_ref[1:2, :] * (jnp.dot(h, w2_ref[...], preferred_element_type=F32, precision=hp) + b2_ref[...]))
        hid_s[...] = h
        tt_s[...] = jnp.broadcast_to(tt, tt_s.shape)

    filt = jnp.dot(hid_s[...], w3_ref[...], preferred_element_type=F32, precision=hp)
    filt = filt * jnp.exp(-tt_s[:, 0:1] * dl_ref[...])
    dead = is_bwd & (lax.broadcasted_iota(I32, filt.shape, 0) == 0)
    o_ref[...] = jnp.where(dead, 0.0, filt).astype(o_ref.dtype)


def _hy_filter(L, f_w1, f_b1, f_w2, f_b2, f_freq, f_w3):
    bands = (HY_EMB - 1) // 2
    f = jnp.linspace(1e-4, bands - 1, bands, dtype=F32)
    fv = jnp.zeros((LANES,), F32).at[1:1 + bands].set(f).at[1 + bands:1 + 2 * bands].set(f).reshape(1, LANES)
    padw = lambda w, r, c: jnp.pad(w.astype(F32), ((0, r - w.shape[0]), (0, c - w.shape[1])))
    padv = lambda v: jnp.pad(v.astype(F32), (0, LANES - v.shape[0])).reshape(1, LANES)
    w1, w2 = padw(f_w1, LANES, LANES), padw(f_w2, LANES, LANES)
    w3 = padw(f_w3, LANES, f_w3.shape[1])
    fr = jnp.pad(f_freq.astype(F32), ((0, 0), (0, LANES - HY_W)))
    deltas = jnp.abs(jnp.linspace(HY_MIN_DECAY, HY_MAX_DECAY, D, dtype=F32)).reshape(1, D)
    tn = 1024
    per = D // tn
    full = lambda r: pl.BlockSpec((r, LANES), lambda d, j: (0, 0))
    col = lambda d, j: ((j // per) * 2 + d) * per + j % per
    return pl.pallas_call(
        functools.partial(_hy_filter_body, L=L), grid=(2, 2 * per),
        in_specs=[full(1), full(LANES), full(1), full(LANES), full(1), full(2),
                  pl.BlockSpec((LANES, tn), lambda d, j: (0, col(d, j))),
                  pl.BlockSpec((1, tn), lambda d, j: (0, j % per))],
        out_specs=pl.BlockSpec((L, tn), lambda d, j: (0, col(d, j))),
        out_shape=jax.ShapeDtypeStruct((L, 4 * D), BF16),
        scratch_shapes=[pltpu.VMEM((L, LANES), F32), pltpu.VMEM((L, LANES), F32)],
        compiler_params=_cparams(("arbitrary", "arbitrary")), name=f"hy_filter_{L}",
    )(fv, w1, padv(f_b1), w2, padv(f_b2), fr, w3, deltas)


DFT_RB = 64


def _dft_body(ca_ref, sa_ref, cb_ref, sb_ref, cf_ref, ci_ref, *, L):
    n = 2 * L
    ca, sa, cb, sb = ca_ref[...], sa_ref[...], cb_ref[...], sb_ref[...]
    cos = ca * cb - sa * sb
    sin = sa * cb + ca * sb
    r = pl.program_id(0) * DFT_RB + lax.broadcasted_iota(I32, cos.shape, 0)
    j = lax.broadcasted_iota(I32, cos.shape, 1)
    alt_j = jnp.where(j % 2 == 0, 1.0, -1.0)
    alt_r = jnp.where(r % 2 == 0, 1.0, -1.0)
    cf_ref[0] = cos.astype(BF16)
    cf_ref[1] = jnp.where(r == 0, alt_j, -sin).astype(BF16)
    w = jnp.where(j == 0, 1.0, 2.0) / n
    ci_ref[0] = (w * cos).astype(BF16)
    ci_ref[1] = jnp.where(j == 0, alt_r / n, -w * sin).astype(BF16)


def _dft_mats(L):
    n = 2 * L
    j = jnp.arange(L, dtype=I32)[None, :]
    ang = lambda r: ((r[:, None] * j) % n).astype(F32) * (2.0 * math.pi / n)
    ang_a = ang(jnp.arange(L // DFT_RB, dtype=I32) * DFT_RB)
    ang_b = ang(jnp.arange(DFT_RB, dtype=I32))
    row = pl.BlockSpec((None, 1, L), lambda a: (a, 0, 0))
    full = pl.BlockSpec((DFT_RB, L), lambda a: (0, 0))
    out = pl.BlockSpec((2, DFT_RB, L), lambda a: (0, a, 0))
    return pl.pallas_call(
        functools.partial(_dft_body, L=L), grid=(L // DFT_RB,), in_specs=[row, row, full, full],
        out_specs=[out, out], out_shape=[jax.ShapeDtypeStruct((2, L, L), BF16)] * 2,
        compiler_params=_cparams(("arbitrary",)), name=f"dft_tables_{L}",
    )(jnp.cos(ang_a)[:, None, :], jnp.sin(ang_a)[:, None, :], jnp.cos(ang_b), jnp.sin(ang_b))


def _spec_body(cf_ref, c1_ref, c2_ref, o_ref):
    f = pl.program_id(1)
    h = cf_ref.shape[1]
    k = f * h + lax.broadcasted_iota(I32, (h, c1_ref.shape[1]), 0)
    odd = k % 2 == 1
    for part in range(2):
        z1 = jnp.dot(cf_ref[part], c1_ref[...], preferred_element_type=F32)
        z2 = jnp.dot(cf_ref[part], c2_ref[...], preferred_element_type=F32)
        flip = odd if part == 0 else odd & (k != 0)
        o_ref[part * h:(part + 1) * h, :] = z1 + jnp.where(flip, -z2, z2)


def _hy_spectra(filt, cf, h):
    L = cf.shape[1]
    nf = L // h
    td = HY_TD
    per = D // td
    return pl.pallas_call(
        _spec_body, grid=(2 * per, nf),
        in_specs=[pl.BlockSpec((2, h, L), lambda c, f: (0, f, 0)),
                  pl.BlockSpec((L, td), lambda c, f: (0, (c // per) * 2 * per + c % per)),
                  pl.BlockSpec((L, td), lambda c, f: (0, (c // per) * 2 * per + per + c % per))],
        out_specs=pl.BlockSpec((None, 2 * h, td), lambda c, f: (f, 0, c)),
        out_shape=jax.ShapeDtypeStruct((nf, 2 * h, 2 * D), F32),
        compiler_params=_cparams(("arbitrary", "arbitrary")), name=f"hy_spectra_{L}")(cf, filt, filt)


HY_TC = 256
HY_TR = 2048


def _hy_conv(u_ref, cw_ref, cb_ref, L):
    u = u_ref[...]
    rows = u.shape[0]
    pos = lax.broadcasted_iota(I32, u.shape, 0) % L
    uc = cb_ref[...] + u * cw_ref[1:2, :]
    uc = uc + jnp.where(pos >= 1, pltpu.roll(u, 1, 0), 0.0) * cw_ref[0:1, :]
    return uc + jnp.where(pos < L - 1, pltpu.roll(u, rows - 1, 0), 0.0) * cw_ref[2:3, :]


def _hy_third_specs(third, row0, L):
    tr = max(L, HY_TR)
    per = D // HY_TC
    col = lambda i, c: third * per + c
    return [pl.BlockSpec((tr, HY_TC), lambda i, c: (row0 // tr + i, col(i, c))),
            pl.BlockSpec((3, HY_TC), lambda i, c: (0, col(i, c))),
            pl.BlockSpec((1, HY_TC), lambda i, c: (0, col(i, c)))]


def _hy_prep_body(u_ref, cw_ref, cb_ref, zb_ref, *, L):
    zb_ref[...] = _hy_conv(u_ref, cw_ref, cb_ref, L).astype(BF16)


def _hy_prep(u, conv_w, conv_b, row0, rows, L):
    tr = max(L, HY_TR)
    return pl.pallas_call(
        functools.partial(_hy_prep_body, L=L), grid=(rows // tr, D // HY_TC),
        in_specs=_hy_third_specs(0, row0, L), out_specs=pl.BlockSpec((tr, HY_TC), lambda i, c: (i, c)),
        out_shape=jax.ShapeDtypeStruct((rows, D), BF16),
        compiler_params=_cparams(("arbitrary", "arbitrary")), name=f"hy_prep_{L}",
    )(u, conv_w, conv_b.reshape(1, 3 * D))


def _longconv_body(z_ref, cf_ref, ci_ref, s_ref, o_ref, acc_ref):
    f = pl.program_id(2)
    h = cf_ref.shape[1]
    z = z_ref[...]
    zre = jnp.dot(cf_ref[0], z, preferred_element_type=F32)
    zim = jnp.dot(cf_ref[1], z, preferred_element_type=F32)
    sre, sim = s_ref[:h, :], s_ref[h:, :]
    first = (lax.broadcasted_iota(I32, zre.shape, 0) == 0) & (f == 0)
    yre = zre * sre - jnp.where(first, 0.0, zim * sim)
    yim = jnp.where(first, zim * sim, zre * sim + zim * sre)
    contrib = (jnp.dot(ci_ref[0], yre.astype(BF16), preferred_element_type=F32)
               + jnp.dot(ci_ref[1], yim.astype(BF16), preferred_element_type=F32))

    @pl.when(f == 0)
    def _():
        acc_ref[...] = contrib

    @pl.when(f > 0)
    def _():
        acc_ref[...] += contrib

    @pl.when(f == pl.num_programs(2) - 1)
    def _():
        o_ref[...] = acc_ref[...]


def _longconv(z, n_seq, L, cf, ci, spec, order):
    nf, fc, _ = spec.shape
    h = fc // 2
    td = HY_TD if L > TM else D
    per = D // td
    return pl.pallas_call(
        _longconv_body, grid=(n_seq, per, nf),
        in_specs=[pl.BlockSpec((L, td), lambda b, c, f: (b, c)),
                  pl.BlockSpec((2, h, L), lambda b, c, f: (0, f, 0)),
                  pl.BlockSpec((2, L, h), lambda b, c, f: (0, 0, f)),
                  pl.BlockSpec((None, fc, td), lambda b, c, f: (f, 0, order * per + c))],
        out_specs=pl.BlockSpec((L, td), lambda b, c, f: (b, c)),
        out_shape=jax.ShapeDtypeStruct((n_seq * L, D), F32),
        scratch_shapes=[pltpu.VMEM((L, td), F32)],
        compiler_params=_cparams(("arbitrary", "arbitrary", "arbitrary")), name=f"longconv_{L}_{order}",
    )(z, cf, ci, spec)


def _hy_gate_body(zc_ref, b_ref, *refs, L, first):
    if first:
        z = _hy_conv(*refs[0:3], L)
        gate_refs, out_refs = refs[3:6], refs[6:]
    else:
        z = refs[0][...]
        gate_refs, out_refs = refs[1:4], refs[4:]
    z = _hy_conv(*gate_refs, L) * (zc_ref[...] + z * b_ref[...])
    for o_ref in out_refs:
        o_ref[...] = z.astype(o_ref.dtype)


def _hy_gate(zc, z, bias, u, conv_w, conv_b, order, row0, L):
    rows = zc.shape[0]
    tr = max(L, HY_TR)
    blk = pl.BlockSpec((tr, HY_TC), lambda i, c: (i, c))
    first = z is None
    uargs = (u, conv_w, conv_b.reshape(1, 3 * D))
    ins = [zc, bias.reshape(1, D)] + (list(uargs) if first else [z]) + list(uargs)
    in_specs = ([blk, pl.BlockSpec((1, HY_TC), lambda i, c: (0, c))]
                + (_hy_third_specs(0, row0, L) if first else [blk]) + _hy_third_specs(order + 1, row0, L))
    dts = ([F32] if first else []) + [BF16]
    return pl.pallas_call(
        functools.partial(_hy_gate_body, L=L, first=first), grid=(rows // tr, D // HY_TC),
        in_specs=in_specs, out_specs=[blk] * len(dts),
        out_shape=[jax.ShapeDtypeStruct((rows, D), dt) for dt in dts],
        compiler_params=_cparams(("arbitrary", "arbitrary")), name=f"hy_gate_{L}_{order}",
    )(*ins)


DISPATCH_ROWS = 256
COMBINE_ROWS = 128
DMA_UNROLL = 8


def _dispatch_body(d0_ref, d1_ref, zs_ref, h_ref, o_ref, zbuf, sem, zsem):
    base = pl.program_id(0) * DISPATCH_ROWS

    @pl.when(pl.program_id(0) == 0)
    def _():
        zbuf[...] = jnp.zeros_like(zbuf)
        for k in range(2 * N_EXP):
            fill = pltpu.make_async_copy(zbuf, o_ref.at[pl.ds(zs_ref[k], MOE_TM)], zsem)
            fill.start()
            fill.wait()

    def issue(q, c):
        for u in range(DMA_UNROLL):
            r = q * DMA_UNROLL + u
            pltpu.make_async_copy(h_ref.at[r], o_ref.at[d0_ref[base + r]], sem).start(priority=0)
            pltpu.make_async_copy(h_ref.at[r], o_ref.at[d1_ref[base + r]], sem).start(priority=1)
        return c

    lax.fori_loop(0, DISPATCH_ROWS // DMA_UNROLL, issue, 0)
    for _ in range(2):
        pltpu.make_async_copy(h_ref, h_ref, sem).wait()


def _moe_dispatch(h, dest0, dest1, zero_starts):
    sub = D // LANES
    gs = pltpu.PrefetchScalarGridSpec(
        num_scalar_prefetch=3, grid=(T // DISPATCH_ROWS,),
        in_specs=[pl.BlockSpec((DISPATCH_ROWS, sub, LANES), lambda i, *p: (i, 0, 0))],
        out_specs=pl.BlockSpec(memory_space=pl.ANY),
        scratch_shapes=[pltpu.VMEM((MOE_TM, sub, LANES), BF16), pltpu.SemaphoreType.DMA(()),
                        pltpu.SemaphoreType.DMA(())])
    out = pl.pallas_call(_dispatch_body, grid_spec=gs, out_shape=jax.ShapeDtypeStruct((R_MOE, sub, LANES), BF16),
                         compiler_params=_cparams(("arbitrary",)), name="moe_dispatch",
                         )(dest0, dest1, zero_starts, h.reshape(T, sub, LANES))
    return out.reshape(R_MOE, D)


def _combine_body(d0_ref, d1_ref, y_ref, x_ref, g0_ref, g1_ref, gate_ref, fg_ref, oc_ref, ol_ref, buf, sems):
    i = pl.program_id(0)

    def issue(step, slot):
        base = step * COMBINE_ROWS

        def body(q, c):
            for u in range(DMA_UNROLL):
                r = q * DMA_UNROLL + u
                pltpu.make_async_copy(y_ref.at[pl.ds(d0_ref[base + r], 1), :], buf.at[slot, 0, pl.ds(r, 1), :],
                                      sems.at[slot]).start(priority=0)
                pltpu.make_async_copy(y_ref.at[pl.ds(d1_ref[base + r], 1), :], buf.at[slot, 1, pl.ds(r, 1), :],
                                      sems.at[slot]).start(priority=1)
            return c

        lax.fori_loop(0, COMBINE_ROWS // DMA_UNROLL, body, 0)

    slot = i % 2

    @pl.when(i == 0)
    def _():
        issue(0, 0)

    @pl.when(i + 1 < pl.num_programs(0))
    def _():
        issue(i + 1, 1 - slot)

    pltpu.make_async_copy(buf.at[slot], buf.at[slot], sems.at[slot]).wait()
    y = g0_ref[:, 0:1] * buf[slot, 0] + g1_ref[:, 0:1] * buf[slot, 1]
    x = x_ref[...] + gate_ref[...] * y
    out = x * lax.rsqrt(jnp.mean(x * x, axis=-1, keepdims=True) + EPS) * fg_ref[...]
    nbc = T_CTX // COMBINE_ROWS

    @pl.when(i < nbc)
    def _():
        oc_ref[...] = out

    @pl.when(i >= nbc)
    def _():
        ol_ref[...] = out


def _moe_combine_norm(ybuf, dest0, dest1, g0, g1, x, mods, slot, final_g):
    nbc = T_CTX // COMBINE_ROWS
    tok = pl.BlockSpec((COMBINE_ROWS, D), lambda i, *p: (i, 0))
    gsp = pl.BlockSpec((COMBINE_ROWS, LANES), lambda i, *p: (i, 0))
    gs = pltpu.PrefetchScalarGridSpec(
        num_scalar_prefetch=2, grid=(T // COMBINE_ROWS,),
        in_specs=[pl.BlockSpec(memory_space=pl.ANY), tok, gsp, gsp,
                  pl.BlockSpec((None, None, 1, D), lambda i, *p: (slot, _cond_of_block(i, COMBINE_ROWS), 0, 0)),
                  pl.BlockSpec((1, D), lambda i, *p: (0, 0))],
        out_specs=[pl.BlockSpec((COMBINE_ROWS, D), lambda i, *p: (jnp.minimum(i, nbc - 1), 0)),
                   pl.BlockSpec((COMBINE_ROWS, D), lambda i, *p: (jnp.maximum(i - nbc, 0), 0))],
        scratch_shapes=[pltpu.VMEM((2, 2, COMBINE_ROWS, D), F32), pltpu.SemaphoreType.DMA((2,))])
    return pl.pallas_call(_combine_body, grid_spec=gs,
                          out_shape=[jax.ShapeDtypeStruct((T_CTX, D), F32), jax.ShapeDtypeStruct((T_LAT, D), F32)],
                          compiler_params=_cparams(("arbitrary",)), name="moe_combine_norm",
                          )(dest0, dest1, ybuf, x, g0, g1, mods, final_g.reshape(1, D))


def _moe_plan(idx, cnt):
    counts = cnt[0, :N_EXP]
    padded = (counts + MOE_TM - 1) // MOE_TM * MOE_TM
    p_ends = jnp.cumsum(padded)
    p_starts = p_ends - padded
    experts = jnp.arange(N_EXP, dtype=I32)[None, :]

    def dest(e, rank):
        return jnp.sum(jnp.where(e[:, None] == experts, p_starts[None, :], 0), axis=1) + rank

    def groups(tm):
        nblk = R_MOE // tm
        blk_start = jnp.arange(nblk, dtype=I32) * tm
        gid = jnp.minimum(jnp.sum((blk_start[:, None] >= p_ends[None, :]).astype(I32), axis=1), N_EXP - 1)
        nact = (p_ends[-1] // tm).astype(I32).reshape(1)
        gid = jnp.where(jnp.arange(nblk) < nact[0], gid, gid[jnp.maximum(nact[0] - 1, 0)])
        chg = jnp.concatenate([jnp.ones((1,), I32), (gid[1:] != gid[:-1]).astype(I32)])
        blk = jnp.arange(nblk, dtype=I32)
        starts = (chg == 1) & (blk < nact[0])
        later = jnp.where(starts[None, :] & (blk[None, :] > blk[:, None]), blk[None, :], nblk)
        nxt = jnp.min(later, axis=1)
        return chg, gid, nact, jnp.where(nxt == nblk, -1, nxt).astype(I32)

    tail = jnp.minimum(p_ends[-1] + jnp.arange(N_EXP, dtype=I32) * MOE_TM, R_MOE - MOE_TM)
    zero_starts = jnp.concatenate([jnp.maximum(p_ends - MOE_TM, 0), tail]).astype(I32)
    return dest(idx[:, 0], idx[:, 2]), dest(idx[:, 1], idx[:, 3]), groups, zero_starts


def _rope_tables():
    quarter = HD // 4
    inv_freq = ROPE_BASE ** (-jnp.arange(quarter, dtype=F32) / quarter)
    t = jnp.arange(L_LAT)
    row = (t // GRID_W).astype(F32)[:, None] * inv_freq
    col = (t % GRID_W).astype(F32)[:, None] * inv_freq
    ang = jnp.concatenate([row, row, col, col], axis=1)
    sign = jnp.tile(jnp.concatenate([-jnp.ones((quarter,), F32), jnp.ones((quarter,), F32)]), 2)
    cos = jnp.concatenate([jnp.ones((T_CTX, HD), F32), jnp.tile(jnp.cos(ang), (B_LAT, 1))], axis=0)
    sin = jnp.concatenate([jnp.zeros((T_CTX, HD), F32), jnp.tile(jnp.sin(ang) * sign, (B_LAT, 1))], axis=0)
    return cos, sin


def kernel(x_prompt, x_sample, cache_k, cache_v, state_rglru, c, c_ctx, w_mod, b_mod, norm_g, final_g, a_w_in, a_w_out, rnn_conv_w, rnn_conv_b, rnn_w_a, rnn_b_a, rnn_w_x, rnn_b_x, rnn_lam, attn_sink, ffn_w1, ffn_w3, ffn_w2, h_w_in, h_w_out, h_conv_w, h_conv_b, hf_w1, hf_b1, hf_w2, hf_b2, hf_freq, hf_w3, h_bias, moe_router, moe_router_b, moe_w_gate, moe_w_up, moe_w_down):
    x = (x_prompt.reshape(T_CTX, D), x_sample.reshape(T_LAT, D))
    cond = jnp.concatenate([c_ctx[None, :], c, jnp.zeros((N_COND - 1 - B_LAT, D), F32)], axis=0)
    mods_all = _ada_params(cond, w_mod, b_mod)
    mods_all = mods_all.reshape(-1, N_COND, 6, D).transpose(0, 2, 1, 3).reshape(-1, 6, N_COND, 1, D)
    cos_t, sin_t = _rope_tables()

    mods = mods_all[0]
    h = _norm(x, norm_g[0, 0], mods=mods, slots=(0, 1), out_dtype=BF16, name="norm_mix0")
    w_in = a_w_in[0]
    mm1 = functools.partial(_mm, [h], [w_in], [0], tm=2 * TM)
    q = mm1(n_cols=D_ATTN, col_off=0, tn=1024, out_dtype=F32, name="proj_q")
    kv = mm1(n_cols=2 * D_KV, col_off=D_ATTN, tn=512, out_dtype=F32, name="proj_kv")
    xy = _mm([h], [w_in[:, D_ATTN + 2 * D_KV:]], [0], n_cols=2 * D_RNN, col_off=0, tn=1024, tm=2 * TM,
             out_dtype=F32, name="proj_rnn")
    qb, kb, vb = _qk_prep(q, kv, cos_t, sin_t)
    ck = cache_k[:, 0].reshape(B_LAT, -1, D_KV)
    cv = cache_v[:, 0].reshape(B_LAT, -1, D_KV)
    o_ctx, o_lat = _attention(qb, kb, vb, ck, cv, attn_sink[0])
    rnn_w = (rnn_conv_w[0], rnn_conv_b[0], rnn_w_a[0], rnn_b_a[0], rnn_w_x[0], rnn_b_x[0], rnn_lam[0])
    r_ctx, s_ctx = _rglru(xy, jnp.zeros((B_CTX, 2, D_RNN), F32), B_CTX, L_CTX, 0, *rnn_w, name="rglru_ctx")
    r_lat, _ = _rglru(xy, state_rglru[:, 0], B_LAT, L_LAT, T_CTX, *rnn_w, name="rglru_lat")
    w_out = a_w_out[0]
    x = _mm([(o_ctx, o_lat), (r_ctx, r_lat)], [w_out, w_out], [0, 1], n_cols=D, col_off=0, tn=1024, tm=TM,
            out_dtype=F32, residual=(x, mods, 2), name="proj_out0")
    h = _norm(x, norm_g[0, 1], mods=mods, slots=(3, 4), out_dtype=BF16, name="norm_ffn0")
    hid = _mm([h], [ffn_w1[0]], [0], dual_w=[ffn_w3[0]], n_cols=D_FF, col_off=0, tn=512, tm=2 * TM,
              out_dtype=BF16, name="ffn_up")
    x = _mm([hid], [ffn_w2[0]], [0], n_cols=D, col_off=0, tn=512, tm=TM, out_dtype=F32, residual=(x, mods, 5),
            name="ffn_down")

    mods = mods_all[1]
    h = _norm(x, norm_g[1, 0], mods=mods, slots=(0, 1), out_dtype=BF16, name="norm_mix1")
    u = _mm([h], [h_w_in[0]], [0], n_cols=3 * D, col_off=0, tn=1024, tm=2 * TM, out_dtype=F32, name="hy_in")
    zs = []
    for row0, n_seq, L in ((0, B_CTX, L_CTX), (T_CTX, B_LAT, L_LAT)):
        zb = _hy_prep(u, h_conv_w[0], h_conv_b[0], row0, n_seq * L, L)
        cf, ci = _dft_mats(L)
        filt = _hy_filter(L, hf_w1[0], hf_b1[0], hf_w2[0], hf_b2[0], hf_freq[0], hf_w3[0])
        spec = _hy_spectra(filt, cf, min(HY_FC // 2, L))
        zc = _longconv(zb, n_seq, L, cf, ci, spec, 0)
        zf, zb = _hy_gate(zc, None, h_bias[0, 0], u, h_conv_w[0], h_conv_b[0], 0, row0, L)
        zc = _longconv(zb, n_seq, L, cf, ci, spec, 1)
        (zb,) = _hy_gate(zc, zf, h_bias[0, 1], u, h_conv_w[0], h_conv_b[0], 1, row0, L)
        zs.append(zb)
    x = _mm([tuple(zs)], [h_w_out[0]], [0], n_cols=D, col_off=0, tn=1024, tm=TM, out_dtype=F32,
            residual=(x, mods, 2), name="hy_out")
    h, idx, g0, g1, cnt = _norm(x, norm_g[1, 1], mods=mods, slots=(3, 4), router=(moe_router[0], moe_router_b[0]),
                                out_dtype=BF16, name="norm_moe")
    dest0, dest1, groups, zero_starts = _moe_plan(idx, cnt)
    xs = _moe_dispatch(h, dest0, dest1, zero_starts)
    group = groups(MOE_TM)
    hid = _gmm(xs, [moe_w_gate[0], moe_w_up[0]], group, tn=1024, tm=MOE_TM, out_dtype=BF16, name="moe_up")
    ybuf = _gmm(hid, [moe_w_down[0]], group, tn=512, tm=MOE_TM, out_dtype=F32, name="moe_down")
    y_prompt, y_sample = _moe_combine_norm(ybuf, dest0, dest1, g0, g1, x, mods, 5, final_g)
    y_prompt = y_prompt.reshape(B_CTX, L_CTX, D)
    y_sample = y_sample.reshape(B_LAT, L_LAT, D)
    new_k = kv[:T_CTX, :D_KV].reshape(B_CTX, 1, L_CTX, N_KV, HD)
    new_v = kv[:T_CTX, D_KV:].reshape(B_CTX, 1, L_CTX, N_KV, HD)
    new_s = s_ctx.reshape(B_CTX, 1, 2, D_RNN)
    return (y_prompt, y_sample, new_k, new_v, new_s)
```

```python
import functools
import math

import jax
import jax.numpy as jnp
from jax import lax
from jax.experimental import pallas as pl
from jax.experimental.pallas import tpu as pltpu

F32 = jnp.float32
BF16 = jnp.bfloat16
I32 = jnp.int32

D = 2048
B_CTX, L_CTX = 32, 256
B_LAT, L_LAT = 4, 2048
T_CTX = B_CTX * L_CTX
T_LAT = B_LAT * L_LAT
T = T_CTX + T_LAT
GRID_W = 64
N_HEADS, N_KV, HD = 8, 2, 128
Q_PER_KV = N_HEADS // N_KV
D_ATTN = N_HEADS * HD
D_KV = N_KV * HD
WINDOW = 128
ROPE_BASE = 10000.0
D_RNN = D // 2
RNN_BLOCKS = 8
RNN_W = D_RNN // RNN_BLOCKS
RG_LRU_C = 8.0
HY_EMB = 33
HY_W = 64
HY_MIN_DECAY = math.log(1e-2) / 1.5
HY_MAX_DECAY = math.log(1e-2) / 0.3
D_FF = 5632
N_EXP = 8
D_FFE = 7168
EPS = 1e-6
NEG_INF = -1e30

LANES = 128
SUBLANES = 8
VMEM_LIMIT = 52 * 1024 * 1024
TM = 512
N_COND = 8
MOE_TM = 512
R_MOE = 2 * T + N_EXP * MOE_TM
HY_TD = 512
HY_FC = 1024


def _cparams(sem):
    return pltpu.CompilerParams(dimension_semantics=sem, vmem_limit_bytes=VMEM_LIMIT)


def _cond_of_block(i, tm):
    nb_ctx = T_CTX // tm
    return jnp.where(i < nb_ctx, 0, 1 + (i - nb_ctx) // (L_LAT // tm))


def _rows_value(refs, i, nbc):
    if len(refs) == 1:
        return refs[0][...]
    return jnp.where(i < nbc, refs[0][...], refs[1][...])


def _rows_specs(src, tm, bw, col, clamp=None):
    nbc = T_CTX // tm
    if isinstance(src, tuple):
        return ([pl.BlockSpec((tm, bw), lambda j, i, *p: (jnp.minimum(i, nbc - 1), col(j))),
                 pl.BlockSpec((tm, bw), lambda j, i, *p: (jnp.maximum(i - nbc, 0), col(j)))], list(src))
    row = (lambda i, p: i) if clamp is None else clamp
    return [pl.BlockSpec((tm, bw), lambda j, i, *p: (row(i, p), col(j)))], [src]


def _mm_body(*refs, a_counts, dual, residual, nbc):
    it = iter(refs)
    n_a = len(a_counts)
    a_refs = [[next(it) for _ in range(c)] for c in a_counts]
    w_refs = [[next(it) for _ in range(n_a)] for _ in range(2 if dual else 1)]
    if residual:
        x_refs = [next(it) for _ in range(residual)]
        g_ref = next(it)
    o_ref = next(it)
    wb_refs = [[next(it) for _ in range(n_a)] for _ in range(2 if dual else 1)]
    i = pl.program_id(1)

    @pl.when(i == 0)
    def _():
        for ws, wbs in zip(w_refs, wb_refs):
            for w, wb in zip(ws, wbs):
                wb[...] = w[...].astype(BF16)

    def prod(wbs):
        acc = None
        for a, wb in zip(a_refs, wbs):
            p = jnp.dot(_rows_value(a, i, nbc), wb[...], preferred_element_type=F32)
            acc = p if acc is None else acc + p
        return acc

    y = prod(wb_refs[0])
    if dual:
        y = y * jax.nn.sigmoid(y) * prod(wb_refs[1])
    if residual:
        y = _rows_value(x_refs, i, nbc) + g_ref[...] * y
    o_ref[...] = y.astype(o_ref.dtype)


def _mm(a_list, w_list, w_row_blocks, *, n_cols, col_off, tn, tm, out_dtype, name, dual_w=None, residual=None):
    width = lambda a: (a[0] if isinstance(a, tuple) else a).shape[1]
    ks = [width(a) for a in a_list]
    dual = dual_w is not None
    assert n_cols % tn == 0 and col_off % tn == 0 and T % tm == 0
    cb = col_off // tn
    in_specs, ins, a_counts = [], [], []
    for a, ka in zip(a_list, ks):
        sp, ops = _rows_specs(a, tm, ka, lambda j: 0)
        in_specs += sp
        ins += ops
        a_counts.append(len(ops))
    for ws in ([w_list, dual_w] if dual else [w_list]):
        ins += list(ws)
        in_specs += [pl.BlockSpec((ka, tn), lambda j, i, rb=rb: (rb, cb + j)) for rb, ka in zip(w_row_blocks, ks)]
    n_x = 0
    if residual is not None:
        x, mods, slot = residual
        sp, ops = _rows_specs(x, tm, tn, lambda j: j)
        n_x = len(ops)
        ins += ops + [mods]
        in_specs += sp + [pl.BlockSpec((None, None, 1, tn), lambda j, i: (slot, _cond_of_block(i, tm), 0, j))]
    scratch = [pltpu.VMEM((ka, tn), BF16) for ka in ks] * (2 if dual else 1)
    body = functools.partial(_mm_body, a_counts=tuple(a_counts), dual=dual, residual=n_x, nbc=T_CTX // tm)
    return pl.pallas_call(body, grid=(n_cols // tn, T // tm), in_specs=in_specs,
                          out_specs=pl.BlockSpec((tm, tn), lambda j, i: (i, j)), scratch_shapes=scratch,
                          out_shape=jax.ShapeDtypeStruct((T, n_cols), out_dtype),
                          compiler_params=_cparams(("arbitrary", "arbitrary")), name=name)(*ins)


def _gmm_body(chg_ref, gid_ref, nact_ref, nxt_ref, a_ref, *rest, n_w, tn):
    w_refs, (o_ref, stage, wb, sem) = rest[:n_w], rest[n_w:]
    j, i = pl.program_id(0), pl.program_id(1)
    nact = nact_ref[0]

    def fetch(jj, ii):
        cols = pl.ds(pl.multiple_of(jj * tn, tn), tn)
        for m, w in enumerate(w_refs):
            pltpu.make_async_copy(w.at[gid_ref[ii], :, cols], stage.at[m], sem).start()

    refresh = (i < nact) & (chg_ref[i] != 0)

    @pl.when(refresh & (j == 0) & (i == 0))
    def _():
        fetch(0, 0)

    @pl.when(refresh)
    def _():
        for m, w in enumerate(w_refs):
            pltpu.make_async_copy(w.at[0, :, pl.ds(0, tn)], stage.at[m], sem).wait()
        for m in range(n_w):
            wb[m] = stage[m].astype(BF16)
        nxt = nxt_ref[i]

        @pl.when(nxt >= 0)
        def _():
            fetch(j, nxt)

        @pl.when((nxt < 0) & (j + 1 < pl.num_programs(0)))
        def _():
            fetch(j + 1, 0)

    @pl.when(i < nact)
    def _():
        a = a_ref[...]
        y = jnp.dot(a, wb[0], preferred_element_type=F32)
        if n_w == 2:
            y = y * jax.nn.sigmoid(y) * jnp.dot(a, wb[1], preferred_element_type=F32)
        o_ref[...] = y.astype(o_ref.dtype)

    @pl.when(i >= nact)
    def _():
        o_ref[...] = jnp.zeros_like(o_ref)


def _gmm(a, w_list, group, *, tn, tm, out_dtype, name):
    K = a.shape[1]
    n = w_list[0].shape[2]
    n_w = len(w_list)
    rowblk = lambda i, p: jnp.maximum(jnp.minimum(i, p[2][0] - 1), 0)
    gs = pltpu.PrefetchScalarGridSpec(
        num_scalar_prefetch=4, grid=(n // tn, R_MOE // tm),
        in_specs=[pl.BlockSpec((tm, K), lambda j, i, *p: (rowblk(i, p), 0))]
        + [pl.BlockSpec(memory_space=pl.ANY)] * n_w,
        out_specs=pl.BlockSpec((tm, tn), lambda j, i, *p: (i, j)),
        scratch_shapes=[pltpu.VMEM((n_w, K, tn), F32), pltpu.VMEM((n_w, K, tn), BF16), pltpu.SemaphoreType.DMA(())])
    return pl.pallas_call(functools.partial(_gmm_body, n_w=n_w, tn=tn), grid_spec=gs,
                          out_shape=jax.ShapeDtypeStruct((R_MOE, n), out_dtype),
                          compiler_params=_cparams(("arbitrary", "arbitrary")), name=name)(*group, a, *w_list)


def _ada_body(c_ref, w_ref, b_ref, o_ref):
    c = c_ref[...]
    s = (c * jax.nn.sigmoid(c)).astype(BF16)
    o_ref[...] = jnp.dot(s, w_ref[...].astype(BF16), preferred_element_type=F32) + b_ref[...]


def _ada_params(cond, w_mod, b_mod):
    depth = w_mod.shape[0]
    tn = 1024
    return pl.pallas_call(
        _ada_body, grid=(depth, 6 * D // tn),
        in_specs=[pl.BlockSpec((N_COND, D), lambda l, j: (0, 0)),
                  pl.BlockSpec((None, D, tn), lambda l, j: (l, 0, j)),
                  pl.BlockSpec((None, 1, tn), lambda l, j: (l, 0, j))],
        out_specs=pl.BlockSpec((None, N_COND, tn), lambda l, j: (l, 0, j)),
        out_shape=jax.ShapeDtypeStruct((depth, N_COND, 6 * D), F32),
        compiler_params=_cparams(("arbitrary", "arbitrary")), name="ada_params",
    )(cond, w_mod, b_mod.reshape(depth, 1, 6 * D))


def _norm_body(*refs, n_x, modulate, router):
    it = iter(refs)
    x_refs = [next(it) for _ in range(n_x)]
    g_ref = next(it)
    if modulate:
        sh_ref, sc_ref = next(it), next(it)
    if router:
        wr_ref, br_ref = next(it), next(it)
    o_ref = next(it)
    pid = pl.program_id(1)
    x = _rows_value(x_refs, pid, T_CTX // TM)
    y = x * lax.rsqrt(jnp.mean(x * x, axis=-1, keepdims=True) + EPS) * g_ref[...]
    if modulate:
        y = y * (1.0 + sc_ref[...]) + sh_ref[...]
    o_ref[...] = y.astype(o_ref.dtype)
    if router:
        idx_ref, g0_ref, g1_ref, cnt_ref, carry_ref = next(it), next(it), next(it), next(it), next(it)
        logits = jnp.dot(y, wr_ref[...], preferred_element_type=F32, precision=lax.Precision.HIGHEST) + br_ref[...]
        lane = lax.broadcasted_iota(I32, logits.shape, 1)
        logits = jnp.where(lane < N_EXP, logits, -jnp.inf)
        lanef = lane.astype(F32)
        m1 = jnp.max(logits, axis=-1, keepdims=True)
        i1 = jnp.min(jnp.where(logits == m1, lanef, float(LANES)), axis=-1, keepdims=True)
        rest = jnp.where(lanef == i1, -jnp.inf, logits)
        m2 = jnp.max(rest, axis=-1, keepdims=True)
        i2 = jnp.min(jnp.where(rest == m2, lanef, float(LANES)), axis=-1, keepdims=True)
        e21 = jnp.exp(m2 - m1)
        gate1 = 1.0 / (1.0 + e21)
        g0_ref[...] = jnp.broadcast_to(gate1, logits.shape)
        g1_ref[...] = jnp.broadcast_to(e21 * gate1, logits.shape)

        @pl.when(pid == 0)
        def _():
            carry_ref[...] = jnp.zeros_like(carry_ref)

        hot1 = jnp.where(lanef == i1, 1.0, 0.0)
        hot2 = jnp.where(lanef == i2, 1.0, 0.0)
        n = logits.shape[0]
        tri = jnp.where(lax.broadcasted_iota(I32, (n, n), 1) < lax.broadcasted_iota(I32, (n, n), 0), 1.0, 0.0)
        tri = tri.astype(BF16)
        before1 = jnp.dot(tri, hot1.astype(BF16), preferred_element_type=F32)
        before2 = jnp.dot(tri, hot2.astype(BF16), preferred_element_type=F32)
        tot1 = jnp.sum(hot1, axis=0, keepdims=True)
        tot2 = jnp.sum(hot2, axis=0, keepdims=True)
        carry = carry_ref[...]
        rank1 = jnp.sum(hot1 * (before1 + carry), axis=-1, keepdims=True)
        rank2 = jnp.sum(hot2 * (before2 + carry + tot1), axis=-1, keepdims=True)
        carry = carry + tot1 + tot2
        carry_ref[...] = carry
        cnt_ref[...] = jnp.broadcast_to(carry, cnt_ref.shape).astype(I32)
        packed = jnp.where(lane == 0, i1, jnp.where(lane == 1, i2, jnp.where(lane == 2, rank1,
                                                                             jnp.where(lane == 3, rank2, 0.0))))
        idx_ref[...] = packed.astype(I32)


def _norm(x, g, *, mods=None, slots=None, router=None, out_dtype, name, row0=0, rows=None):
    pair = isinstance(x, tuple)
    rows = (T if pair else x.shape[0]) if rows is None else rows
    rb0 = row0 // TM
    modulate = mods is not None
    in_specs, ins = _rows_specs(x, TM, D, lambda j: 0, clamp=lambda i, p: rb0 + i)
    n_x = len(ins)
    ins.append(g.reshape(1, D))
    in_specs.append(pl.BlockSpec((1, D), lambda j, i: (0, 0)))
    if modulate:
        for slot in slots:
            ins.append(mods)
            in_specs.append(pl.BlockSpec((None, None, 1, D),
                                         lambda j, i, slot=slot: (slot, _cond_of_block(i, TM), 0, 0)))
    out_shape = [jax.ShapeDtypeStruct((rows, D), out_dtype)]
    out_specs = [pl.BlockSpec((TM, D), lambda j, i: (i, 0))]
    scratch = []
    if router is not None:
        w_r, b_r = router
        ins += [jnp.pad(w_r, ((0, 0), (0, LANES - N_EXP))), jnp.pad(b_r, (0, LANES - N_EXP)).reshape(1, LANES)]
        in_specs += [pl.BlockSpec((D, LANES), lambda j, i: (0, 0)), pl.BlockSpec((1, LANES), lambda j, i: (0, 0))]
        out_shape += [jax.ShapeDtypeStruct((rows, LANES), I32), jax.ShapeDtypeStruct((rows, LANES), F32),
                      jax.ShapeDtypeStruct((rows, LANES), F32), jax.ShapeDtypeStruct((SUBLANES, LANES), I32)]
        out_specs += [pl.BlockSpec((TM, LANES), lambda j, i: (i, 0))] * 3
        out_specs += [pl.BlockSpec((SUBLANES, LANES), lambda j, i: (0, 0))]
        scratch = [pltpu.VMEM((1, LANES), F32)]
    body = functools.partial(_norm_body, n_x=n_x, modulate=modulate, router=router is not None)
    res = pl.pallas_call(body, grid=(1, rows // TM), in_specs=in_specs, out_specs=out_specs, out_shape=out_shape,
                         scratch_shapes=scratch, compiler_params=_cparams(("arbitrary", "arbitrary")), name=name)(*ins)
    return res if router is not None else res[0]


def _swap32(x):
    up = jnp.concatenate([x[:, 32:], x[:, :32]], axis=1)
    down = jnp.concatenate([x[:, 96:], x[:, :96]], axis=1)
    lane = lax.broadcasted_iota(I32, x.shape, 1)
    return jnp.where((lane % 64) < 32, up, down)


def _qkprep_body(q_ref, kv_ref, cos_ref, sin_ref, qo_ref, ko_ref, vo_ref):
    cos, sin = cos_ref[...], sin_ref[...]
    for h in range(N_HEADS):
        x = q_ref[:, h * HD:(h + 1) * HD]
        qo_ref[:, h * HD:(h + 1) * HD] = (x * cos + _swap32(x) * sin).astype(BF16)
    for h in range(N_KV):
        x = kv_ref[:, h * HD:(h + 1) * HD]
        ko_ref[:, h * HD:(h + 1) * HD] = (x * cos + _swap32(x) * sin).astype(BF16)
    vo_ref[...] = kv_ref[:, D_KV:].astype(BF16)


def _qk_prep(q, kv, cos_t, sin_t):
    return pl.pallas_call(
        _qkprep_body, grid=(T // TM,),
        in_specs=[pl.BlockSpec((TM, D_ATTN), lambda i: (i, 0)), pl.BlockSpec((TM, 2 * D_KV), lambda i: (i, 0)),
                  pl.BlockSpec((TM, HD), lambda i: (i, 0)), pl.BlockSpec((TM, HD), lambda i: (i, 0))],
        out_specs=[pl.BlockSpec((TM, D_ATTN), lambda i: (i, 0)), pl.BlockSpec((TM, D_KV), lambda i: (i, 0)),
                   pl.BlockSpec((TM, D_KV), lambda i: (i, 0))],
        out_shape=[jax.ShapeDtypeStruct((T, D_ATTN), BF16), jax.ShapeDtypeStruct((T, D_KV), BF16),
                   jax.ShapeDtypeStruct((T, D_KV), BF16)],
        compiler_params=_cparams(("arbitrary",)), name="qk_prep")(q, kv, cos_t, sin_t)


def _attn_core(q_ref, o_ref, sink_ref, kvh, kall, vall, mask):
    scale = HD ** -0.5
    for g in range(Q_PER_KV):
        head = kvh * Q_PER_KV + g
        qh = q_ref[:, head * HD:(head + 1) * HD]
        s = lax.dot_general(qh, kall, (((1,), (1,)), ((), ())), preferred_element_type=F32) * scale
        if mask is not None:
            s = jnp.where(mask, s, NEG_INF)
        sk = sink_ref[head]
        m = jnp.maximum(jnp.max(s, axis=-1, keepdims=True), sk)
        p = jnp.exp(s - m)
        denom = jnp.sum(p, axis=-1, keepdims=True) + jnp.exp(sk - m)
        o = jnp.dot(p.astype(BF16), vall, preferred_element_type=F32) / denom
        o_ref[:, head * HD:(head + 1) * HD] = o.astype(o_ref.dtype)


def _attn_ctx_body(sink_ref, q_ref, k_ref, v_ref, o_ref):
    for kvh in range(N_KV):
        cols = slice(kvh * HD, (kvh + 1) * HD)
        _attn_core(q_ref, o_ref, sink_ref, kvh, k_ref[:, cols], v_ref[:, cols], None)


def _attn_lat_body(sink_ref, q_ref, kp_ref, kc_ref, kn_ref, vp_ref, vc_ref, vn_ref, ck_ref, cv_ref, o_ref):
    qb = pl.program_id(1)
    nk = 3 * WINDOW + ck_ref.shape[0]
    qpos = qb * WINDOW + lax.broadcasted_iota(I32, (WINDOW, nk), 0)
    col = lax.broadcasted_iota(I32, (WINDOW, nk), 1)
    kpos = (qb - 1) * WINDOW + col
    in_win = (jnp.abs(kpos - qpos) <= WINDOW) & (kpos >= 0) & (kpos < L_LAT)
    mask = in_win | (col >= 3 * WINDOW)
    for kvh in range(N_KV):
        cols = slice(kvh * HD, (kvh + 1) * HD)
        kall = jnp.concatenate([kp_ref[:, cols], kc_ref[:, cols], kn_ref[:, cols], ck_ref[:, cols].astype(BF16)], axis=0)
        vall = jnp.concatenate([vp_ref[:, cols], vc_ref[:, cols], vn_ref[:, cols], cv_ref[:, cols].astype(BF16)], axis=0)
        _attn_core(q_ref, o_ref, sink_ref, kvh, kall, vall, mask)


def _attention(qb, kb, vb, cache_k, cache_v, sink):
    smem = pl.BlockSpec(memory_space=pltpu.SMEM)
    o_ctx = pl.pallas_call(
        _attn_ctx_body, grid=(B_CTX,),
        in_specs=[smem, pl.BlockSpec((L_CTX, D_ATTN), lambda b: (b, 0)),
                  pl.BlockSpec((L_CTX, D_KV), lambda b: (b, 0)), pl.BlockSpec((L_CTX, D_KV), lambda b: (b, 0))],
        out_specs=pl.BlockSpec((L_CTX, D_ATTN), lambda b: (b, 0)),
        out_shape=jax.ShapeDtypeStruct((T_CTX, D_ATTN), BF16),
        compiler_params=_cparams(("arbitrary",)), name="attn_ctx")(sink, qb, kb, vb)
    nb = L_LAT // WINDOW
    base = T_CTX // WINDOW
    cur = lambda b, i: (base + b * nb + i, 0)
    prv = lambda b, i: (base + b * nb + jnp.maximum(i - 1, 0), 0)
    nxt = lambda b, i: (base + b * nb + jnp.minimum(i + 1, nb - 1), 0)
    blk = lambda f: pl.BlockSpec((WINDOW, D_KV), f)
    cspec = pl.BlockSpec((None, cache_k.shape[1], D_KV), lambda b, i: (b, 0, 0))
    o_lat = pl.pallas_call(
        _attn_lat_body, grid=(B_LAT, nb),
        in_specs=[smem, pl.BlockSpec((WINDOW, D_ATTN), cur), blk(prv), blk(cur), blk(nxt), blk(prv), blk(cur), blk(nxt),
                  cspec, cspec],
        out_specs=pl.BlockSpec((WINDOW, D_ATTN), lambda b, i: (b * nb + i, 0)),
        out_shape=jax.ShapeDtypeStruct((T_LAT, D_ATTN), BF16),
        compiler_params=_cparams(("arbitrary", "arbitrary")), name="attn_lat",
    )(sink, qb, kb, kb, kb, vb, vb, vb, cache_k, cache_v)
    return o_ctx, o_lat


SCAN_ROWS = 2048


def _scan_pitch(L):
    return L // SUBLANES + 4


def _rglru_body(x_ref, y_ref, h0_ref, cw_ref, cb_ref, wa_ref, ba_ref, wx_ref, bx_ref, lam_ref,
                o_ref, fin_ref, a_s, b_s, p_s, h_s, *, per_chunk_seq):
    rows_total = x_ref.shape[0]
    lc = rows_total // SUBLANES
    L = lc if per_chunk_seq else rows_total
    pitch = _scan_pitch(rows_total)
    x = x_ref[...]
    row = lax.broadcasted_iota(I32, x.shape, 0)
    pos = row % L if per_chunk_seq else row
    xc = cb_ref[...] + x * cw_ref[2:3, :]
    xc = xc + jnp.where(pos >= 2, pltpu.roll(x, 2, 0), 0.0) * cw_ref[0:1, :]
    xc = xc + jnp.where(pos >= 1, pltpu.roll(x, 1, 0), 0.0) * cw_ref[1:2, :]
    xc = xc + jnp.where(pos < L - 1, pltpu.roll(x, rows_total - 1, 0), 0.0) * cw_ref[3:4, :]
    xcb = xc.astype(BF16)
    for d in range(2):
        ga = jnp.dot(xcb, wa_ref[d].astype(BF16), preferred_element_type=F32) + ba_ref[d:d + 1, :]
        gx = jnp.dot(xcb, wx_ref[d].astype(BF16), preferred_element_type=F32) + bx_ref[d:d + 1, :]
        sigmoid = lambda v: 0.5 * jnp.tanh(0.5 * v) + 0.5
        log_a = -RG_LRU_C * sigmoid(ga) * jax.nn.softplus(-lam_ref[d:d + 1, :])
        a = jnp.exp(log_a)
        b = jnp.sqrt(-jnp.tanh(log_a) * (1.0 + a * a)) * sigmoid(gx) * xc
        for s in range(SUBLANES):
            a_s[d, s * pitch:s * pitch + lc, :] = a[s * lc:(s + 1) * lc]
            b_s[d, s * pitch:s * pitch + lc, :] = b[s * lc:(s + 1) * lc]

    def step(i, carry):
        out = []
        for d, t in ((0, i), (1, lc - 1 - i)):
            rows = pl.ds(t, SUBLANES, stride=pitch)
            a = a_s[d, rows, :]
            h = a * carry[2 * d] + b_s[d, rows, :]
            p = a * carry[2 * d + 1]
            h_s[d, rows, :] = h
            p_s[d, rows, :] = p
            out += [h, p]
        return tuple(out)

    zero = jnp.zeros((SUBLANES, RNN_W), F32)
    one = jnp.ones((SUBLANES, RNN_W), F32)
    hf, pf, hb, pb = lax.fori_loop(0, lc, step, (zero, one, zero, one), unroll=4)

    r8 = lax.broadcasted_iota(I32, (SUBLANES, RNN_W), 0)

    def chunk_carry(p, h, h0, reverse):
        for k in (1, 2, 4):
            sh = SUBLANES - k if reverse else k
            m = (r8 < SUBLANES - k) if reverse else (r8 >= k)
            h = jnp.where(m, p * pltpu.roll(h, sh, 0) + h, h)
            p = jnp.where(m, p * pltpu.roll(p, sh, 0), p)
        h0 = jnp.broadcast_to(h0, (SUBLANES, RNN_W))
        state = p * h0 + h
        if reverse:
            return state, jnp.where(r8 < SUBLANES - 1, pltpu.roll(state, SUBLANES - 1, 0), h0)
        return state, jnp.where(r8 >= 1, pltpu.roll(state, 1, 0), h0)

    if per_chunk_seq:
        cf, cb = h0_ref[0], h0_ref[1]
        sf, sb = pf * cf + hf, pb * cb + hb
    else:
        sf, cf = chunk_carry(pf, hf, h0_ref[0:1, :], False)
        sb, cb = chunk_carry(pb, hb, h0_ref[1:2, :], True)

    for s in range(SUBLANES):
        rows = slice(s * pitch, s * pitch + lc)
        hsum = (h_s[0, rows, :] + p_s[0, rows, :] * cf[s:s + 1, :]) + (h_s[1, rows, :] + p_s[1, rows, :] * cb[s:s + 1, :])
        o_ref[s * lc:(s + 1) * lc, :] = (hsum * jax.nn.gelu(y_ref[s * lc:(s + 1) * lc, :])).astype(o_ref.dtype)
    if per_chunk_seq:
        fin_ref[0] = sf
        fin_ref[1] = sb
    else:
        fin_ref[0:1, :] = sf[SUBLANES - 1:SUBLANES, :]
        fin_ref[1:2, :] = sb[0:1, :]


def _rglru(xy, h0, n_seq, L, row0, conv_w, conv_b, w_a, b_a, w_x, b_x, lam, name):
    assert L in (SCAN_ROWS, SCAN_ROWS // SUBLANES)
    multi = L != SCAN_ROWS
    rb0 = row0 // SCAN_ROWS
    nb = RNN_BLOCKS
    vec = lambda r: pl.BlockSpec((r, RNN_W), lambda b, n: (0, n))
    wsp = pl.BlockSpec((2, None, RNN_W, RNN_W), lambda b, n: (0, n, 0, 0))
    if multi:
        h0 = h0.transpose(1, 0, 2)
        state = pl.BlockSpec((2, SUBLANES, RNN_W), lambda b, n: (0, b, n))
        state_shape = (2, n_seq, D_RNN)
    else:
        state = pl.BlockSpec((None, 2, RNN_W), lambda b, n: (b, 0, n))
        state_shape = (n_seq, 2, D_RNN)
    out, fin = pl.pallas_call(
        functools.partial(_rglru_body, per_chunk_seq=multi), grid=(n_seq * L // SCAN_ROWS, nb),
        in_specs=[pl.BlockSpec((SCAN_ROWS, RNN_W), lambda b, n: (rb0 + b, n)),
                  pl.BlockSpec((SCAN_ROWS, RNN_W), lambda b, n: (rb0 + b, nb + n)),
                  state, vec(4), vec(1), wsp, vec(2), wsp, vec(2), vec(2)],
        out_specs=[pl.BlockSpec((SCAN_ROWS, RNN_W), lambda b, n: (b, n)), state],
        out_shape=[jax.ShapeDtypeStruct((n_seq * L, D_RNN), BF16), jax.ShapeDtypeStruct(state_shape, F32)],
        scratch_shapes=[pltpu.VMEM((2, SUBLANES * _scan_pitch(SCAN_ROWS), RNN_W), F32)] * 4,
        compiler_params=_cparams(("arbitrary", "arbitrary")), name=name,
    )(xy, xy, h0, conv_w, conv_b.reshape(1, D_RNN), w_a, b_a, w_x, b_x, lam)
    return out, (fin.transpose(1, 0, 2) if multi else fin)


def _hy_filter_body(fv_ref, w1_ref, b1_ref, w2_ref, b2_ref, fr_ref, w3_ref, dl_ref, o_ref, hid_s, tt_s, *, L):
    hp = lax.Precision.HIGHEST
    rowi = lax.broadcasted_iota(I32, (L, LANES), 0)
    lane = lax.broadcasted_iota(I32, (L, LANES), 1)

    def features(pos):
        posf = pos.astype(F32)
        tt = posf / (L - 1)
        ang = fv_ref[...] * (2.0 * math.pi * posf / L)
        z = jnp.where(lane == 0, tt, jnp.where(lane <= 16, jnp.cos(ang), jnp.where(lane <= 32, -jnp.sin(ang), 0.0)))
        return z, tt[:, 0:1]

    is_bwd = pl.program_id(0) == 1

    @pl.when(pl.program_id(1) == 0)
    def _():
        z, tt = features(jnp.where(is_bwd, L - rowi, rowi))
        h = jnp.sin(fr_ref[0:1, :] * (jnp.dot(z, w1_ref[...], preferred_element_type=F32, precision=hp) + b1_ref[...]))
        h = jnp.sin(fr_ref[1:2, :] * (jnp.dot(h, w2_ref[...], preferred_element_type=F32, precision=hp) + b2_ref[...]))
        hid_s[...] = h
        tt_s[...] = jnp.broadcast_to(tt, tt_s.shape)

    filt = jnp.dot(hid_s[...], w3_ref[...], preferred_element_type=F32, precision=hp)
    filt = filt * jnp.exp(-tt_s[:, 0:1] * dl_ref[...])
    dead = is_bwd & (lax.broadcasted_iota(I32, filt.shape, 0) == 0)
    o_ref[...] = jnp.where(dead, 0.0, filt).astype(o_ref.dtype)


def _hy_filter(L, f_w1, f_b1, f_w2, f_b2, f_freq, f_w3):
    bands = (HY_EMB - 1) // 2
    f = jnp.linspace(1e-4, bands - 1, bands, dtype=F32)
    fv = jnp.zeros((LANES,), F32).at[1:1 + bands].set(f).at[1 + bands:1 + 2 * bands].set(f).reshape(1, LANES)
    padw = lambda w, r, c: jnp.pad(w.astype(F32), ((0, r - w.shape[0]), (0, c - w.shape[1])))
    padv = lambda v: jnp.pad(v.astype(F32), (0, LANES - v.shape[0])).reshape(1, LANES)
    w1, w2 = padw(f_w1, LANES, LANES), padw(f_w2, LANES, LANES)
    w3 = padw(f_w3, LANES, f_w3.shape[1])
    fr = jnp.pad(f_freq.astype(F32), ((0, 0), (0, LANES - HY_W)))
    deltas = jnp.abs(jnp.linspace(HY_MIN_DECAY, HY_MAX_DECAY, D, dtype=F32)).reshape(1, D)
    tn = 1024
    per = D // tn
    full = lambda r: pl.BlockSpec((r, LANES), lambda d, j: (0, 0))
    col = lambda d, j: ((j // per) * 2 + d) * per + j % per
    return pl.pallas_call(
        functools.partial(_hy_filter_body, L=L), grid=(2, 2 * per),
        in_specs=[full(1), full(LANES), full(1), full(LANES), full(1), full(2),
                  pl.BlockSpec((LANES, tn), lambda d, j: (0, col(d, j))),
                  pl.BlockSpec((1, tn), lambda d, j: (0, j % per))],
        out_specs=pl.BlockSpec((L, tn), lambda d, j: (0, col(d, j))),
        out_shape=jax.ShapeDtypeStruct((L, 4 * D), BF16),
        scratch_shapes=[pltpu.VMEM((L, LANES), F32), pltpu.VMEM((L, LANES), F32)],
        compiler_params=_cparams(("arbitrary", "arbitrary")), name=f"hy_filter_{L}",
    )(fv, w1, padv(f_b1), w2, padv(f_b2), fr, w3, deltas)


DFT_RB = 64


def _dft_body(ca_ref, sa_ref, cb_ref, sb_ref, cf_ref, ci_ref, *, L):
    n = 2 * L
    ca, sa, cb, sb = ca_ref[...], sa_ref[...], cb_ref[...], sb_ref[...]
    cos = ca * cb - sa * sb
    sin = sa * cb + ca * sb
    r = pl.program_id(0) * DFT_RB + lax.broadcasted_iota(I32, cos.shape, 0)
    j = lax.broadcasted_iota(I32, cos.shape, 1)
    alt_j = jnp.where(j % 2 == 0, 1.0, -1.0)
    alt_r = jnp.where(r % 2 == 0, 1.0, -1.0)
    cf_ref[0] = cos.astype(BF16)
    cf_ref[1] = jnp.where(r == 0, alt_j, -sin).astype(BF16)
    w = jnp.where(j == 0, 1.0, 2.0) / n
    ci_ref[0] = (w * cos).astype(BF16)
    ci_ref[1] = jnp.where(j == 0, alt_r / n, -w * sin).astype(BF16)


def _dft_mats(L):
    n = 2 * L
    j = jnp.arange(L, dtype=I32)[None, :]
    ang = lambda r: ((r[:, None] * j) % n).astype(F32) * (2.0 * math.pi / n)
    ang_a = ang(jnp.arange(L // DFT_RB, dtype=I32) * DFT_RB)
    ang_b = ang(jnp.arange(DFT_RB, dtype=I32))
    row = pl.BlockSpec((None, 1, L), lambda a: (a, 0, 0))
    full = pl.BlockSpec((DFT_RB, L), lambda a: (0, 0))
    out = pl.BlockSpec((2, DFT_RB, L), lambda a: (0, a, 0))
    return pl.pallas_call(
        functools.partial(_dft_body, L=L), grid=(L // DFT_RB,), in_specs=[row, row, full, full],
        out_specs=[out, out], out_shape=[jax.ShapeDtypeStruct((2, L, L), BF16)] * 2,
        compiler_params=_cparams(("arbitrary",)), name=f"dft_tables_{L}",
    )(jnp.cos(ang_a)[:, None, :], jnp.sin(ang_a)[:, None, :], jnp.cos(ang_b), jnp.sin(ang_b))


def _spec_body(cf_ref, c1_ref, c2_ref, o_ref):
    f = pl.program_id(1)
    h = cf_ref.shape[1]
    k = f * h + lax.broadcasted_iota(I32, (h, c1_ref.shape[1]), 0)
    odd = k % 2 == 1
    for part in range(2):
        z1 = jnp.dot(cf_ref[part], c1_ref[...], preferred_element_type=F32)
        z2 = jnp.dot(cf_ref[part], c2_ref[...], preferred_element_type=F32)
        flip = odd if part == 0 else odd & (k != 0)
        o_ref[part * h:(part + 1) * h, :] = z1 + jnp.where(flip, -z2, z2)


def _hy_spectra(filt, cf, h):
    L = cf.shape[1]
    nf = L // h
    td = HY_TD
    per = D // td
    return pl.pallas_call(
        _spec_body, grid=(2 * per, nf),
        in_specs=[pl.BlockSpec((2, h, L), lambda c, f: (0, f, 0)),
                  pl.BlockSpec((L, td), lambda c, f: (0, (c // per) * 2 * per + c % per)),
                  pl.BlockSpec((L, td), lambda c, f: (0, (c // per) * 2 * per + per + c % per))],
        out_specs=pl.BlockSpec((None, 2 * h, td), lambda c, f: (f, 0, c)),
        out_shape=jax.ShapeDtypeStruct((nf, 2 * h, 2 * D), F32),
        compiler_params=_cparams(("arbitrary", "arbitrary")), name=f"hy_spectra_{L}")(cf, filt, filt)


HY_TC = 512
HY_TR = 2048


def _hy_conv(u_ref, cw_ref, cb_ref, L):
    u = u_ref[...]
    rows = u.shape[0]
    pos = lax.broadcasted_iota(I32, u.shape, 0) % L
    uc = cb_ref[...] + u * cw_ref[1:2, :]
    uc = uc + jnp.where(pos >= 1, pltpu.roll(u, 1, 0), 0.0) * cw_ref[0:1, :]
    return uc + jnp.where(pos < L - 1, pltpu.roll(u, rows - 1, 0), 0.0) * cw_ref[2:3, :]


def _hy_third_specs(third, row0, L):
    tr = max(L, HY_TR)
    per = D // HY_TC
    col = lambda i, c: third * per + c
    return [pl.BlockSpec((tr, HY_TC), lambda i, c: (row0 // tr + i, col(i, c))),
            pl.BlockSpec((3, HY_TC), lambda i, c: (0, col(i, c))),
            pl.BlockSpec((1, HY_TC), lambda i, c: (0, col(i, c)))]


def _hy_prep_body(u_ref, cw_ref, cb_ref, zb_ref, *, L):
    zb_ref[...] = _hy_conv(u_ref, cw_ref, cb_ref, L).astype(BF16)


def _hy_prep(u, conv_w, conv_b, row0, rows, L):
    tr = max(L, HY_TR)
    return pl.pallas_call(
        functools.partial(_hy_prep_body, L=L), grid=(rows // tr, D // HY_TC),
        in_specs=_hy_third_specs(0, row0, L), out_specs=pl.BlockSpec((tr, HY_TC), lambda i, c: (i, c)),
        out_shape=jax.ShapeDtypeStruct((rows, D), BF16),
        compiler_params=_cparams(("arbitrary", "arbitrary")), name=f"hy_prep_{L}",
    )(u, conv_w, conv_b.reshape(1, 3 * D))


def _longconv_body(z_ref, cf_ref, ci_ref, s_ref, o_ref, acc_ref):
    f = pl.program_id(2)
    h = cf_ref.shape[1]
    z = z_ref[...]
    zre = jnp.dot(cf_ref[0], z, preferred_element_type=F32)
    zim = jnp.dot(cf_ref[1], z, preferred_element_type=F32)
    sre, sim = s_ref[:h, :], s_ref[h:, :]
    first = (lax.broadcasted_iota(I32, zre.shape, 0) == 0) & (f == 0)
    yre = zre * sre - jnp.where(first, 0.0, zim * sim)
    yim = jnp.where(first, zim * sim, zre * sim + zim * sre)
    contrib = (jnp.dot(ci_ref[0], yre.astype(BF16), preferred_element_type=F32)
               + jnp.dot(ci_ref[1], yim.astype(BF16), preferred_element_type=F32))

    @pl.when(f == 0)
    def _():
        acc_ref[...] = contrib

    @pl.when(f > 0)
    def _():
        acc_ref[...] += contrib

    @pl.when(f == pl.num_programs(2) - 1)
    def _():
        o_ref[...] = acc_ref[...]


def _longconv(z, n_seq, L, cf, ci, spec, order):
    nf, fc, _ = spec.shape
    h = fc // 2
    td = HY_TD if L > TM else D
    per = D // td
    return pl.pallas_call(
        _longconv_body, grid=(n_seq, per, nf),
        in_specs=[pl.BlockSpec((L, td), lambda b, c, f: (b, c)),
                  pl.BlockSpec((2, h, L), lambda b, c, f: (0, f, 0)),
                  pl.BlockSpec((2, L, h), lambda b, c, f: (0, 0, f)),
                  pl.BlockSpec((None, fc, td), lambda b, c, f: (f, 0, order * per + c))],
        out_specs=pl.BlockSpec((L, td), lambda b, c, f: (b, c)),
        out_shape=jax.ShapeDtypeStruct((n_seq * L, D), F32),
        scratch_shapes=[pltpu.VMEM((L, td), F32)],
        compiler_params=_cparams(("arbitrary", "arbitrary", "arbitrary")), name=f"longconv_{L}_{order}",
    )(z, cf, ci, spec)


def _hy_gate_body(zc_ref, b_ref, *refs, L, first):
    if first:
        z = _hy_conv(*refs[0:3], L)
        gate_refs, out_refs = refs[3:6], refs[6:]
    else:
        z = refs[0][...]
        gate_refs, out_refs = refs[1:4], refs[4:]
    z = _hy_conv(*gate_refs, L) * (zc_ref[...] + z * b_ref[...])
    for o_ref in out_refs:
        o_ref[...] = z.astype(o_ref.dtype)


def _hy_gate(zc, z, bias, u, conv_w, conv_b, order, row0, L):
    rows = zc.shape[0]
    tr = max(L, HY_TR)
    blk = pl.BlockSpec((tr, HY_TC), lambda i, c: (i, c))
    first = z is None
    uargs = (u, conv_w, conv_b.reshape(1, 3 * D))
    ins = [zc, bias.reshape(1, D)] + (list(uargs) if first else [z]) + list(uargs)
    in_specs = ([blk, pl.BlockSpec((1, HY_TC), lambda i, c: (0, c))]
                + (_hy_third_specs(0, row0, L) if first else [blk]) + _hy_third_specs(order + 1, row0, L))
    dts = ([F32] if first else []) + [BF16]
    return pl.pallas_call(
        functools.partial(_hy_gate_body, L=L, first=first), grid=(rows // tr, D // HY_TC),
        in_specs=in_specs, out_specs=[blk] * len(dts),
        out_shape=[jax.ShapeDtypeStruct((rows, D), dt) for dt in dts],
        compiler_params=_cparams(("arbitrary", "arbitrary")), name=f"hy_gate_{L}_{order}",
    )(*ins)


DISPATCH_ROWS = 512
COMBINE_ROWS = 256
DMA_UNROLL = 8


def _dispatch_body(d0_ref, d1_ref, zs_ref, h_ref, o_ref, zbuf, sem, zsem):
    base = pl.program_id(0) * DISPATCH_ROWS

    @pl.when(pl.program_id(0) == 0)
    def _():
        zbuf[...] = jnp.zeros_like(zbuf)
        for k in range(2 * N_EXP):
            fill = pltpu.make_async_copy(zbuf, o_ref.at[pl.ds(zs_ref[k], MOE_TM)], zsem)
            fill.start()
            fill.wait()

    def issue(q, c):
        for u in range(DMA_UNROLL):
            r = q * DMA_UNROLL + u
            pltpu.make_async_copy(h_ref.at[r], o_ref.at[d0_ref[base + r]], sem).start(priority=0)
            pltpu.make_async_copy(h_ref.at[r], o_ref.at[d1_ref[base + r]], sem).start(priority=1)
        return c

    lax.fori_loop(0, DISPATCH_ROWS // DMA_UNROLL, issue, 0)
    for _ in range(2):
        pltpu.make_async_copy(h_ref, h_ref, sem).wait()


def _moe_dispatch(h, dest0, dest1, zero_starts):
    sub = D // LANES
    gs = pltpu.PrefetchScalarGridSpec(
        num_scalar_prefetch=3, grid=(T // DISPATCH_ROWS,),
        in_specs=[pl.BlockSpec((DISPATCH_ROWS, sub, LANES), lambda i, *p: (i, 0, 0))],
        out_specs=pl.BlockSpec(memory_space=pl.ANY),
        scratch_shapes=[pltpu.VMEM((MOE_TM, sub, LANES), BF16), pltpu.SemaphoreType.DMA(()),
                        pltpu.SemaphoreType.DMA(())])
    out = pl.pallas_call(_dispatch_body, grid_spec=gs, out_shape=jax.ShapeDtypeStruct((R_MOE, sub, LANES), BF16),
                         compiler_params=_cparams(("arbitrary",)), name="moe_dispatch",
                         )(dest0, dest1, zero_starts, h.reshape(T, sub, LANES))
    return out.reshape(R_MOE, D)


def _combine_body(d0_ref, d1_ref, y_ref, x_ref, g0_ref, g1_ref, gate_ref, fg_ref, oc_ref, ol_ref, buf, sems):
    i = pl.program_id(0)

    def issue(step, slot):
        base = step * COMBINE_ROWS

        def body(q, c):
            for u in range(DMA_UNROLL):
                r = q * DMA_UNROLL + u
                pltpu.make_async_copy(y_ref.at[pl.ds(d0_ref[base + r], 1), :], buf.at[slot, 0, pl.ds(r, 1), :],
                                      sems.at[slot]).start(priority=0)
                pltpu.make_async_copy(y_ref.at[pl.ds(d1_ref[base + r], 1), :], buf.at[slot, 1, pl.ds(r, 1), :],
                                      sems.at[slot]).start(priority=1)
            return c

        lax.fori_loop(0, COMBINE_ROWS // DMA_UNROLL, body, 0)

    slot = i % 2

    @pl.when(i == 0)
    def _():
        issue(0, 0)

    @pl.when(i + 1 < pl.num_programs(0))
    def _():
        issue(i + 1, 1 - slot)

    pltpu.make_async_copy(buf.at[slot], buf.at[slot], sems.at[slot]).wait()
    y = g0_ref[:, 0:1] * buf[slot, 0] + g1_ref[:, 0:1] * buf[slot, 1]
    x = x_ref[...] + gate_ref[...] * y
    out = x * lax.rsqrt(jnp.mean(x * x, axis=-1, keepdims=True) + EPS) * fg_ref[...]
    nbc = T_CTX // COMBINE_ROWS

    @pl.when(i < nbc)
    def _():
        oc_ref[...] = out

    @pl.when(i >= nbc)
    def _():
        ol_ref[...] = out


def _moe_combine_norm(ybuf, dest0, dest1, g0, g1, x, mods, slot, final_g):
    nbc = T_CTX // COMBINE_ROWS
    tok = pl.BlockSpec((COMBINE_ROWS, D), lambda i, *p: (i, 0))
    gsp = pl.BlockSpec((COMBINE_ROWS, LANES), lambda i, *p: (i, 0))
    gs = pltpu.PrefetchScalarGridSpec(
        num_scalar_prefetch=2, grid=(T // COMBINE_ROWS,),
        in_specs=[pl.BlockSpec(memory_space=pl.ANY), tok, gsp, gsp,
                  pl.BlockSpec((None, None, 1, D), lambda i, *p: (slot, _cond_of_block(i, COMBINE_ROWS), 0, 0)),
                  pl.BlockSpec((1, D), lambda i, *p: (0, 0))],
        out_specs=[pl.BlockSpec((COMBINE_ROWS, D), lambda i, *p: (jnp.minimum(i, nbc - 1), 0)),
                   pl.BlockSpec((COMBINE_ROWS, D), lambda i, *p: (jnp.maximum(i - nbc, 0), 0))],
        scratch_shapes=[pltpu.VMEM((2, 2, COMBINE_ROWS, D), F32), pltpu.SemaphoreType.DMA((2,))])
    return pl.pallas_call(_combine_body, grid_spec=gs,
                          out_shape=[jax.ShapeDtypeStruct((T_CTX, D), F32), jax.ShapeDtypeStruct((T_LAT, D), F32)],
                          compiler_params=_cparams(("arbitrary",)), name="moe_combine_norm",
                          )(dest0, dest1, ybuf, x, g0, g1, mods, final_g.reshape(1, D))


def _moe_plan(idx, cnt):
    counts = cnt[0, :N_EXP]
    padded = (counts + MOE_TM - 1) // MOE_TM * MOE_TM
    p_ends = jnp.cumsum(padded)
    p_starts = p_ends - padded
    experts = jnp.arange(N_EXP, dtype=I32)[None, :]

    def dest(e, rank):
        return jnp.sum(jnp.where(e[:, None] == experts, p_starts[None, :], 0), axis=1) + rank

    def groups(tm):
        nblk = R_MOE // tm
        blk_start = jnp.arange(nblk, dtype=I32) * tm
        gid = jnp.minimum(jnp.sum((blk_start[:, None] >= p_ends[None, :]).astype(I32), axis=1), N_EXP - 1)
        nact = (p_ends[-1] // tm).astype(I32).reshape(1)
        gid = jnp.where(jnp.arange(nblk) < nact[0], gid, gid[jnp.maximum(nact[0] - 1, 0)])
        chg = jnp.concatenate([jnp.ones((1,), I32), (gid[1:] != gid[:-1]).astype(I32)])
        blk = jnp.arange(nblk, dtype=I32)
        starts = (chg == 1) & (blk < nact[0])
        later = jnp.where(starts[None, :] & (blk[None, :] > blk[:, None]), blk[None, :], nblk)
        nxt = jnp.min(later, axis=1)
        return chg, gid, nact, jnp.where(nxt == nblk, -1, nxt).astype(I32)

    tail = jnp.minimum(p_ends[-1] + jnp.arange(N_EXP, dtype=I32) * MOE_TM, R_MOE - MOE_TM)
    zero_starts = jnp.concatenate([jnp.maximum(p_ends - MOE_TM, 0), tail]).astype(I32)
    return dest(idx[:, 0], idx[:, 2]), dest(idx[:, 1], idx[:, 3]), groups, zero_starts


def _rope_tables():
    quarter = HD // 4
    inv_freq = ROPE_BASE ** (-jnp.arange(quarter, dtype=F32) / quarter)
    t = jnp.arange(L_LAT)
    row = (t // GRID_W).astype(F32)[:, None] * inv_freq
    col = (t % GRID_W).astype(F32)[:, None] * inv_freq
    ang = jnp.concatenate([row, row, col, col], axis=1)
    sign = jnp.tile(jnp.concatenate([-jnp.ones((quarter,), F32), jnp.ones((quarter,), F32)]), 2)
    cos = jnp.concatenate([jnp.ones((T_CTX, HD), F32), jnp.tile(jnp.cos(ang), (B_LAT, 1))], axis=0)
    sin = jnp.concatenate([jnp.zeros((T_CTX, HD), F32), jnp.tile(jnp.sin(ang) * sign, (B_LAT, 1))], axis=0)
    return cos, sin


def kernel(x_prompt, x_sample, cache_k, cache_v, state_rglru, c, c_ctx, w_mod, b_mod, norm_g, final_g, a_w_in, a_w_out, rnn_conv_w, rnn_conv_b, rnn_w_a, rnn_b_a, rnn_w_x, rnn_b_x, rnn_lam, attn_sink, ffn_w1, ffn_w3, ffn_w2, h_w_in, h_w_out, h_conv_w, h_conv_b, hf_w1, hf_b1, hf_w2, hf_b2, hf_freq, hf_w3, h_bias, moe_router, moe_router_b, moe_w_gate, moe_w_up, moe_w_down):
    x = (x_prompt.reshape(T_CTX, D), x_sample.reshape(T_LAT, D))
    cond = jnp.concatenate([c_ctx[None, :], c, jnp.zeros((N_COND - 1 - B_LAT, D), F32)], axis=0)
    mods_all = _ada_params(cond, w_mod, b_mod)
    mods_all = mods_all.reshape(-1, N_COND, 6, D).transpose(0, 2, 1, 3).reshape(-1, 6, N_COND, 1, D)
    cos_t, sin_t = _rope_tables()

    mods = mods_all[0]
    h = _norm(x, norm_g[0, 0], mods=mods, slots=(0, 1), out_dtype=BF16, name="norm_mix0")
    w_in = a_w_in[0]
    mm1 = functools.partial(_mm, [h], [w_in], [0], tm=2 * TM)
    q = mm1(n_cols=D_ATTN, col_off=0, tn=1024, out_dtype=F32, name="proj_q")
    kv = mm1(n_cols=2 * D_KV, col_off=D_ATTN, tn=512, out_dtype=F32, name="proj_kv")
    xy = _mm([h], [w_in[:, D_ATTN + 2 * D_KV:]], [0], n_cols=2 * D_RNN, col_off=0, tn=1024, tm=2 * TM,
             out_dtype=F32, name="proj_rnn")
    qb, kb, vb = _qk_prep(q, kv, cos_t, sin_t)
    ck = cache_k[:, 0].reshape(B_LAT, -1, D_KV)
    cv = cache_v[:, 0].reshape(B_LAT, -1, D_KV)
    o_ctx, o_lat = _attention(qb, kb, vb, ck, cv, attn_sink[0])
    rnn_w = (rnn_conv_w[0], rnn_conv_b[0], rnn_w_a[0], rnn_b_a[0], rnn_w_x[0], rnn_b_x[0], rnn_lam[0])
    r_ctx, s_ctx = _rglru(xy, jnp.zeros((B_CTX, 2, D_RNN), F32), B_CTX, L_CTX, 0, *rnn_w, name="rglru_ctx")
    r_lat, _ = _rglru(xy, state_rglru[:, 0], B_LAT, L_LAT, T_CTX, *rnn_w, name="rglru_lat")
    w_out = a_w_out[0]
    x = _mm([(o_ctx, o_lat), (r_ctx, r_lat)], [w_out, w_out], [0, 1], n_cols=D, col_off=0, tn=1024, tm=TM,
            out_dtype=F32, residual=(x, mods, 2), name="proj_out0")
    h = _norm(x, norm_g[0, 1], mods=mods, slots=(3, 4), out_dtype=BF16, name="norm_ffn0")
    hid = _mm([h], [ffn_w1[0]], [0], dual_w=[ffn_w3[0]], n_cols=D_FF, col_off=0, tn=512, tm=2 * TM,
              out_dtype=BF16, name="ffn_up")
    x = _mm([hid], [ffn_w2[0]], [0], n_cols=D, col_off=0, tn=512, tm=TM, out_dtype=F32, residual=(x, mods, 5),
            name="ffn_down")

    mods = mods_all[1]
    h = _norm(x, norm_g[1, 0], mods=mods, slots=(0, 1), out_dtype=BF16, name="norm_mix1")
    u = _mm([h], [h_w_in[0]], [0], n_cols=3 * D, col_off=0, tn=1024, tm=2 * TM, out_dtype=F32, name="hy_in")
    zs = []
    for row0, n_seq, L in ((0, B_CTX, L_CTX), (T_CTX, B_LAT, L_LAT)):
        zb = _hy_prep(u, h_conv_w[0], h_conv_b[0], row0, n_seq * L, L)
        cf, ci = _dft_mats(L)
        filt = _hy_filter(L, hf_w1[0], hf_b1[0], hf_w2[0], hf_b2[0], hf_freq[0], hf_w3[0])
        spec = _hy_spectra(filt, cf, min(HY_FC // 2, L))
        zc = _longconv(zb, n_seq, L, cf, ci, spec, 0)
        zf, zb = _hy_gate(zc, None, h_bias[0, 0], u, h_conv_w[0], h_conv_b[0], 0, row0, L)
        zc = _longconv(zb, n_seq, L, cf, ci, spec, 1)
        (zb,) = _hy_gate(zc, zf, h_bias[0, 1], u, h_conv_w[0], h_conv_b[0], 1, row0, L)
        zs.append(zb)
    x = _mm([tuple(zs)], [h_w_out[0]], [0], n_cols=D, col_off=0, tn=1024, tm=TM, out_dtype=F32,
            residual=(x, mods, 2), name="hy_out")
    h, idx, g0, g1, cnt = _norm(x, norm_g[1, 1], mods=mods, slots=(3, 4), router=(moe_router[0], moe_router_b[0]),
                                out_dtype=BF16, name="norm_moe")
    dest0, dest1, groups, zero_starts = _moe_plan(idx, cnt)
    xs = _moe_dispatch(h, dest0, dest1, zero_starts)
    group = groups(MOE_TM)
    hid = _gmm(xs, [moe_w_gate[0], moe_w_up[0]], group, tn=1024, tm=MOE_TM, out_dtype=BF16, name="moe_up")
    ybuf = _gmm(hid, [moe_w_down[0]], group, tn=512, tm=MOE_TM, out_dtype=F32, name="moe_down")
    y_prompt, y_sample = _moe_combine_norm(ybuf, dest0, dest1, g0, g1, x, mods, 5, final_g)
    y_prompt = y_prompt.reshape(B_CTX, L_CTX, D)
    y_sample = y_sample.reshape(B_LAT, L_LAT, D)
    new_k = kv[:T_CTX, :D_KV].reshape(B_CTX, 1, L_CTX, N_KV, HD)
    new_v = kv[:T_CTX, D_KV:].reshape(B_CTX, 1, L_CTX, N_KV, HD)
    new_s = s_ctx.reshape(B_CTX, 1, 2, D_RNN)
    return (y_prompt, y_sample, new_k, new_v, new_s)
```

```python
import functools
import math

import jax
import jax.numpy as jnp
from jax import lax
from jax.experimental import pallas as pl
from jax.experimental.pallas import tpu as pltpu

F32 = jnp.float32
BF16 = jnp.bfloat16
I32 = jnp.int32

D = 2048
B_CTX, L_CTX = 32, 256
B_LAT, L_LAT = 4, 2048
T_CTX = B_CTX * L_CTX
T_LAT = B_LAT * L_LAT
T = T_CTX + T_LAT
GRID_W = 64
N_HEADS, N_KV, HD = 8, 2, 128
Q_PER_KV = N_HEADS // N_KV
D_ATTN = N_HEADS * HD
D_KV = N_KV * HD
WINDOW = 128
ROPE_BASE = 10000.0
D_RNN = D // 2
RNN_BLOCKS = 8
RNN_W = D_RNN // RNN_BLOCKS
RG_LRU_C = 8.0
HY_EMB = 33
HY_W = 64
HY_MIN_DECAY = math.log(1e-2) / 1.5
HY_MAX_DECAY = math.log(1e-2) / 0.3
D_FF = 5632
N_EXP = 8
D_FFE = 7168
EPS = 1e-6
NEG_INF = -1e30

LANES = 128
SUBLANES = 8
VMEM_LIMIT = 52 * 1024 * 1024
TM = 512
N_COND = 8
MOE_TM = 512
R_MOE = 2 * T + N_EXP * MOE_TM
HY_TD = 512
HY_FC = 1024


def _cparams(sem):
    return pltpu.CompilerParams(dimension_semantics=sem, vmem_limit_bytes=VMEM_LIMIT)


def _cond_of_block(i, tm):
    nb_ctx = T_CTX // tm
    return jnp.where(i < nb_ctx, 0, 1 + (i - nb_ctx) // (L_LAT // tm))


def _rows_value(refs, i, nbc):
    if len(refs) == 1:
        return refs[0][...]
    return jnp.where(i < nbc, refs[0][...], refs[1][...])


def _rows_specs(src, tm, bw, col, clamp=None):
    nbc = T_CTX // tm
    if isinstance(src, tuple):
        return ([pl.BlockSpec((tm, bw), lambda j, i, *p: (jnp.minimum(i, nbc - 1), col(j))),
                 pl.BlockSpec((tm, bw), lambda j, i, *p: (jnp.maximum(i - nbc, 0), col(j)))], list(src))
    row = (lambda i, p: i) if clamp is None else clamp
    return [pl.BlockSpec((tm, bw), lambda j, i, *p: (row(i, p), col(j)))], [src]


def _mm_body(*refs, a_counts, dual, residual, nbc):
    it = iter(refs)
    n_a = len(a_counts)
    a_refs = [[next(it) for _ in range(c)] for c in a_counts]
    w_refs = [[next(it) for _ in range(n_a)] for _ in range(2 if dual else 1)]
    if residual:
        x_refs = [next(it) for _ in range(residual)]
        g_ref = next(it)
    o_ref = next(it)
    wb_refs = [[next(it) for _ in range(n_a)] for _ in range(2 if dual else 1)]
    i = pl.program_id(1)

    @pl.when(i == 0)
    def _():
        for ws, wbs in zip(w_refs, wb_refs):
            for w, wb in zip(ws, wbs):
                wb[...] = w[...].astype(BF16)

    def prod(wbs):
        acc = None
        for a, wb in zip(a_refs, wbs):
            p = jnp.dot(_rows_value(a, i, nbc), wb[...], preferred_element_type=F32)
            acc = p if acc is None else acc + p
        return acc

    y = prod(wb_refs[0])
    if dual:
        y = y * jax.nn.sigmoid(y) * prod(wb_refs[1])
    if residual:
        y = _rows_value(x_refs, i, nbc) + g_ref[...] * y
    o_ref[...] = y.astype(o_ref.dtype)


def _mm(a_list, w_list, w_row_blocks, *, n_cols, col_off, tn, tm, out_dtype, name, dual_w=None, residual=None):
    width = lambda a: (a[0] if isinstance(a, tuple) else a).shape[1]
    ks = [width(a) for a in a_list]
    dual = dual_w is not None
    assert n_cols % tn == 0 and col_off % tn == 0 and T % tm == 0
    cb = col_off // tn
    in_specs, ins, a_counts = [], [], []
    for a, ka in zip(a_list, ks):
        sp, ops = _rows_specs(a, tm, ka, lambda j: 0)
        in_specs += sp
        ins += ops
        a_counts.append(len(ops))
    for ws in ([w_list, dual_w] if dual else [w_list]):
        ins += list(ws)
        in_specs += [pl.BlockSpec((ka, tn), lambda j, i, rb=rb: (rb, cb + j)) for rb, ka in zip(w_row_blocks, ks)]
    n_x = 0
    if residual is not None:
        x, mods, slot = residual
        sp, ops = _rows_specs(x, tm, tn, lambda j: j)
        n_x = len(ops)
        ins += ops + [mods]
        in_specs += sp + [pl.BlockSpec((None, None, 1, tn), lambda j, i: (slot, _cond_of_block(i, tm), 0, j))]
    scratch = [pltpu.VMEM((ka, tn), BF16) for ka in ks] * (2 if dual else 1)
    body = functools.partial(_mm_body, a_counts=tuple(a_counts), dual=dual, residual=n_x, nbc=T_CTX // tm)
    return pl.pallas_call(body, grid=(n_cols // tn, T // tm), in_specs=in_specs,
                          out_specs=pl.BlockSpec((tm, tn), lambda j, i: (i, j)), scratch_shapes=scratch,
                          out_shape=jax.ShapeDtypeStruct((T, n_cols), out_dtype),
                          compiler_params=_cparams(("arbitrary", "arbitrary")), name=name)(*ins)


def _gmm_body(chg_ref, gid_ref, nact_ref, nxt_ref, a_ref, *rest, n_w, tn):
    w_refs, (o_ref, stage, wb, sem) = rest[:n_w], rest[n_w:]
    j, i = pl.program_id(0), pl.program_id(1)
    nact = nact_ref[0]

    def fetch(jj, ii):
        cols = pl.ds(pl.multiple_of(jj * tn, tn), tn)
        for m, w in enumerate(w_refs):
            pltpu.make_async_copy(w.at[gid_ref[ii], :, cols], stage.at[m], sem).start()

    refresh = (i < nact) & (chg_ref[i] != 0)

    @pl.when(refresh & (j == 0) & (i == 0))
    def _():
        fetch(0, 0)

    @pl.when(refresh)
    def _():
        for m, w in enumerate(w_refs):
            pltpu.make_async_copy(w.at[0, :, pl.ds(0, tn)], stage.at[m], sem).wait()
        for m in range(n_w):
            wb[m] = stage[m].astype(BF16)
        nxt = nxt_ref[i]

        @pl.when(nxt >= 0)
        def _():
            fetch(j, nxt)

        @pl.when((nxt < 0) & (j + 1 < pl.num_programs(0)))
        def _():
            fetch(j + 1, 0)

    @pl.when(i < nact)
    def _():
        a = a_ref[...]
        y = jnp.dot(a, wb[0], preferred_element_type=F32)
        if n_w == 2:
            y = y * jax.nn.sigmoid(y) * jnp.dot(a, wb[1], preferred_element_type=F32)
        o_ref[...] = y.astype(o_ref.dtype)

    @pl.when(i >= nact)
    def _():
        o_ref[...] = jnp.zeros_like(o_ref)


def _gmm(a, w_list, group, *, tn, tm, out_dtype, name):
    K = a.shape[1]
    n = w_list[0].shape[2]
    n_w = len(w_list)
    rowblk = lambda i, p: jnp.maximum(jnp.minimum(i, p[2][0] - 1), 0)
    gs = pltpu.PrefetchScalarGridSpec(
        num_scalar_prefetch=4, grid=(n // tn, R_MOE // tm),
        in_specs=[pl.BlockSpec((tm, K), lambda j, i, *p: (rowblk(i, p), 0))]
        + [pl.BlockSpec(memory_space=pl.ANY)] * n_w,
        out_specs=pl.BlockSpec((tm, tn), lambda j, i, *p: (i, j)),
        scratch_shapes=[pltpu.VMEM((n_w, K, tn), F32), pltpu.VMEM((n_w, K, tn), BF16), pltpu.SemaphoreType.DMA(())])
    return pl.pallas_call(functools.partial(_gmm_body, n_w=n_w, tn=tn), grid_spec=gs,
                          out_shape=jax.ShapeDtypeStruct((R_MOE, n), out_dtype),
                          compiler_params=_cparams(("arbitrary", "arbitrary")), name=name)(*group, a, *w_list)


def _ada_body(c_ref, w_ref, b_ref, o_ref):
    c = c_ref[...]
    s = (c * jax.nn.sigmoid(c)).astype(BF16)
    o_ref[...] = jnp.dot(s, w_ref[...].astype(BF16), preferred_element_type=F32) + b_ref[...]


def _ada_params(cond, w_mod, b_mod):
    depth = w_mod.shape[0]
    tn = 1024
    return pl.pallas_call(
        _ada_body, grid=(depth, 6 * D // tn),
        in_specs=[pl.BlockSpec((N_COND, D), lambda l, j: (0, 0)),
                  pl.BlockSpec((None, D, tn), lambda l, j: (l, 0, j)),
                  pl.BlockSpec((None, 1, tn), lambda l, j: (l, 0, j))],
        out_specs=pl.BlockSpec((None, N_COND, tn), lambda l, j: (l, 0, j)),
        out_shape=jax.ShapeDtypeStruct((depth, N_COND, 6 * D), F32),
        compiler_params=_cparams(("arbitrary", "arbitrary")), name="ada_params",
    )(cond, w_mod, b_mod.reshape(depth, 1, 6 * D))


def _norm_body(*refs, n_x, modulate, router):
    it = iter(refs)
    x_refs = [next(it) for _ in range(n_x)]
    g_ref = next(it)
    if modulate:
        sh_ref, sc_ref = next(it), next(it)
    if router:
        wr_ref, br_ref = next(it), next(it)
    o_ref = next(it)
    pid = pl.program_id(1)
    x = _rows_value(x_refs, pid, T_CTX // TM)
    y = x * lax.rsqrt(jnp.mean(x * x, axis=-1, keepdims=True) + EPS) * g_ref[...]
    if modulate:
        y = y * (1.0 + sc_ref[...]) + sh_ref[...]
    o_ref[...] = y.astype(o_ref.dtype)
    if router:
        idx_ref, g0_ref, g1_ref, cnt_ref, carry_ref = next(it), next(it), next(it), next(it), next(it)
        logits = jnp.dot(y, wr_ref[...], preferred_element_type=F32, precision=lax.Precision.HIGHEST) + br_ref[...]
        lane = lax.broadcasted_iota(I32, logits.shape, 1)
        logits = jnp.where(lane < N_EXP, logits, -jnp.inf)
        lanef = lane.astype(F32)
        m1 = jnp.max(logits, axis=-1, keepdims=True)
        i1 = jnp.min(jnp.where(logits == m1, lanef, float(LANES)), axis=-1, keepdims=True)
        rest = jnp.where(lanef == i1, -jnp.inf, logits)
        m2 = jnp.max(rest, axis=-1, keepdims=True)
        i2 = jnp.min(jnp.where(rest == m2, lanef, float(LANES)), axis=-1, keepdims=True)
        e21 = jnp.exp(m2 - m1)
        gate1 = 1.0 / (1.0 + e21)
        g0_ref[...] = jnp.broadcast_to(gate1, logits.shape)
        g1_ref[...] = jnp.broadcast_to(e21 * gate1, logits.shape)

        @pl.when(pid == 0)
        def _():
            carry_ref[...] = jnp.zeros_like(carry_ref)

        hot1 = jnp.where(lanef == i1, 1.0, 0.0)
        hot2 = jnp.where(lanef == i2, 1.0, 0.0)
        n = logits.shape[0]
        tri = jnp.where(lax.broadcasted_iota(I32, (n, n), 1) < lax.broadcasted_iota(I32, (n, n), 0), 1.0, 0.0)
        tri = tri.astype(BF16)
        before1 = jnp.dot(tri, hot1.astype(BF16), preferred_element_type=F32)
        before2 = jnp.dot(tri, hot2.astype(BF16), preferred_element_type=F32)
        tot1 = jnp.sum(hot1, axis=0, keepdims=True)
        tot2 = jnp.sum(hot2, axis=0, keepdims=True)
        carry = carry_ref[...]
        rank1 = jnp.sum(hot1 * (before1 + carry), axis=-1, keepdims=True)
        rank2 = jnp.sum(hot2 * (before2 + carry + tot1), axis=-1, keepdims=True)
        carry = carry + tot1 + tot2
        carry_ref[...] = carry
        cnt_ref[...] = jnp.broadcast_to(carry, cnt_ref.shape).astype(I32)
        packed = jnp.where(lane == 0, i1, jnp.where(lane == 1, i2, jnp.where(lane == 2, rank1,
                                                                             jnp.where(lane == 3, rank2, 0.0))))
        idx_ref[...] = packed.astype(I32)


def _norm(x, g, *, mods=None, slots=None, router=None, out_dtype, name, row0=0, rows=None):
    pair = isinstance(x, tuple)
    rows = (T if pair else x.shape[0]) if rows is None else rows
    rb0 = row0 // TM
    modulate = mods is not None
    in_specs, ins = _rows_specs(x, TM, D, lambda j: 0, clamp=lambda i, p: rb0 + i)
    n_x = len(ins)
    ins.append(g.reshape(1, D))
    in_specs.append(pl.BlockSpec((1, D), lambda j, i: (0, 0)))
    if modulate:
        for slot in slots:
            ins.append(mods)
            in_specs.append(pl.BlockSpec((None, None, 1, D),
                                         lambda j, i, slot=slot: (slot, _cond_of_block(i, TM), 0, 0)))
    out_shape = [jax.ShapeDtypeStruct((rows, D), out_dtype)]
    out_specs = [pl.BlockSpec((TM, D), lambda j, i: (i, 0))]
    scratch = []
    if router is not None:
        w_r, b_r = router
        ins += [jnp.pad(w_r, ((0, 0), (0, LANES - N_EXP))), jnp.pad(b_r, (0, LANES - N_EXP)).reshape(1, LANES)]
        in_specs += [pl.BlockSpec((D, LANES), lambda j, i: (0, 0)), pl.BlockSpec((1, LANES), lambda j, i: (0, 0))]
        out_shape += [jax.ShapeDtypeStruct((rows, LANES), I32), jax.ShapeDtypeStruct((rows, LANES), F32),
                      jax.ShapeDtypeStruct((rows, LANES), F32), jax.ShapeDtypeStruct((SUBLANES, LANES), I32)]
        out_specs += [pl.BlockSpec((TM, LANES), lambda j, i: (i, 0))] * 3
        out_specs += [pl.BlockSpec((SUBLANES, LANES), lambda j, i: (0, 0))]
        scratch = [pltpu.VMEM((1, LANES), F32)]
    body = functools.partial(_norm_body, n_x=n_x, modulate=modulate, router=router is not None)
    res = pl.pallas_call(body, grid=(1, rows // TM), in_specs=in_specs, out_specs=out_specs, out_shape=out_shape,
                         scratch_shapes=scratch, compiler_params=_cparams(("arbitrary", "arbitrary")), name=name)(*ins)
    return res if router is not None else res[0]


def _swap32(x):
    up = jnp.concatenate([x[:, 32:], x[:, :32]], axis=1)
    down = jnp.concatenate([x[:, 96:], x[:, :96]], axis=1)
    lane = lax.broadcasted_iota(I32, x.shape, 1)
    return jnp.where((lane % 64) < 32, up, down)


def _qkprep_body(q_ref, kv_ref, cos_ref, sin_ref, qo_ref, ko_ref, vo_ref):
    cos, sin = cos_ref[...], sin_ref[...]
    for h in range(N_HEADS):
        x = q_ref[:, h * HD:(h + 1) * HD]
        qo_ref[:, h * HD:(h + 1) * HD] = (x * cos + _swap32(x) * sin).astype(BF16)
    for h in range(N_KV):
        x = kv_ref[:, h * HD:(h + 1) * HD]
        ko_ref[:, h * HD:(h + 1) * HD] = (x * cos + _swap32(x) * sin).astype(BF16)
    vo_ref[...] = kv_ref[:, D_KV:].astype(BF16)


def _qk_prep(q, kv, cos_t, sin_t):
    return pl.pallas_call(
        _qkprep_body, grid=(T // TM,),
        in_specs=[pl.BlockSpec((TM, D_ATTN), lambda i: (i, 0)), pl.BlockSpec((TM, 2 * D_KV), lambda i: (i, 0)),
                  pl.BlockSpec((TM, HD), lambda i: (i, 0)), pl.BlockSpec((TM, HD), lambda i: (i, 0))],
        out_specs=[pl.BlockSpec((TM, D_ATTN), lambda i: (i, 0)), pl.BlockSpec((TM, D_KV), lambda i: (i, 0)),
                   pl.BlockSpec((TM, D_KV), lambda i: (i, 0))],
        out_shape=[jax.ShapeDtypeStruct((T, D_ATTN), BF16), jax.ShapeDtypeStruct((T, D_KV), BF16),
                   jax.ShapeDtypeStruct((T, D_KV), BF16)],
        compiler_params=_cparams(("arbitrary",)), name="qk_prep")(q, kv, cos_t, sin_t)


def _attn_core(q_ref, o_ref, sink_ref, kvh, kall, vall, mask):
    scale = HD ** -0.5
    for g in range(Q_PER_KV):
        head = kvh * Q_PER_KV + g
        qh = q_ref[:, head * HD:(head + 1) * HD]
        s = lax.dot_general(qh, kall, (((1,), (1,)), ((), ())), preferred_element_type=F32) * scale
        if mask is not None:
            s = jnp.where(mask, s, NEG_INF)
        sk = sink_ref[head]
        m = jnp.maximum(jnp.max(s, axis=-1, keepdims=True), sk)
        p = jnp.exp(s - m)
        denom = jnp.sum(p, axis=-1, keepdims=True) + jnp.exp(sk - m)
        o = jnp.dot(p.astype(BF16), vall, preferred_element_type=F32) / denom
        o_ref[:, head * HD:(head + 1) * HD] = o.astype(o_ref.dtype)


def _attn_ctx_body(sink_ref, q_ref, k_ref, v_ref, o_ref):
    for kvh in range(N_KV):
        cols = slice(kvh * HD, (kvh + 1) * HD)
        _attn_core(q_ref, o_ref, sink_ref, kvh, k_ref[:, cols], v_ref[:, cols], None)


def _attn_lat_body(sink_ref, q_ref, kp_ref, kc_ref, kn_ref, vp_ref, vc_ref, vn_ref, ck_ref, cv_ref, o_ref):
    qb = pl.program_id(1)
    nk = 3 * WINDOW + ck_ref.shape[0]
    qpos = qb * WINDOW + lax.broadcasted_iota(I32, (WINDOW, nk), 0)
    col = lax.broadcasted_iota(I32, (WINDOW, nk), 1)
    kpos = (qb - 1) * WINDOW + col
    in_win = (jnp.abs(kpos - qpos) <= WINDOW) & (kpos >= 0) & (kpos < L_LAT)
    mask = in_win | (col >= 3 * WINDOW)
    for kvh in range(N_KV):
        cols = slice(kvh * HD, (kvh + 1) * HD)
        kall = jnp.concatenate([kp_ref[:, cols], kc_ref[:, cols], kn_ref[:, cols], ck_ref[:, cols].astype(BF16)], axis=0)
        vall = jnp.concatenate([vp_ref[:, cols], vc_ref[:, cols], vn_ref[:, cols], cv_ref[:, cols].astype(BF16)], axis=0)
        _attn_core(q_ref, o_ref, sink_ref, kvh, kall, vall, mask)


def _attention(qb, kb, vb, cache_k, cache_v, sink):
    smem = pl.BlockSpec(memory_space=pltpu.SMEM)
    o_ctx = pl.pallas_call(
        _attn_ctx_body, grid=(B_CTX,),
        in_specs=[smem, pl.BlockSpec((L_CTX, D_ATTN), lambda b: (b, 0)),
                  pl.BlockSpec((L_CTX, D_KV), lambda b: (b, 0)), pl.BlockSpec((L_CTX, D_KV), lambda b: (b, 0))],
        out_specs=pl.BlockSpec((L_CTX, D_ATTN), lambda b: (b, 0)),
        out_shape=jax.ShapeDtypeStruct((T_CTX, D_ATTN), BF16),
        compiler_params=_cparams(("arbitrary",)), name="attn_ctx")(sink, qb, kb, vb)
    nb = L_LAT // WINDOW
    base = T_CTX // WINDOW
    cur = lambda b, i: (base + b * nb + i, 0)
    prv = lambda b, i: (base + b * nb + jnp.maximum(i - 1, 0), 0)
    nxt = lambda b, i: (base + b * nb + jnp.minimum(i + 1, nb - 1), 0)
    blk = lambda f: pl.BlockSpec((WINDOW, D_KV), f)
    cspec = pl.BlockSpec((None, cache_k.shape[1], D_KV), lambda b, i: (b, 0, 0))
    o_lat = pl.pallas_call(
        _attn_lat_body, grid=(B_LAT, nb),
        in_specs=[smem, pl.BlockSpec((WINDOW, D_ATTN), cur), blk(prv), blk(cur), blk(nxt), blk(prv), blk(cur), blk(nxt),
                  cspec, cspec],
        out_specs=pl.BlockSpec((WINDOW, D_ATTN), lambda b, i: (b * nb + i, 0)),
        out_shape=jax.ShapeDtypeStruct((T_LAT, D_ATTN), BF16),
        compiler_params=_cparams(("arbitrary", "arbitrary")), name="attn_lat",
    )(sink, qb, kb, kb, kb, vb, vb, vb, cache_k, cache_v)
    return o_ctx, o_lat


SCAN_ROWS = 2048


def _scan_pitch(L):
    return L // SUBLANES + 4


def _rglru_body(x_ref, y_ref, h0_ref, cw_ref, cb_ref, wa_ref, ba_ref, wx_ref, bx_ref, lam_ref,
                o_ref, fin_ref, a_s, b_s, p_s, h_s, *, per_chunk_seq):
    rows_total = x_ref.shape[0]
    lc = rows_total // SUBLANES
    L = lc if per_chunk_seq else rows_total
    pitch = _scan_pitch(rows_total)
    x = x_ref[...]
    row = lax.broadcasted_iota(I32, x.shape, 0)
    pos = row % L if per_chunk_seq else row
    xc = cb_ref[...] + x * cw_ref[2:3, :]
    xc = xc + jnp.where(pos >= 2, pltpu.roll(x, 2, 0), 0.0) * cw_ref[0:1, :]
    xc = xc + jnp.where(pos >= 1, pltpu.roll(x, 1, 0), 0.0) * cw_ref[1:2, :]
    xc = xc + jnp.where(pos < L - 1, pltpu.roll(x, rows_total - 1, 0), 0.0) * cw_ref[3:4, :]
    xcb = xc.astype(BF16)
    for d in range(2):
        ga = jnp.dot(xcb, wa_ref[d].astype(BF16), preferred_element_type=F32) + ba_ref[d:d + 1, :]
        gx = jnp.dot(xcb, wx_ref[d].astype(BF16), preferred_element_type=F32) + bx_ref[d:d + 1, :]
        sigmoid = lambda v: 0.5 * jnp.tanh(0.5 * v) + 0.5
        log_a = -RG_LRU_C * sigmoid(ga) * jax.nn.softplus(-lam_ref[d:d + 1, :])
        a = jnp.exp(log_a)
        b = jnp.sqrt(-jnp.tanh(log_a) * (1.0 + a * a)) * sigmoid(gx) * xc
        for s in range(SUBLANES):
            a_s[d, s * pitch:s * pitch + lc, :] = a[s * lc:(s + 1) * lc]
            b_s[d, s * pitch:s * pitch + lc, :] = b[s * lc:(s + 1) * lc]

    def step(i, carry):
        out = []
        for d, t in ((0, i), (1, lc - 1 - i)):
            rows = pl.ds(t, SUBLANES, stride=pitch)
            a = a_s[d, rows, :]
            h = a * carry[2 * d] + b_s[d, rows, :]
            p = a * carry[2 * d + 1]
            h_s[d, rows, :] = h
            p_s[d, rows, :] = p
            out += [h, p]
        return tuple(out)

    zero = jnp.zeros((SUBLANES, RNN_W), F32)
    one = jnp.ones((SUBLANES, RNN_W), F32)
    hf, pf, hb, pb = lax.fori_loop(0, lc, step, (zero, one, zero, one), unroll=4)

    r8 = lax.broadcasted_iota(I32, (SUBLANES, RNN_W), 0)

    def chunk_carry(p, h, h0, reverse):
        for k in (1, 2, 4):
            sh = SUBLANES - k if reverse else k
            m = (r8 < SUBLANES - k) if reverse else (r8 >= k)
            h = jnp.where(m, p * pltpu.roll(h, sh, 0) + h, h)
            p = jnp.where(m, p * pltpu.roll(p, sh, 0), p)
        h0 = jnp.broadcast_to(h0, (SUBLANES, RNN_W))
        state = p * h0 + h
        if reverse:
            return state, jnp.where(r8 < SUBLANES - 1, pltpu.roll(state, SUBLANES - 1, 0), h0)
        return state, jnp.where(r8 >= 1, pltpu.roll(state, 1, 0), h0)

    if per_chunk_seq:
        cf, cb = h0_ref[0], h0_ref[1]
        sf, sb = pf * cf + hf, pb * cb + hb
    else:
        sf, cf = chunk_carry(pf, hf, h0_ref[0:1, :], False)
        sb, cb = chunk_carry(pb, hb, h0_ref[1:2, :], True)

    for s in range(SUBLANES):
        rows = slice(s * pitch, s * pitch + lc)
        hsum = (h_s[0, rows, :] + p_s[0, rows, :] * cf[s:s + 1, :]) + (h_s[1, rows, :] + p_s[1, rows, :] * cb[s:s + 1, :])
        o_ref[s * lc:(s + 1) * lc, :] = (hsum * jax.nn.gelu(y_ref[s * lc:(s + 1) * lc, :])).astype(o_ref.dtype)
    if per_chunk_seq:
        fin_ref[0] = sf
        fin_ref[1] = sb
    else:
        fin_ref[0:1, :] = sf[SUBLANES - 1:SUBLANES, :]
        fin_ref[1:2, :] = sb[0:1, :]


def _rglru(xy, h0, n_seq, L, row0, conv_w, conv_b, w_a, b_a, w_x, b_x, lam, name):
    assert L in (SCAN_ROWS, SCAN_ROWS // SUBLANES)
    multi = L != SCAN_ROWS
    rb0 = row0 // SCAN_ROWS
    c0 = (xy.shape[1] - 2 * D_RNN) // RNN_W
    nb = RNN_BLOCKS
    vec = lambda r: pl.BlockSpec((r, RNN_W), lambda b, n: (0, n))
    wsp = pl.BlockSpec((2, None, RNN_W, RNN_W), lambda b, n: (0, n, 0, 0))
    if multi:
        h0 = h0.transpose(1, 0, 2)
        state = pl.BlockSpec((2, SUBLANES, RNN_W), lambda b, n: (0, b, n))
        state_shape = (2, n_seq, D_RNN)
    else:
        state = pl.BlockSpec((None, 2, RNN_W), lambda b, n: (b, 0, n))
        state_shape = (n_seq, 2, D_RNN)
    out, fin = pl.pallas_call(
        functools.partial(_rglru_body, per_chunk_seq=multi), grid=(n_seq * L // SCAN_ROWS, nb),
        in_specs=[pl.BlockSpec((SCAN_ROWS, RNN_W), lambda b, n: (rb0 + b, c0 + n)),
                  pl.BlockSpec((SCAN_ROWS, RNN_W), lambda b, n: (rb0 + b, c0 + nb + n)),
                  state, vec(4), vec(1), wsp, vec(2), wsp, vec(2), vec(2)],
        out_specs=[pl.BlockSpec((SCAN_ROWS, RNN_W), lambda b, n: (b, n)), state],
        out_shape=[jax.ShapeDtypeStruct((n_seq * L, D_RNN), BF16), jax.ShapeDtypeStruct(state_shape, F32)],
        scratch_shapes=[pltpu.VMEM((2, SUBLANES * _scan_pitch(SCAN_ROWS), RNN_W), F32)] * 4,
        compiler_params=_cparams(("arbitrary", "arbitrary")), name=name,
    )(xy, xy, h0, conv_w, conv_b.reshape(1, D_RNN), w_a, b_a, w_x, b_x, lam)
    return out, (fin.transpose(1, 0, 2) if multi else fin)


def _hy_filter_body(fv_ref, w1_ref, b1_ref, w2_ref, b2_ref, fr_ref, w3_ref, dl_ref, o_ref, hid_s, tt_s, *, L):
    hp = lax.Precision.HIGHEST
    rowi = lax.broadcasted_iota(I32, (L, LANES), 0)
    lane = lax.broadcasted_iota(I32, (L, LANES), 1)

    def features(pos):
        posf = pos.astype(F32)
        tt = posf / (L - 1)
        ang = fv_ref[...] * (2.0 * math.pi * posf / L)
        z = jnp.where(lane == 0, tt, jnp.where(lane <= 16, jnp.cos(ang), jnp.where(lane <= 32, -jnp.sin(ang), 0.0)))
        return z, tt[:, 0:1]

    is_bwd = pl.program_id(0) == 1

    @pl.when(pl.program_id(1) == 0)
    def _():
        z, tt = features(jnp.where(is_bwd, L - rowi, rowi))
        h = jnp.sin(fr_ref[0:1, :] * (jnp.dot(z, w1_ref[...], preferred_element_type=F32, precision=hp) + b1_ref[...]))
        h = jnp.sin(fr_ref[1:2, :] * (jnp.dot(h, w2_ref[...], preferred_element_type=F32, precision=hp) + b2_ref[...]))
        hid_s[...] = h
        tt_s[...] = jnp.broadcast_to(tt, tt_s.shape)

    filt = jnp.dot(hid_s[...], w3_ref[...], preferred_element_type=F32, precision=hp)
    filt = filt * jnp.exp(-tt_s[:, 0:1] * dl_ref[...])
    dead = is_bwd & (lax.broadcasted_iota(I32, filt.shape, 0) == 0)
    o_ref[...] = jnp.where(dead, 0.0, filt).astype(o_ref.dtype)


def _hy_filter(L, f_w1, f_b1, f_w2, f_b2, f_freq, f_w3):
    bands = (HY_EMB - 1) // 2
    f = jnp.linspace(1e-4, bands - 1, bands, dtype=F32)
    fv = jnp.zeros((LANES,), F32).at[1:1 + bands].set(f).at[1 + bands:1 + 2 * bands].set(f).reshape(1, LANES)
    padw = lambda w, r, c: jnp.pad(w.astype(F32), ((0, r - w.shape[0]), (0, c - w.shape[1])))
    padv = lambda v: jnp.pad(v.astype(F32), (0, LANES - v.shape[0])).reshape(1, LANES)
    w1, w2 = padw(f_w1, LANES, LANES), padw(f_w2, LANES, LANES)
    w3 = padw(f_w3, LANES, f_w3.shape[1])
    fr = jnp.pad(f_freq.astype(F32), ((0, 0), (0, LANES - HY_W)))
    deltas = jnp.abs(jnp.linspace(HY_MIN_DECAY, HY_MAX_DECAY, D, dtype=F32)).reshape(1, D)
    tn = 1024
    per = D // tn
    full = lambda r: pl.BlockSpec((r, LANES), lambda d, j: (0, 0))
    col = lambda d, j: ((j // per) * 2 + d) * per + j % per
    return pl.pallas_call(
        functools.partial(_hy_filter_body, L=L), grid=(2, 2 * per),
        in_specs=[full(1), full(LANES), full(1), full(LANES), full(1), full(2),
                  pl.BlockSpec((LANES, tn), lambda d, j: (0, col(d, j))),
                  pl.BlockSpec((1, tn), lambda d, j: (0, j % per))],
        out_specs=pl.BlockSpec((L, tn), lambda d, j: (0, col(d, j))),
        out_shape=jax.ShapeDtypeStruct((L, 4 * D), BF16),
        scratch_shapes=[pltpu.VMEM((L, LANES), F32), pltpu.VMEM((L, LANES), F32)],
        compiler_params=_cparams(("arbitrary", "arbitrary")), name=f"hy_filter_{L}",
    )(fv, w1, padv(f_b1), w2, padv(f_b2), fr, w3, deltas)


DFT_RB = 64


def _dft_body(ca_ref, sa_ref, cb_ref, sb_ref, cf_ref, ci_ref, *, L):
    n = 2 * L
    ca, sa, cb, sb = ca_ref[...], sa_ref[...], cb_ref[...], sb_ref[...]
    cos = ca * cb - sa * sb
    sin = sa * cb + ca * sb
    r = pl.program_id(0) * DFT_RB + lax.broadcasted_iota(I32, cos.shape, 0)
    j = lax.broadcasted_iota(I32, cos.shape, 1)
    alt_j = jnp.where(j % 2 == 0, 1.0, -1.0)
    alt_r = jnp.where(r % 2 == 0, 1.0, -1.0)
    cf_ref[0] = cos.astype(BF16)
    cf_ref[1] = jnp.where(r == 0, alt_j, -sin).astype(BF16)
    w = jnp.where(j == 0, 1.0, 2.0) / n
    ci_ref[0] = (w * cos).astype(BF16)
    ci_ref[1] = jnp.where(j == 0, alt_r / n, -w * sin).astype(BF16)


def _dft_mats(L):
    n = 2 * L
    j = jnp.arange(L, dtype=I32)[None, :]
    ang = lambda r: ((r[:, None] * j) % n).astype(F32) * (2.0 * math.pi / n)
    ang_a = ang(jnp.arange(L // DFT_RB, dtype=I32) * DFT_RB)
    ang_b = ang(jnp.arange(DFT_RB, dtype=I32))
    row = pl.BlockSpec((None, 1, L), lambda a: (a, 0, 0))
    full = pl.BlockSpec((DFT_RB, L), lambda a: (0, 0))
    out = pl.BlockSpec((2, DFT_RB, L), lambda a: (0, a, 0))
    return pl.pallas_call(
        functools.partial(_dft_body, L=L), grid=(L // DFT_RB,), in_specs=[row, row, full, full],
        out_specs=[out, out], out_shape=[jax.ShapeDtypeStruct((2, L, L), BF16)] * 2,
        compiler_params=_cparams(("arbitrary",)), name=f"dft_tables_{L}",
    )(jnp.cos(ang_a)[:, None, :], jnp.sin(ang_a)[:, None, :], jnp.cos(ang_b), jnp.sin(ang_b))


def _spec_body(cf_ref, c1_ref, c2_ref, o_ref):
    f = pl.program_id(1)
    h = cf_ref.shape[1]
    k = f * h + lax.broadcasted_iota(I32, (h, c1_ref.shape[1]), 0)
    odd = k % 2 == 1
    for part in range(2):
        z1 = jnp.dot(cf_ref[part], c1_ref[...], preferred_element_type=F32)
        z2 = jnp.dot(cf_ref[part], c2_ref[...], preferred_element_type=F32)
        flip = odd if part == 0 else odd & (k != 0)
        o_ref[part * h:(part + 1) * h, :] = z1 + jnp.where(flip, -z2, z2)


def _hy_spectra(filt, cf, h):
    L = cf.shape[1]
    nf = L // h
    td = HY_TD
    per = D // td
    return pl.pallas_call(
        _spec_body, grid=(2 * per, nf),
        in_specs=[pl.BlockSpec((2, h, L), lambda c, f: (0, f, 0)),
                  pl.BlockSpec((L, td), lambda c, f: (0, (c // per) * 2 * per + c % per)),
                  pl.BlockSpec((L, td), lambda c, f: (0, (c // per) * 2 * per + per + c % per))],
        out_specs=pl.BlockSpec((None, 2 * h, td), lambda c, f: (f, 0, c)),
        out_shape=jax.ShapeDtypeStruct((nf, 2 * h, 2 * D), F32),
        compiler_params=_cparams(("arbitrary", "arbitrary")), name=f"hy_spectra_{L}")(cf, filt, filt)


HY_TC = 512
HY_TR = 2048


def _hy_conv(u_ref, cw_ref, cb_ref, L):
    u = u_ref[...]
    rows = u.shape[0]
    pos = lax.broadcasted_iota(I32, u.shape, 0) % L
    uc = cb_ref[...] + u * cw_ref[1:2, :]
    uc = uc + jnp.where(pos >= 1, pltpu.roll(u, 1, 0), 0.0) * cw_ref[0:1, :]
    return uc + jnp.where(pos < L - 1, pltpu.roll(u, rows - 1, 0), 0.0) * cw_ref[2:3, :]


def _hy_third_specs(third, row0, L):
    tr = max(L, HY_TR)
    per = D // HY_TC
    col = lambda i, c: third * per + c
    return [pl.BlockSpec((tr, HY_TC), lambda i, c: (row0 // tr + i, col(i, c))),
            pl.BlockSpec((3, HY_TC), lambda i, c: (0, col(i, c))),
            pl.BlockSpec((1, HY_TC), lambda i, c: (0, col(i, c)))]


def _hy_prep_body(u_ref, cw_ref, cb_ref, zb_ref, *, L):
    zb_ref[...] = _hy_conv(u_ref, cw_ref, cb_ref, L).astype(BF16)


def _hy_prep(u, conv_w, conv_b, row0, rows, L):
    tr = max(L, HY_TR)
    return pl.pallas_call(
        functools.partial(_hy_prep_body, L=L), grid=(rows // tr, D // HY_TC),
        in_specs=_hy_third_specs(0, row0, L), out_specs=pl.BlockSpec((tr, HY_TC), lambda i, c: (i, c)),
        out_shape=jax.ShapeDtypeStruct((rows, D), BF16),
        compiler_params=_cparams(("arbitrary", "arbitrary")), name=f"hy_prep_{L}",
    )(u, conv_w, conv_b.reshape(1, 3 * D))


def _longconv_body(z_ref, cf_ref, ci_ref, s_ref, o_ref, acc_ref):
    f = pl.program_id(2)
    h = cf_ref.shape[1]
    z = z_ref[...]
    zre = jnp.dot(cf_ref[0], z, preferred_element_type=F32)
    zim = jnp.dot(cf_ref[1], z, preferred_element_type=F32)
    sre, sim = s_ref[:h, :], s_ref[h:, :]
    first = (lax.broadcasted_iota(I32, zre.shape, 0) == 0) & (f == 0)
    yre = zre * sre - jnp.where(first, 0.0, zim * sim)
    yim = jnp.where(first, zim * sim, zre * sim + zim * sre)
    contrib = (jnp.dot(ci_ref[0], yre.astype(BF16), preferred_element_type=F32)
               + jnp.dot(ci_ref[1], yim.astype(BF16), preferred_element_type=F32))

    @pl.when(f == 0)
    def _():
        acc_ref[...] = contrib

    @pl.when(f > 0)
    def _():
        acc_ref[...] += contrib

    @pl.when(f == pl.num_programs(2) - 1)
    def _():
        o_ref[...] = acc_ref[...]


def _longconv(z, n_seq, L, cf, ci, spec, order):
    nf, fc, _ = spec.shape
    h = fc // 2
    td = HY_TD if L > TM else D
    per = D // td
    return pl.pallas_call(
        _longconv_body, grid=(n_seq, per, nf),
        in_specs=[pl.BlockSpec((L, td), lambda b, c, f: (b, c)),
                  pl.BlockSpec((2, h, L), lambda b, c, f: (0, f, 0)),
                  pl.BlockSpec((2, L, h), lambda b, c, f: (0, 0, f)),
                  pl.BlockSpec((None, fc, td), lambda b, c, f: (f, 0, order * per + c))],
        out_specs=pl.BlockSpec((L, td), lambda b, c, f: (b, c)),
        out_shape=jax.ShapeDtypeStruct((n_seq * L, D), F32),
        scratch_shapes=[pltpu.VMEM((L, td), F32)],
        compiler_params=_cparams(("arbitrary", "arbitrary", "arbitrary")), name=f"longconv_{L}_{order}",
    )(z, cf, ci, spec)


def _hy_gate_body(zc_ref, b_ref, *refs, L, first):
    if first:
        z = _hy_conv(*refs[0:3], L)
        gate_refs, out_refs = refs[3:6], refs[6:]
    else:
        z = refs[0][...]
        gate_refs, out_refs = refs[1:4], refs[4:]
    z = _hy_conv(*gate_refs, L) * (zc_ref[...] + z * b_ref[...])
    for o_ref in out_refs:
        o_ref[...] = z.astype(o_ref.dtype)


def _hy_gate(zc, z, bias, u, conv_w, conv_b, order, row0, L):
    rows = zc.shape[0]
    tr = max(L, HY_TR)
    blk = pl.BlockSpec((tr, HY_TC), lambda i, c: (i, c))
    first = z is None
    uargs = (u, conv_w, conv_b.reshape(1, 3 * D))
    ins = [zc, bias.reshape(1, D)] + (list(uargs) if first else [z]) + list(uargs)
    in_specs = ([blk, pl.BlockSpec((1, HY_TC), lambda i, c: (0, c))]
                + (_hy_third_specs(0, row0, L) if first else [blk]) + _hy_third_specs(order + 1, row0, L))
    dts = ([F32] if first else []) + [BF16]
    return pl.pallas_call(
        functools.partial(_hy_gate_body, L=L, first=first), grid=(rows // tr, D // HY_TC),
        in_specs=in_specs, out_specs=[blk] * len(dts),
        out_shape=[jax.ShapeDtypeStruct((rows, D), dt) for dt in dts],
        compiler_params=_cparams(("arbitrary", "arbitrary")), name=f"hy_gate_{L}_{order}",
    )(*ins)


DISPATCH_ROWS = 512
COMBINE_ROWS = 256
DMA_UNROLL = 8


def _dispatch_body(d0_ref, d1_ref, zs_ref, h_ref, o_ref, zbuf, sem, zsem):
    base = pl.program_id(0) * DISPATCH_ROWS

    @pl.when(pl.program_id(0) == 0)
    def _():
        zbuf[...] = jnp.zeros_like(zbuf)
        for k in range(2 * N_EXP):
            fill = pltpu.make_async_copy(zbuf, o_ref.at[pl.ds(zs_ref[k], MOE_TM)], zsem)
            fill.start()
            fill.wait()

    def issue(q, c):
        for u in range(DMA_UNROLL):
            r = q * DMA_UNROLL + u
            pltpu.make_async_copy(h_ref.at[r], o_ref.at[d0_ref[base + r]], sem).start(priority=0)
            pltpu.make_async_copy(h_ref.at[r], o_ref.at[d1_ref[base + r]], sem).start(priority=1)
        return c

    lax.fori_loop(0, DISPATCH_ROWS // DMA_UNROLL, issue, 0)
    for _ in range(2):
        pltpu.make_async_copy(h_ref, h_ref, sem).wait()


def _moe_dispatch(h, dest0, dest1, zero_starts):
    sub = D // LANES
    gs = pltpu.PrefetchScalarGridSpec(
        num_scalar_prefetch=3, grid=(T // DISPATCH_ROWS,),
        in_specs=[pl.BlockSpec((DISPATCH_ROWS, sub, LANES), lambda i, *p: (i, 0, 0))],
        out_specs=pl.BlockSpec(memory_space=pl.ANY),
        scratch_shapes=[pltpu.VMEM((MOE_TM, sub, LANES), BF16), pltpu.SemaphoreType.DMA(()),
                        pltpu.SemaphoreType.DMA(())])
    out = pl.pallas_call(_dispatch_body, grid_spec=gs, out_shape=jax.ShapeDtypeStruct((R_MOE, sub, LANES), BF16),
                         compiler_params=_cparams(("arbitrary",)), name="moe_dispatch",
                         )(dest0, dest1, zero_starts, h.reshape(T, sub, LANES))
    return out.reshape(R_MOE, D)


def _combine_body(d0_ref, d1_ref, y_ref, x_ref, g0_ref, g1_ref, gate_ref, fg_ref, oc_ref, ol_ref, buf, sems):
    i = pl.program_id(0)

    def issue(step, slot):
        base = step * COMBINE_ROWS

        def body(q, c):
            for u in range(DMA_UNROLL):
                r = q * DMA_UNROLL + u
                pltpu.make_async_copy(y_ref.at[pl.ds(d0_ref[base + r], 1), :], buf.at[slot, 0, pl.ds(r, 1), :],
                                      sems.at[slot]).start(priority=0)
                pltpu.make_async_copy(y_ref.at[pl.ds(d1_ref[base + r], 1), :], buf.at[slot, 1, pl.ds(r, 1), :],
                                      sems.at[slot]).start(priority=1)
            return c

        lax.fori_loop(0, COMBINE_ROWS // DMA_UNROLL, body, 0)

    slot = i % 2

    @pl.when(i == 0)
    def _():
        issue(0, 0)

    @pl.when(i + 1 < pl.num_programs(0))
    def _():
        issue(i + 1, 1 - slot)

    pltpu.make_async_copy(buf.at[slot], buf.at[slot], sems.at[slot]).wait()
    y = g0_ref[:, 0:1] * buf[slot, 0] + g1_ref[:, 0:1] * buf[slot, 1]
    x = x_ref[...] + gate_ref[...] * y
    out = x * lax.rsqrt(jnp.mean(x * x, axis=-1, keepdims=True) + EPS) * fg_ref[...]
    nbc = T_CTX // COMBINE_ROWS

    @pl.when(i < nbc)
    def _():
        oc_ref[...] = out

    @pl.when(i >= nbc)
    def _():
        ol_ref[...] = out


def _moe_combine_norm(ybuf, dest0, dest1, g0, g1, x, mods, slot, final_g):
    nbc = T_CTX // COMBINE_ROWS
    tok = pl.BlockSpec((COMBINE_ROWS, D), lambda i, *p: (i, 0))
    gsp = pl.BlockSpec((COMBINE_ROWS, LANES), lambda i, *p: (i, 0))
    gs = pltpu.PrefetchScalarGridSpec(
        num_scalar_prefetch=2, grid=(T // COMBINE_ROWS,),
        in_specs=[pl.BlockSpec(memory_space=pl.ANY), tok, gsp, gsp,
                  pl.BlockSpec((None, None, 1, D), lambda i, *p: (slot, _cond_of_block(i, COMBINE_ROWS), 0, 0)),
                  pl.BlockSpec((1, D), lambda i, *p: (0, 0))],
        out_specs=[pl.BlockSpec((COMBINE_ROWS, D), lambda i, *p: (jnp.minimum(i, nbc - 1), 0)),
                   pl.BlockSpec((COMBINE_ROWS, D), lambda i, *p: (jnp.maximum(i - nbc, 0), 0))],
        scratch_shapes=[pltpu.VMEM((2, 2, COMBINE_ROWS, D), F32), pltpu.SemaphoreType.DMA((2,))])
    return pl.pallas_call(_combine_body, grid_spec=gs,
                          out_shape=[jax.ShapeDtypeStruct((T_CTX, D), F32), jax.ShapeDtypeStruct((T_LAT, D), F32)],
                          compiler_params=_cparams(("arbitrary",)), name="moe_combine_norm",
                          )(dest0, dest1, ybuf, x, g0, g1, mods, final_g.reshape(1, D))


def _moe_plan(idx, cnt):
    counts = cnt[0, :N_EXP]
    padded = (counts + MOE_TM - 1) // MOE_TM * MOE_TM
    p_ends = jnp.cumsum(padded)
    p_starts = p_ends - padded
    experts = jnp.arange(N_EXP, dtype=I32)[None, :]

    def dest(e, rank):
        return jnp.sum(jnp.where(e[:, None] == experts, p_starts[None, :], 0), axis=1) + rank

    def groups(tm):
        nblk = R_MOE // tm
        blk_start = jnp.arange(nblk, dtype=I32) * tm
        gid = jnp.minimum(jnp.sum((blk_start[:, None] >= p_ends[None, :]).astype(I32), axis=1), N_EXP - 1)
        nact = (p_ends[-1] // tm).astype(I32).reshape(1)
        gid = jnp.where(jnp.arange(nblk) < nact[0], gid, gid[jnp.maximum(nact[0] - 1, 0)])
        chg = jnp.concatenate([jnp.ones((1,), I32), (gid[1:] != gid[:-1]).astype(I32)])
        blk = jnp.arange(nblk, dtype=I32)
        starts = (chg == 1) & (blk < nact[0])
        later = jnp.where(starts[None, :] & (blk[None, :] > blk[:, None]), blk[None, :], nblk)
        nxt = jnp.min(later, axis=1)
        return chg, gid, nact, jnp.where(nxt == nblk, -1, nxt).astype(I32)

    tail = jnp.minimum(p_ends[-1] + jnp.arange(N_EXP, dtype=I32) * MOE_TM, R_MOE - MOE_TM)
    zero_starts = jnp.concatenate([jnp.maximum(p_ends - MOE_TM, 0), tail]).astype(I32)
    return dest(idx[:, 0], idx[:, 2]), dest(idx[:, 1], idx[:, 3]), groups, zero_starts


def _rope_tables():
    quarter = HD // 4
    inv_freq = ROPE_BASE ** (-jnp.arange(quarter, dtype=F32) / quarter)
    t = jnp.arange(L_LAT)
    row = (t // GRID_W).astype(F32)[:, None] * inv_freq
    col = (t % GRID_W).astype(F32)[:, None] * inv_freq
    ang = jnp.concatenate([row, row, col, col], axis=1)
    sign = jnp.tile(jnp.concatenate([-jnp.ones((quarter,), F32), jnp.ones((quarter,), F32)]), 2)
    cos = jnp.concatenate([jnp.ones((T_CTX, HD), F32), jnp.tile(jnp.cos(ang), (B_LAT, 1))], axis=0)
    sin = jnp.concatenate([jnp.zeros((T_CTX, HD), F32), jnp.tile(jnp.sin(ang) * sign, (B_LAT, 1))], axis=0)
    return cos, sin


def kernel(x_prompt, x_sample, cache_k, cache_v, state_rglru, c, c_ctx, w_mod, b_mod, norm_g, final_g, a_w_in, a_w_out, rnn_conv_w, rnn_conv_b, rnn_w_a, rnn_b_a, rnn_w_x, rnn_b_x, rnn_lam, attn_sink, ffn_w1, ffn_w3, ffn_w2, h_w_in, h_w_out, h_conv_w, h_conv_b, hf_w1, hf_b1, hf_w2, hf_b2, hf_freq, hf_w3, h_bias, moe_router, moe_router_b, moe_w_gate, moe_w_up, moe_w_down):
    x = (x_prompt.reshape(T_CTX, D), x_sample.reshape(T_LAT, D))
    cond = jnp.concatenate([c_ctx[None, :], c, jnp.zeros((N_COND - 1 - B_LAT, D), F32)], axis=0)
    mods_all = _ada_params(cond, w_mod, b_mod)
    mods_all = mods_all.reshape(-1, N_COND, 6, D).transpose(0, 2, 1, 3).reshape(-1, 6, N_COND, 1, D)
    cos_t, sin_t = _rope_tables()

    mods = mods_all[0]
    h = _norm(x, norm_g[0, 0], mods=mods, slots=(0, 1), out_dtype=BF16, name="norm_mix0")
    w_in = a_w_in[0]
    mm1 = functools.partial(_mm, [h], [w_in], [0], tm=2 * TM)
    q = mm1(n_cols=D_ATTN, col_off=0, tn=1024, out_dtype=F32, name="proj_q")
    kv = xy = _mm([h], [w_in[:, D_ATTN:]], [0], n_cols=2 * D_KV + 2 * D_RNN, col_off=0, tn=1280, tm=TM,
                  out_dtype=F32, name="proj_kv_rnn")
    qb, kb, vb = _qk_prep(q, kv, cos_t, sin_t)
    ck = cache_k[:, 0].reshape(B_LAT, -1, D_KV)
    cv = cache_v[:, 0].reshape(B_LAT, -1, D_KV)
    o_ctx, o_lat = _attention(qb, kb, vb, ck, cv, attn_sink[0])
    rnn_w = (rnn_conv_w[0], rnn_conv_b[0], rnn_w_a[0], rnn_b_a[0], rnn_w_x[0], rnn_b_x[0], rnn_lam[0])
    r_ctx, s_ctx = _rglru(xy, jnp.zeros((B_CTX, 2, D_RNN), F32), B_CTX, L_CTX, 0, *rnn_w, name="rglru_ctx")
    r_lat, _ = _rglru(xy, state_rglru[:, 0], B_LAT, L_LAT, T_CTX, *rnn_w, name="rglru_lat")
    w_out = a_w_out[0]
    x = _mm([(o_ctx, o_lat), (r_ctx, r_lat)], [w_out, w_out], [0, 1], n_cols=D, col_off=0, tn=1024, tm=TM,
            out_dtype=F32, residual=(x, mods, 2), name="proj_out0")
    h = _norm(x, norm_g[0, 1], mods=mods, slots=(3, 4), out_dtype=BF16, name="norm_ffn0")
    hid = _mm([h], [ffn_w1[0]], [0], dual_w=[ffn_w3[0]], n_cols=D_FF, col_off=0, tn=512, tm=2 * TM,
              out_dtype=BF16, name="ffn_up")
    x = _mm([hid], [ffn_w2[0]], [0], n_cols=D, col_off=0, tn=512, tm=TM, out_dtype=F32, residual=(x, mods, 5),
            name="ffn_down")

    mods = mods_all[1]
    h = _norm(x, norm_g[1, 0], mods=mods, slots=(0, 1), out_dtype=BF16, name="norm_mix1")
    u = _mm([h], [h_w_in[0]], [0], n_cols=3 * D, col_off=0, tn=1024, tm=2 * TM, out_dtype=F32, name="hy_in")
    zs = []
    for row0, n_seq, L in ((0, B_CTX, L_CTX), (T_CTX, B_LAT, L_LAT)):
        zb = _hy_prep(u, h_conv_w[0], h_conv_b[0], row0, n_seq * L, L)
        cf, ci = _dft_mats(L)
        filt = _hy_filter(L, hf_w1[0], hf_b1[0], hf_w2[0], hf_b2[0], hf_freq[0], hf_w3[0])
        spec = _hy_spectra(filt, cf, min(HY_FC // 2, L))
        zc = _longconv(zb, n_seq, L, cf, ci, spec, 0)
        zf, zb = _hy_gate(zc, None, h_bias[0, 0], u, h_conv_w[0], h_conv_b[0], 0, row0, L)
        zc = _longconv(zb, n_seq, L, cf, ci, spec, 1)
        (zb,) = _hy_gate(zc, zf, h_bias[0, 1], u, h_conv_w[0], h_conv_b[0], 1, row0, L)
        zs.append(zb)
    x = _mm([tuple(zs)], [h_w_out[0]], [0], n_cols=D, col_off=0, tn=1024, tm=TM, out_dtype=F32,
            residual=(x, mods, 2), name="hy_out")
    h, idx, g0, g1, cnt = _norm(x, norm_g[1, 1], mods=mods, slots=(3, 4), router=(moe_router[0], moe_router_b[0]),
                                out_dtype=BF16, name="norm_moe")
    dest0, dest1, groups, zero_starts = _moe_plan(idx, cnt)
    xs = _moe_dispatch(h, dest0, dest1, zero_starts)
    group = groups(MOE_TM)
    hid = _gmm(xs, [moe_w_gate[0], moe_w_up[0]], group, tn=1024, tm=MOE_TM, out_dtype=BF16, name="moe_up")
    ybuf = _gmm(hid, [moe_w_down[0]], group, tn=512, tm=MOE_TM, out_dtype=F32, name="moe_down")
    y_prompt, y_sample = _moe_combine_norm(ybuf, dest0, dest1, g0, g1, x, mods, 5, final_g)
    y_prompt = y_prompt.reshape(B_CTX, L_CTX, D)
    y_sample = y_sample.reshape(B_LAT, L_LAT, D)
    new_k = kv[:T_CTX, :D_KV].reshape(B_CTX, 1, L_CTX, N_KV, HD)
    new_v = kv[:T_CTX, D_KV:2 * D_KV].reshape(B_CTX, 1, L_CTX, N_KV, HD)
    new_s = s_ctx.reshape(B_CTX, 1, 2, D_RNN)
    return (y_prompt, y_sample, new_k, new_v, new_s)
```
